```python
import functools
import jax, jax.numpy as jnp
from jax import lax
import numpy as np

D_MODEL = 1024
BATCH = 4
SEQ = 4096
DEPTH = 4
DEC_BATCH = 32
DEC_SEQ = 4
PAST_LEN = 8192
PAGE_SIZE = 128

D_MIX = D_MODEL
A_WIDTH = D_MIX // 2
A_HD = 64
A_HEADS = A_WIDTH // A_HD
A_KV = 2
A_G = A_HEADS // A_KV
CMP_BLOCK = 64
TOPK = 16
WINDOW = 512
Q_BLOCK = 64
ROPE_THETA = 10000.0
FORCE = float(A_G + 1)
M_WIDTH = D_MIX - A_WIDTH
M_HEADS = 4
M_DV = M_WIDTH // M_HEADS
M_DK = M_DV // 2
M_CHUNK = 64
D_FF = ((8 * D_MODEL) // 3 + 127) // 128 * 128
CONV_W = 3
EPS = 1e-6
IN_SPLITS = (A_HEADS * A_HD, 2 * A_KV * A_HD, 2 * A_KV * A_HD, 2 * A_KV * A_HD, 3 * A_HEADS,
             M_HEADS * M_DK, M_HEADS * M_DK, M_WIDTH, M_WIDTH, 2 * M_HEADS)
IN_WIDTH = sum(IN_SPLITS)

kernel_name = 'nsa_mlstm_convffn_hybrid_step'

F32 = jnp.float32


def rmsnorm(x, g):
    xf = x.astype(F32)
    y = xf * lax.rsqrt(jnp.mean(xf * xf, axis=-1, keepdims=True) + EPS)
    return (y * g.astype(F32)).astype(x.dtype)


def adaln(c, w, b):
    mod = jax.nn.silu(c) @ w + b
    return [m[:, None, :] for m in jnp.split(mod, 6, axis=-1)]


def rope(x, pos):
    d = x.shape[-1]
    half = d // 2
    freq = ROPE_THETA ** (-2.0 * jnp.arange(half, dtype=F32) / d)
    ang = pos.astype(F32)[:, None] * freq[None, :]
    cos = jnp.cos(ang)[None, :, None, :]
    sin = jnp.sin(ang)[None, :, None, :]
    x1 = x[..., :half].astype(F32)
    x2 = x[..., half:].astype(F32)
    return jnp.concatenate([x1 * cos - x2 * sin, x1 * sin + x2 * cos], axis=-1).astype(x.dtype)


def make_kv(z, pos):
    bt, t = z.shape[:2]
    kv = z.reshape(bt, t, 2, A_KV, A_HD)
    return jnp.stack([rope(kv[:, :, 0], pos), kv[:, :, 1]], axis=2)


def block_view(kv):
    bt, length = kv.shape[:2]
    nb = -(-length // CMP_BLOCK)
    kv = jnp.pad(kv, ((0, 0), (0, nb * CMP_BLOCK - length), (0, 0), (0, 0), (0, 0)))
    return kv.reshape(bt, nb, CMP_BLOCK, 2, A_KV, A_HD)


def cmp_means(kv):
    return block_view(kv).astype(F32).mean(axis=2)


def sel_blocks(kv):
    return block_view(kv).transpose(0, 4, 1, 2, 3, 5)


def nsa_cmp_sel(q, cm, sb, q_pos):
    bt, tq = q.shape[:2]
    nb = cm.shape[1]
    scale = A_HD ** -0.5
    qg = q.reshape(bt, tq, A_KV, A_G, A_HD)
    s = jnp.einsum('btkgd,bnkd->btkgn', qg.astype(F32), cm[:, :, 0]) * scale
    blk = jnp.arange(nb)
    cmp_ok = (blk[None, :] + 1) * CMP_BLOCK <= q_pos[:, None] + 1
    s = jnp.where(cmp_ok[None, :, None, None, :], s, -jnp.inf)
    s_max = s.max(axis=-1, keepdims=True)
    e = jnp.exp(s - jnp.where(jnp.isfinite(s_max), s_max, 0.0))
    den = e.sum(axis=-1, keepdims=True)
    p = e / jnp.where(den > 0, den, 1.0)
    o_cmp = jnp.einsum('btkgn,bnkd->btkgd', p, cm[:, :, 1]).reshape(bt, tq, A_HEADS, A_HD).astype(q.dtype)
    imp = p.sum(axis=3)
    cur = q_pos // CMP_BLOCK
    forced = (blk[None, :] == cur[:, None]) | (blk[None, :] == 0)
    allowed = blk[None, :] <= cur[:, None]
    score = jnp.where(forced[None, :, None, :], FORCE, jnp.where(allowed[None, :, None, :], imp, -1.0))
    top_val, top_idx = lax.top_k(score, min(TOPK, nb))
    idx_t = top_idx.transpose(0, 2, 1, 3)
    ok_blk = (top_val >= 0).transpose(0, 2, 1, 3)
    bi = jnp.arange(bt)[:, None, None, None]
    ki = jnp.arange(A_KV)[None, :, None, None]
    g = sb[bi, ki, idx_t]
    kpos = idx_t[..., None] * CMP_BLOCK + jnp.arange(CMP_BLOCK)
    ok = (kpos <= q_pos[None, None, :, None, None]) & ok_blk[..., None]
    qt = qg.transpose(0, 2, 1, 3, 4)
    s2 = jnp.einsum('bktgd,bktsjd->bktgsj', qt, g[..., 0, :]).astype(F32) * scale
    s2 = jnp.where(ok[:, :, :, None], s2, -jnp.inf)
    kk = top_idx.shape[-1]
    p2 = jax.nn.softmax(s2.reshape(bt, A_KV, tq, A_G, kk * CMP_BLOCK), axis=-1).reshape(s2.shape)
    o_sel = jnp.einsum('bktgsj,bktsjd->bktgd', p2.astype(q.dtype), g[..., 1, :])
    o_sel = o_sel.transpose(0, 2, 1, 3, 4).reshape(bt, tq, A_HEADS, A_HD)
    return o_cmp, o_sel


def window_attn(q, kv, q_pos, k_pos):
    bt, tq = q.shape[:2]
    qg = q.reshape(bt, tq, A_KV, A_G, A_HD)
    s = jnp.einsum('btkgd,bskd->btkgs', qg, kv[:, :, 0]).astype(F32) * A_HD ** -0.5
    ok = (k_pos[None, :] <= q_pos[:, None]) & (k_pos[None, :] > q_pos[:, None] - WINDOW) & (k_pos[None, :] >= 0)
    s = jnp.where(ok[None, :, None, None, :], s, -jnp.inf)
    p = jax.nn.softmax(s, axis=-1)
    o = jnp.einsum('btkgs,bskd->btkgd', p.astype(q.dtype), kv[:, :, 1])
    return o.reshape(bt, tq, A_HEADS, A_HD)


def nsa_attend_prompt(q, kv_c, kv_s, kv_w):
    bt, t = q.shape[:2]
    cm = cmp_means(kv_c)
    sb = sel_blocks(kv_s)
    kvw_pad = jnp.pad(kv_w, ((0, 0), (WINDOW, 0), (0, 0), (0, 0), (0, 0)))

    def body(i):
        start = i * Q_BLOCK
        qb = lax.dynamic_slice_in_dim(q, start, Q_BLOCK, axis=1)
        q_pos = start + jnp.arange(Q_BLOCK)
        o_c, o_s = nsa_cmp_sel(qb, cm, sb, q_pos)
        kb = lax.dynamic_slice_in_dim(kvw_pad, start, WINDOW + Q_BLOCK, axis=1)
        k_pos = start - WINDOW + jnp.arange(WINDOW + Q_BLOCK)
        return o_c, o_s, window_attn(qb, kb, q_pos, k_pos)

    outs = lax.map(body, jnp.arange(t // Q_BLOCK))
    o_c, o_s, o_w = [jnp.moveaxis(o, 0, 1).reshape(bt, t, A_HEADS, A_HD) for o in outs]
    return o_c, o_s, o_w, kv_w[:, -min(WINDOW, t):]


def nsa_attend_sample(q, kv_c, kv_s, kv_w, past_c, past_s, win_buf):
    t = q.shape[1]
    wb = win_buf.shape[1]
    q_pos = PAST_LEN + jnp.arange(t)
    full_c = jnp.concatenate([past_c.astype(kv_c.dtype), kv_c], axis=1)
    full_s = jnp.concatenate([past_s.astype(kv_s.dtype), kv_s], axis=1)
    o_c, o_s = nsa_cmp_sel(q, cmp_means(full_c), sel_blocks(full_s), q_pos)
    kw = jnp.concatenate([win_buf.astype(kv_w.dtype), kv_w], axis=1)
    k_pos = PAST_LEN - wb + jnp.arange(wb + t)
    return o_c, o_s, window_attn(q, kw, q_pos, k_pos), kw[:, -wb:]


def mlstm_chunk(state, inp):
    C, n, m = [s.astype(F32) for s in state]
    q, k, v, ig, lf = inp
    L = q.shape[1]
    b = jnp.cumsum(lf, axis=1)
    a = b + m[:, None, :]
    d = b[:, :, None, :] - b[:, None, :, :] + ig[:, None, :, :]
    causal = jnp.tril(jnp.ones((L, L), dtype=bool))
    d = jnp.where(causal[None, :, :, None], d, -jnp.inf)
    m_t = jnp.maximum(a, d.max(axis=2))
    w = jnp.einsum('btnd,bsnd->btsn', q, k) * jnp.exp(d - m_t[:, :, None, :])
    aw = jnp.exp(a - m_t)
    num = jnp.einsum('btsn,bsnv->btnv', w, v) + aw[..., None] * jnp.einsum('btnd,bndv->btnv', q, C)
    den = w.sum(axis=2) + aw * jnp.einsum('btnd,bnd->btn', q, n)
    h = num / jnp.maximum(jnp.abs(den), jnp.exp(-m_t))[..., None]
    b_end = b[:, -1]
    wl = b_end[:, None, :] - b + ig
    m_new = jnp.maximum(b_end + m, wl.max(axis=1))
    decay = jnp.exp(b_end + m - m_new)
    ws = jnp.exp(wl - m_new[:, None, :])
    C_new = decay[:, :, None, None] * C + jnp.einsum('bsn,bsnd,bsnv->bndv', ws, k, v)
    n_new = decay[:, :, None] * n + jnp.einsum('bsn,bsnd->bnd', ws, k)
    return (C_new, n_new, m_new), h


def mlstm_prompt(q, k, v, ig, lf):
    bt, t = q.shape[:2]
    nc = t // M_CHUNK

    def chunks(z):
        return jnp.moveaxis(z.reshape((bt, nc, M_CHUNK) + z.shape[2:]), 1, 0)

    init = (jnp.zeros((bt, M_HEADS, M_DK, M_DV), F32), jnp.zeros((bt, M_HEADS, M_DK), F32),
            jnp.zeros((bt, M_HEADS), F32))
    state, h = lax.scan(mlstm_chunk, init, (chunks(q), chunks(k), chunks(v), chunks(ig), chunks(lf)))
    return jnp.moveaxis(h, 0, 1).reshape(bt, t, M_HEADS, M_DV), state


def mlstm_sample(q, k, v, ig, lf, C, n, m):
    state, h = mlstm_chunk((C, n, m), (q, k, v, ig, lf))
    return h, state


def conv_ffn(xn, buf, w_up, w_conv, b_conv, w_down):
    u, gt = jnp.split(xn @ w_up, 2, axis=-1)
    t = u.shape[1]
    up = jnp.concatenate([buf.astype(u.dtype), u], axis=1)
    conv = b_conv
    for j in range(CONV_W):
        conv = conv + w_conv[j] * up[:, j:j + t]
    y = (jax.nn.silu(conv) * gt) @ w_down
    return y, up[:, -(CONV_W - 1):]


def trunk_layer(x, c, pos, conv_buf, attend, recur, lw):
    n1, n2, wm, bm, wi, bif, mng, wo, wu, wc, bc, wd = lw
    sh1, sc1, g1, sh2, sc2, g2 = adaln(c, wm, bm)
    bt, t = x.shape[:2]
    xn = rmsnorm(x, n1) * (1 + sc1) + sh1
    points = [int(s) for s in np.cumsum(IN_SPLITS)[:-1]]
    a_q, a_kc, a_ks, a_kw, a_g, m_q, m_k, m_v, m_o, m_if = jnp.split(xn @ wi, points, axis=-1)
    q = rope(a_q.reshape(bt, t, A_HEADS, A_HD), pos)
    kv_c, kv_s, kv_w = make_kv(a_kc, pos), make_kv(a_ks, pos), make_kv(a_kw, pos)
    o_c, o_s, o_w, win_state = attend(q, kv_c, kv_s, kv_w)
    gate = jax.nn.sigmoid(a_g.astype(F32)).reshape(bt, t, A_HEADS, 3, 1).astype(x.dtype)
    a_out = (gate[:, :, :, 0] * o_c + gate[:, :, :, 1] * o_s + gate[:, :, :, 2] * o_w).reshape(bt, t, A_WIDTH)
    ifg = (m_if + bif).astype(F32).reshape(bt, t, 2, M_HEADS)
    mq = m_q.astype(F32).reshape(bt, t, M_HEADS, M_DK)
    mk = m_k.astype(F32).reshape(bt, t, M_HEADS, M_DK) * M_DK ** -0.5
    mv = m_v.astype(F32).reshape(bt, t, M_HEADS, M_DV)
    h, mstate = recur(mq, mk, mv, ifg[:, :, 0], jax.nn.log_sigmoid(ifg[:, :, 1]))
    h = h * lax.rsqrt(jnp.mean(h * h, axis=-1, keepdims=True) + EPS) * mng.astype(F32).reshape(M_HEADS, M_DV)
    m_out = jax.nn.sigmoid(m_o) * h.reshape(bt, t, M_WIDTH).astype(x.dtype)
    x = x + g1 * (jnp.concatenate([a_out, m_out], axis=-1) @ wo)
    xn2 = rmsnorm(x, n2) * (1 + sc2) + sh2
    f, new_buf = conv_ffn(xn2, conv_buf, wu, wc, bc, wd)
    x = x + g2 * f
    return x, kv_c, kv_s, win_state, mstate, new_buf


def setup_inputs(seed: int = 0) -> dict:
    key = jax.random.key(seed)
    ks = jax.random.split(key, 26)

    def nrm(k, shape, s=1.0):
        return s * jax.random.normal(k, shape, F32)

    n_pages = PAST_LEN // PAGE_SIZE
    n_used = DEC_BATCH * n_pages
    n_phys = n_used + max(1, n_used // 4)
    win_buf = min(WINDOW, PAST_LEN)
    page_table = jax.random.permutation(ks[0], n_phys)[:n_used].reshape(DEC_BATCH, n_pages).astype(jnp.int32)
    kv_row = (2, A_KV, A_HD)
    kb1, kb2 = jax.random.split(ks[17])
    b_if = jnp.concatenate([nrm(kb1, (DEPTH, M_HEADS), 0.1),
                            jnp.linspace(3.0, 6.0, M_HEADS, dtype=F32)[None, :] + nrm(kb2, (DEPTH, M_HEADS), 0.1)], axis=-1)
    return {
        'x_prompt': nrm(ks[1], (BATCH, SEQ, D_MODEL)),
        'x_sample': nrm(ks[2], (DEC_BATCH, DEC_SEQ, D_MODEL)),
        'cache_cmp_kv': nrm(ks[3], (DEPTH, n_phys, PAGE_SIZE) + kv_row),
        'cache_sel_kv': nrm(ks[4], (DEPTH, n_phys, PAGE_SIZE) + kv_row),
        'cache_win_kv': nrm(ks[5], (DEPTH, DEC_BATCH, win_buf) + kv_row),
        'state_mlstm_C': nrm(ks[6], (DEPTH, DEC_BATCH, M_HEADS, M_DK, M_DV), 0.3),
        'state_mlstm_n': nrm(ks[7], (DEPTH, DEC_BATCH, M_HEADS, M_DK), 0.3),
        'state_mlstm_m': nrm(ks[8], (DEPTH, DEC_BATCH, M_HEADS), 0.5),
        'state_ffn_conv': nrm(ks[9], (DEPTH, DEC_BATCH, CONV_W - 1, D_FF)),
        'page_table': page_table,
        'c_prompt': nrm(ks[10], (BATCH, D_MODEL)),
        'c_sample': nrm(ks[11], (DEC_BATCH, D_MODEL)),
        'norm1_g': 1.0 + nrm(ks[12], (DEPTH, D_MODEL), 0.1),
        'norm2_g': 1.0 + nrm(ks[13], (DEPTH, D_MODEL), 0.1),
        'w_mod': nrm(ks[14], (DEPTH, D_MODEL, 6 * D_MODEL), 0.5 * D_MODEL ** -0.5),
        'b_mod': nrm(ks[15], (DEPTH, 6 * D_MODEL), 0.02),
        'w_in': nrm(ks[16], (DEPTH, D_MODEL, IN_WIDTH), D_MODEL ** -0.5),
        'b_if': b_if,
        'mlstm_norm_g': 1.0 + nrm(ks[18], (DEPTH, M_WIDTH), 0.1),
        'w_out': nrm(ks[19], (DEPTH, D_MIX, D_MODEL), D_MIX ** -0.5),
        'w_up': nrm(ks[20], (DEPTH, D_MODEL, 2 * D_FF), D_MODEL ** -0.5),
        'w_conv': nrm(ks[21], (DEPTH, CONV_W, D_FF), CONV_W ** -0.5),
        'b_conv': nrm(ks[22], (DEPTH, D_FF), 0.02),
        'w_down': nrm(ks[23], (DEPTH, D_FF, D_MODEL), D_FF ** -0.5),
        'final_g': 1.0 + nrm(ks[24], (D_MODEL,), 0.1),
    }


def reference(x_prompt, x_sample, cache_cmp_kv, cache_sel_kv, cache_win_kv, state_mlstm_C, state_mlstm_n,
              state_mlstm_m, state_ffn_conv, page_table, c_prompt, c_sample, norm1_g, norm2_g, w_mod, b_mod,
              w_in, b_if, mlstm_norm_g, w_out, w_up, w_conv, b_conv, w_down, final_g):
    pos_p = jnp.arange(SEQ)
    pos_s = PAST_LEN + jnp.arange(DEC_SEQ)
    xp, xs = x_prompt, x_sample
    conv0 = jnp.zeros((BATCH, CONV_W - 1, D_FF), x_prompt.dtype)
    outs_p = [[] for _ in range(7)]
    outs_s = [[] for _ in range(7)]
    for l in range(DEPTH):
        lw = (norm1_g[l], norm2_g[l], w_mod[l], b_mod[l], w_in[l], b_if[l], mlstm_norm_g[l], w_out[l],
              w_up[l], w_conv[l], b_conv[l], w_down[l])
        xp, kc, ksl, kw, (mc, mn, mm), cb = trunk_layer(xp, c_prompt, pos_p, conv0, nsa_attend_prompt,
                                                        mlstm_prompt, lw)
        for lst, arr in zip(outs_p, (kc, ksl, kw, mc, mn, mm, cb)):
            lst.append(arr)
        past_c = cache_cmp_kv[l][page_table].reshape(DEC_BATCH, -1, 2, A_KV, A_HD)
        past_s = cache_sel_kv[l][page_table].reshape(DEC_BATCH, -1, 2, A_KV, A_HD)
        attend_s = functools.partial(nsa_attend_sample, past_c=past_c, past_s=past_s, win_buf=cache_win_kv[l])
        recur_s = functools.partial(mlstm_sample, C=state_mlstm_C[l], n=state_mlstm_n[l], m=state_mlstm_m[l])
        xs, kc, ksl, kw, (mc, mn, mm), cb = trunk_layer(xs, c_sample, pos_s, state_ffn_conv[l], attend_s,
                                                        recur_s, lw)
        for lst, arr in zip(outs_s, (kc, ksl, kw, mc, mn, mm, cb)):
            lst.append(arr)
    y_prompt = rmsnorm(xp, final_g)
    y_sample = rmsnorm(xs, final_g)
    p_cmp, p_sel, p_win, p_C, p_n, p_m, p_conv = [jnp.stack(a, axis=0) for a in outs_p]
    s_cmp, s_sel, s_win, s_C, s_n, s_m, s_conv = [jnp.stack(a, axis=0) for a in outs_s]
    return (y_prompt, y_sample, p_cmp, p_sel, p_win, p_C, p_n, p_m, p_conv,
            s_cmp, s_sel, s_win, s_C, s_n, s_m, s_conv)
```

```python
import functools

import numpy as np
import jax
import jax.numpy as jnp
from jax import lax
from jax.experimental import pallas as pl
from jax.experimental.pallas import tpu as pltpu

F32 = jnp.float32
BF16 = jnp.bfloat16
I32 = jnp.int32

A_HD = 64
A_HEADS = 8
A_KV = 2
A_G = A_HEADS // A_KV
CMP_BLOCK = 64
TOPK = 16
WINDOW = 512
ROPE_THETA = 10000.0
FORCE = float(A_G + 1)
M_HEADS = 4
M_DK = 64
M_DV = 128
CONV_W = 3
EPS = 1e-6
PAGE_SIZE = 128

LANE = 128
SUBLANE = 8
BF16_ROWS = 16
VMEM_LIMIT = 56 * 1024 * 1024

A_W = A_HEADS * A_HD
KV_W = 2 * A_KV * A_HD
M_QKV_W = 2 * M_HEADS * M_DK + M_HEADS * M_DV
M_W = M_HEADS * M_DV
OFF_Q = 0
OFF_KC = OFF_Q + A_W
OFF_KS = OFF_KC + KV_W
OFF_KW = OFF_KS + KV_W
OFF_M = OFF_KW + KV_W
OFF_MO = OFF_M + M_QKV_W
OFF_SM = OFF_MO + M_W
IN_W_PAD = OFF_SM + LANE
GATE_COLS = 3 * A_HEADS

SEL_OFF_BIAS = -(2.0 ** 30)
MASK_NEG = -1e30


def _cparams(sem):
    return pltpu.CompilerParams(dimension_semantics=sem, vmem_limit_bytes=VMEM_LIMIT)


def _resident(shape, index_map):
    return pl.BlockSpec(shape, index_map, pipeline_mode=pl.Buffered(1))


def _mod_kernel(c_ref, w_ref, b_ref, o_ref):
    c = c_ref[...]
    a = (c * jax.nn.sigmoid(c)).astype(BF16)
    o_ref[0] = jnp.dot(a, w_ref[0].astype(BF16), preferred_element_type=F32) + b_ref[0]


def _mod_call(c_all, w_mod, b_mod):
    depth, d, n = w_mod.shape
    rows = c_all.shape[0]
    tn = 1536
    return pl.pallas_call(
        _mod_kernel,
        grid=(depth, n // tn),
        in_specs=[pl.BlockSpec((rows, d), lambda l, j: (0, 0)),
                  pl.BlockSpec((1, d, tn), lambda l, j: (l, 0, j)),
                  pl.BlockSpec((1, 1, tn), lambda l, j: (l, 0, j))],
        out_specs=pl.BlockSpec((1, rows, tn), lambda l, j: (l, 0, j)),
        out_shape=jax.ShapeDtypeStruct((depth, rows, n), F32),
        compiler_params=_cparams(("arbitrary", "arbitrary")),
    )(c_all, w_mod, b_mod.reshape(depth, 1, n))


def _inproj_kernel(x_ref, sc_ref, sh_ref, g_ref, w_ref, cos_ref, sin_ref,
                   q_ref, kvc_ref, kvs_ref, kvw_ref, mqkv_ref, mo_ref, sm_ref, *extra_refs,
                   tm, tiles_per_seq, attn_layouts):
    if attn_layouts:
        ksa_ref, vsb_ref, kwb_ref, vwb_ref, cm_ref = extra_refs
    x = x_ref[...]
    xn = x * lax.rsqrt(jnp.mean(x * x, axis=-1, keepdims=True) + EPS) * g_ref[...]
    xn = xn * (1.0 + sc_ref[0]) + sh_ref[0]
    xb = xn.astype(BF16)
    cos = cos_ref[...]
    sin = sin_ref[...]
    lane = lax.broadcasted_iota(I32, (tm, LANE), 1)
    lo_half = (lane % A_HD) < (A_HD // 2)
    first_head = lane < A_HD

    def proj(lo, hi):
        return jnp.dot(xb, w_ref[:, lo:hi], preferred_element_type=F32)

    def rope(v):
        rot = jnp.where(lo_half, pltpu.roll(v, LANE - A_HD // 2, 1), pltpu.roll(v, A_HD // 2, 1))
        return v * cos + rot * sin

    qa = proj(OFF_Q, OFF_Q + A_W)
    for s in range(A_W // LANE):
        r = rope(qa[:, s * LANE:(s + 1) * LANE]) * (A_HD ** -0.5)
        q_ref[:, (2 * s) * LANE:(2 * s + 1) * LANE] = jnp.where(first_head, r, 0.0).astype(BF16)
        q_ref[:, (2 * s + 1) * LANE:(2 * s + 2) * LANE] = jnp.where(
            first_head, pltpu.roll(r, A_HD, 1), 0.0).astype(BF16)

    def kv(lo):
        a = proj(lo, lo + KV_W)
        return rope(a[:, :LANE]), a[:, LANE:]

    kc, vc = kv(OFF_KC)
    kvc_ref[:, :LANE] = kc
    kvc_ref[:, LANE:] = vc
    ks, vs = kv(OFF_KS)
    kvs_ref[:, :LANE] = ks
    kvs_ref[:, LANE:] = vs
    kw, vw = kv(OFF_KW)
    kvw_ref[:, :LANE] = kw
    kvw_ref[:, LANE:] = vw

    if attn_layouts:
        nblk = tm // CMP_BLOCK
        cm_ref[:, :LANE] = jnp.sum(kc.reshape(nblk, CMP_BLOCK, LANE), axis=1) * (1.0 / CMP_BLOCK)
        cm_ref[:, LANE:] = jnp.sum(vc.reshape(nblk, CMP_BLOCK, LANE), axis=1) * (1.0 / CMP_BLOCK)
        row = lax.broadcasted_iota(I32, (tm, LANE), 0)
        blk = (pl.program_id(0) % tiles_per_seq) * nblk + row // CMP_BLOCK
        onehot = jnp.where(lane - A_HD == blk, 1.0, 0.0)
        ksa_ref[:, :LANE] = jnp.where(first_head, ks, onehot).astype(BF16)
        ksa_ref[:, LANE:] = jnp.where(first_head, pltpu.roll(ks, A_HD, 1), onehot).astype(BF16)
        vsb_ref[...] = vs.astype(BF16)
        kwb_ref[:, :LANE] = kw.astype(BF16)
        kwb_ref[:, LANE:] = pltpu.roll(kw, A_HD, 1).astype(BF16)
        vwb_ref[...] = vw.astype(BF16)

    m = proj(OFF_M, OFF_MO)
    nqk = M_HEADS * M_DK
    mqkv_ref[:, :nqk] = m[:, :nqk]
    mqkv_ref[:, nqk:2 * nqk] = m[:, nqk:2 * nqk] * (M_DK ** -0.5)
    mqkv_ref[:, 2 * nqk:] = m[:, 2 * nqk:]
    mo_ref[...] = proj(OFF_MO, OFF_SM)
    sm_ref[...] = proj(OFF_SM, IN_W_PAD)


def _mod_specs(per_row_mod, tm, d, tiles_per_seq):
    if per_row_mod:
        return pl.BlockSpec((1, tm, d), lambda i: (0, i, 0))
    return pl.BlockSpec((1, 1, d), lambda i: (i // tiles_per_seq, 0, 0))


def _inproj_call(x, sc, sh, g, w, cos, sin, *, tm, tiles_per_seq, per_row_mod, attn_layouts):
    rows, d = x.shape
    n_tiles = rows // tm
    mod_spec = _mod_specs(per_row_mod, tm, d, tiles_per_seq)
    if per_row_mod:
        tab_spec = pl.BlockSpec((tm, LANE), lambda i: (i, 0))
    else:
        tab_spec = pl.BlockSpec((tm, LANE), lambda i: (i % tiles_per_seq, 0))

    def rowspec(width):
        return pl.BlockSpec((tm, width), lambda i: (i, 0))

    out_shapes = [
        jax.ShapeDtypeStruct((rows, 2 * A_W), BF16),
        jax.ShapeDtypeStruct((rows, KV_W), F32),
        jax.ShapeDtypeStruct((rows, KV_W), F32),
        jax.ShapeDtypeStruct((rows, KV_W), F32),
        jax.ShapeDtypeStruct((rows, M_QKV_W), F32),
        jax.ShapeDtypeStruct((rows, M_W), F32),
        jax.ShapeDtypeStruct((rows, LANE), F32),
    ]
    out_specs = [rowspec(2 * A_W), rowspec(KV_W), rowspec(KV_W), rowspec(KV_W), rowspec(M_QKV_W),
                 rowspec(M_W), rowspec(LANE)]
    if attn_layouts:
        out_shapes += [
            jax.ShapeDtypeStruct((rows, 2 * LANE), BF16),
            jax.ShapeDtypeStruct((rows, LANE), BF16),
            jax.ShapeDtypeStruct((rows, 2 * LANE), BF16),
            jax.ShapeDtypeStruct((rows, LANE), BF16),
            jax.ShapeDtypeStruct((rows // CMP_BLOCK, KV_W), F32),
        ]
        out_specs += [rowspec(2 * LANE), rowspec(LANE), rowspec(2 * LANE), rowspec(LANE),
                      pl.BlockSpec((tm // CMP_BLOCK, KV_W), lambda i: (i, 0))]
    return pl.pallas_call(
        functools.partial(_inproj_kernel, tm=tm, tiles_per_seq=tiles_per_seq, attn_layouts=attn_layouts),
        grid=(n_tiles,),
        in_specs=[rowspec(d), mod_spec, mod_spec, pl.BlockSpec((1, d), lambda i: (0, 0)),
                  _resident(w.shape, lambda i: (0, 0)), tab_spec, tab_spec],
        out_specs=out_specs,
        out_shape=out_shapes,
        compiler_params=_cparams(("arbitrary",)),
    )(x, sc, sh, g, w, cos, sin)


def _nt(a, b):
    return lax.dot_general(a, b, (((1,), (1,)), ((), ())), preferred_element_type=F32)


def _softmax_update(s, v, carry):
    m, l, acc = carry
    m_new = jnp.maximum(m, jnp.max(s, axis=-1, keepdims=True))
    alpha = jnp.exp(m - m_new)
    p = jnp.exp(s - m_new)
    l = alpha * l + jnp.sum(p, axis=-1, keepdims=True)
    acc = alpha * acc + jnp.dot(p.astype(BF16), v, preferred_element_type=F32)
    return m_new, l, acc


def _softmax_init(rows, width):
    return (jnp.full((rows, 1), MASK_NEG, F32), jnp.zeros((rows, 1), F32), jnp.zeros((rows, width), F32))


def _masked_softmax(s, ok, axis):
    s = jnp.where(ok, s, -jnp.inf)
    smax = jnp.max(s, axis=axis, keepdims=True)
    e = jnp.exp(s - jnp.where(jnp.isfinite(smax), smax, 0.0))
    den = jnp.sum(e, axis=axis, keepdims=True)
    return e / jnp.where(den > 0, den, 1.0)


def _stable_rank_sublanes(score, sub8):
    nb, n = score.shape
    groups = [score[SUBLANE * r:SUBLANE * (r + 1), :] for r in range(nb // SUBLANE)]
    cnts = [jnp.zeros((SUBLANE, n), F32) for _ in groups]
    for ib in range(nb):
        s_i = jnp.broadcast_to(score[ib:ib + 1, :], (SUBLANE, n))
        for r, grp in enumerate(groups):
            if SUBLANE * r > ib:
                one = jnp.where(s_i >= grp, 1.0, 0.0)
            elif SUBLANE * (r + 1) <= ib:
                one = jnp.where(s_i > grp, 1.0, 0.0)
            else:
                one = jnp.where(sub8 > ib - SUBLANE * r, jnp.where(s_i >= grp, 1.0, 0.0),
                                jnp.where(s_i > grp, 1.0, 0.0))
            cnts[r] = cnts[r] + one
    return jnp.concatenate(cnts, axis=0)


def _attn_prompt_kernel(q_ref, cm_ref, ksa_ref, vsb_ref, kwb_ref, vwb_ref, sm_ref, o_ref, *, tq, tk, nb):
    i = pl.program_id(1)
    q0 = i * tq
    rows = A_G * tq

    gates = jax.nn.sigmoid(sm_ref[:, :GATE_COLS])
    cm = cm_ref[0]

    blk_t = lax.broadcasted_iota(I32, (nb, tq), 0)
    qpos_t = q0 + lax.broadcasted_iota(I32, (nb, tq), 1)
    cur_t = qpos_t // CMP_BLOCK
    forced_t = (blk_t == 0) | (blk_t == cur_t)
    allowed_t = blk_t <= cur_t
    blk_t4 = lax.broadcasted_iota(I32, (nb, rows), 0)
    qpos_t4 = q0 + (lax.broadcasted_iota(I32, (nb, rows), 1) & (tq - 1))
    cmp_ok_t4 = (blk_t4 + 1) * CMP_BLOCK <= qpos_t4 + 1
    blk_r = lax.broadcasted_iota(I32, (rows, nb), 1)
    qpos_r = q0 + (lax.broadcasted_iota(I32, (rows, nb), 0) & (tq - 1))
    cmp_ok_r = (blk_r + 1) * CMP_BLOCK <= qpos_r + 1
    rr_k = lax.broadcasted_iota(I32, (rows, tk), 0) & (tq - 1)
    cc_k = lax.broadcasted_iota(I32, (rows, tk), 1)
    rr_q = lax.broadcasted_iota(I32, (rows, tq), 0) & (tq - 1)
    cc_q = lax.broadcasted_iota(I32, (rows, tq), 1)
    sub8 = lax.broadcasted_iota(I32, (SUBLANE, tq), 0)

    for h in range(A_KV):
        qh = jnp.concatenate([q_ref[:, (A_G * h + g) * LANE:(A_G * h + g + 1) * LANE] for g in range(A_G)],
                             axis=0)

        cm_k = cm[:, :LANE] if h == 0 else pltpu.roll(cm[:, :LANE], A_HD, 1)
        cm_kb = cm_k.astype(BF16)
        cm_vb = cm[:, LANE:].astype(BF16)
        p_c = _masked_softmax(_nt(qh, cm_kb), cmp_ok_r, 1)
        o_cmp = jnp.dot(p_c.astype(BF16), cm_vb, preferred_element_type=F32)

        p_t = _masked_softmax(_nt(cm_kb, qh), cmp_ok_t4, 0)
        imp = p_t[:, 0:tq]
        for g in range(1, A_G):
            imp = imp + p_t[:, g * tq:(g + 1) * tq]
        score = jnp.where(forced_t, FORCE, jnp.where(allowed_t, imp, -1.0))
        cnt = _stable_rank_sublanes(score, sub8)
        bias_t = jnp.where(cnt < TOPK, jnp.where(score >= 0, 0.0, SEL_OFF_BIAS), SEL_OFF_BIAS)
        pad_t = jnp.concatenate([jnp.zeros((A_HD, tq), F32), bias_t, jnp.zeros((LANE - A_HD - nb, tq), F32)],
                                axis=0) if nb < LANE - A_HD else jnp.concatenate(
                                    [jnp.zeros((A_HD, tq), F32), bias_t], axis=0)
        bias_q = pad_t.T.astype(BF16)
        q_aug = qh + jnp.concatenate([bias_q] * A_G, axis=0)

        kcol = h * LANE

        def sel_tile(kt, carry, masked):
            k0 = pl.multiple_of(kt * tk, tk)
            s = _nt(q_aug, ksa_ref[0, pl.ds(k0, tk), kcol:kcol + LANE])
            if masked:
                s = jnp.where(k0 + cc_k <= q0 + rr_k, s, MASK_NEG)
            return _softmax_update(s, vsb_ref[0, pl.ds(k0, tk), :], carry)

        n_full = q0 // tk
        carry = lax.fori_loop(0, n_full, lambda kt, c: sel_tile(kt, c, False), _softmax_init(rows, LANE))
        _, l, acc = sel_tile(n_full, carry, True)
        o_sel = acc / l

        carry = _softmax_init(rows, LANE)
        n_wt = WINDOW // tq + 1
        for wt in range(n_wt):
            start = q0 - wt * tq
            startc = pl.multiple_of(jnp.maximum(start, 0), tq)
            s = _nt(qh, kwb_ref[0, pl.ds(startc, tq), kcol:kcol + LANE])
            if wt == 0:
                s = jnp.where(cc_q <= rr_q, s, MASK_NEG)
            else:
                if wt == n_wt - 1:
                    s = jnp.where(cc_q > rr_q, s, MASK_NEG)
                s = jnp.where(start + cc_q >= 0, s, MASK_NEG)
            carry = _softmax_update(s, vwb_ref[0, pl.ds(startc, tq), :], carry)
        _, l, acc = carry
        o_win = acc / l

        lo = h * A_HD
        for g in range(A_G):
            head = A_G * h + g
            oc = o_cmp[g * tq:(g + 1) * tq, lo:lo + A_HD]
            os_ = o_sel[g * tq:(g + 1) * tq, lo:lo + A_HD]
            ow = o_win[g * tq:(g + 1) * tq, lo:lo + A_HD]
            mix = (gates[:, 3 * head:3 * head + 1] * oc + gates[:, 3 * head + 1:3 * head + 2] * os_
                   + gates[:, 3 * head + 2:3 * head + 3] * ow)
            o_ref[:, head * A_HD:(head + 1) * A_HD] = mix.astype(o_ref.dtype)


def _attn_prompt_call(q, cm, ksa, vsb, kwb, vwb, sm, *, batch, seq, tq, tk):
    nb = seq // CMP_BLOCK
    tiles = seq // tq
    assert nb <= LANE - A_HD and tq == LANE and tk % tq == 0 and seq % tk == 0

    def full(width):
        return pl.BlockSpec((1, seq, width), lambda b, i: (b, 0, 0))

    return pl.pallas_call(
        functools.partial(_attn_prompt_kernel, tq=tq, tk=tk, nb=nb),
        grid=(batch, tiles),
        in_specs=[pl.BlockSpec((tq, 2 * A_W), lambda b, i: (b * tiles + i, 0)),
                  pl.BlockSpec((1, nb, KV_W), lambda b, i: (b, 0, 0)),
                  full(2 * LANE), full(LANE), full(2 * LANE), full(LANE),
                  pl.BlockSpec((tq, LANE), lambda b, i: (b * tiles + i, 0))],
        out_specs=pl.BlockSpec((tq, A_W), lambda b, i: (b * tiles + i, 0)),
        out_shape=jax.ShapeDtypeStruct((batch * seq, A_W), BF16),
        compiler_params=_cparams(("arbitrary", "arbitrary")),
    )(q, cm.reshape(batch, nb, KV_W), ksa.reshape(batch, seq, 2 * LANE), vsb.reshape(batch, seq, LANE),
      kwb.reshape(batch, seq, 2 * LANE), vwb.reshape(batch, seq, LANE), sm)


def _page_spec(k, pages_per_step, layer):
    def index(b, j, pt):
        return (layer, pt[b, j * pages_per_step + k], 0, 0)
    return pl.BlockSpec((1, 1, PAGE_SIZE, KV_W), index)


def _cmp_means_kernel(pt_ref, *refs, pages_per_step):
    del pt_ref
    o_ref = refs[pages_per_step]
    per_page = PAGE_SIZE // CMP_BLOCK
    for k in range(pages_per_step):
        page = refs[k][0, 0]
        o_ref[0, per_page * k:per_page * (k + 1), :] = (
            jnp.sum(page.reshape(per_page, CMP_BLOCK, KV_W), axis=1) * (1.0 / CMP_BLOCK))


def _cmp_means_call(cache, page_table, layer, *, pages_per_step):
    dec_b, n_pages = page_table.shape
    per_page = PAGE_SIZE // CMP_BLOCK
    grid_spec = pltpu.PrefetchScalarGridSpec(
        num_scalar_prefetch=1,
        grid=(dec_b, n_pages // pages_per_step),
        in_specs=[_page_spec(k, pages_per_step, layer) for k in range(pages_per_step)],
        out_specs=pl.BlockSpec((1, per_page * pages_per_step, KV_W), lambda b, j, pt: (b, j, 0)),
    )
    return pl.pallas_call(
        functools.partial(_cmp_means_kernel, pages_per_step=pages_per_step),
        grid_spec=grid_spec,
        out_shape=jax.ShapeDtypeStruct((dec_b, n_pages * per_page, KV_W), F32),
        compiler_params=_cparams(("arbitrary", "arbitrary")),
    )(page_table, *([cache] * pages_per_step))


def _attn_sample_kernel(pt_ref, q_ref, cm_ref, kn_ref, win_ref, wn_ref, gl_ref, *refs,
                        pages_per_step, dec_t, past_len, nbp):
    del pt_ref
    page_refs = refs[:pages_per_step]
    o_ref = refs[pages_per_step]
    m_scr, l_scr, acc_scr, bias_scr, ocmp_scr = refs[pages_per_step + 1:]
    j = pl.program_id(1)
    rows = q_ref.shape[1]
    grp = A_KV * dec_t
    tk = pages_per_step * PAGE_SIZE
    qb = q_ref[0]
    t_row = lax.broadcasted_iota(I32, (rows, 1), 0) % dec_t

    @pl.when(j == 0)
    def _():
        cm = cm_ref[0]
        blk = lax.broadcasted_iota(I32, (rows, nbp), 1)
        ok = (blk + 1) * CMP_BLOCK <= past_len + t_row + 1
        p = _masked_softmax(_nt(qb, cm[:, :LANE].astype(BF16)), ok, 1)
        ocmp_scr[...] = jnp.dot(p.astype(BF16), cm[:, LANE:].astype(BF16), preferred_element_type=F32)
        imp = p[0:grp]
        for g in range(1, A_G):
            imp = imp + p[g * grp:(g + 1) * grp]
        blk8 = lax.broadcasted_iota(I32, (grp, nbp), 1)
        cur = (past_len + lax.broadcasted_iota(I32, (grp, nbp), 0) % dec_t) // CMP_BLOCK
        score = jnp.where(blk8 == 0, FORCE, jnp.where(blk8 == cur, FORCE, jnp.where(blk8 <= cur, imp, -1.0)))
        cnt = jnp.where(FORCE > score, 1.0, 0.0)
        for ib in range(nbp):
            c_i = score[:, ib:ib + 1]
            cnt = cnt + jnp.where(blk8 > ib, jnp.where(c_i >= score, 1.0, 0.0), jnp.where(c_i > score, 1.0, 0.0))
        bias = jnp.where(cnt < TOPK, jnp.where(score >= 0, 0.0, SEL_OFF_BIAS), SEL_OFF_BIAS)
        bias_scr[...] = jnp.concatenate([bias] * A_G, axis=0).astype(BF16)
        m_scr[...] = jnp.full(m_scr.shape, MASK_NEG, F32)
        l_scr[...] = jnp.zeros(l_scr.shape, F32)
        acc_scr[...] = jnp.zeros(acc_scr.shape, F32)

    keys = jnp.concatenate([r[0, 0, :, :LANE].astype(BF16) for r in page_refs], axis=0)
    vals = jnp.concatenate([r[0, 0, :, LANE:].astype(BF16) for r in page_refs], axis=0)
    blk_e = lax.broadcasted_iota(I32, (nbp, tk), 0)
    key_blk = j * (tk // CMP_BLOCK) + lax.broadcasted_iota(I32, (nbp, tk), 1) // CMP_BLOCK
    expand = jnp.where(blk_e == key_blk, 1.0, 0.0).astype(BF16)
    s = _nt(qb, keys) + jnp.dot(bias_scr[...], expand, preferred_element_type=F32)
    m, l, acc = _softmax_update(s, vals, (m_scr[...], l_scr[...], acc_scr[...]))
    m_scr[...] = m
    l_scr[...] = l
    acc_scr[...] = acc

    @pl.when(j == pl.num_programs(1) - 1)
    def _():
        new_ok = lax.broadcasted_iota(I32, (rows, kn_ref.shape[1]), 1) <= t_row

        def new_rows(carry, ref):
            kv = ref[0]
            s_n = jnp.where(new_ok, _nt(qb, kv[:, :LANE].astype(BF16)), MASK_NEG)
            return _softmax_update(s_n, kv[:, LANE:].astype(BF16), carry)

        _, l_s, acc_s = new_rows((m_scr[...], l_scr[...], acc_scr[...]), kn_ref)
        o_sel = acc_s / l_s

        win = win_ref[0, 0]
        wb = win.shape[0]
        jw = lax.broadcasted_iota(I32, (rows, wb), 1)
        s_w = _nt(qb, win[:, :LANE].astype(BF16))
        s_w = jnp.where(jw > t_row + (wb - WINDOW), s_w, MASK_NEG)
        carry = _softmax_update(s_w, win[:, LANE:].astype(BF16), _softmax_init(rows, LANE))
        _, l_w, acc_w = new_rows(carry, wn_ref)
        o_win = acc_w / l_w

        gate = jax.nn.sigmoid(gl_ref[0])
        o_ref[0] = gate[:, 0:1] * ocmp_scr[...] + gate[:, 1:2] * o_sel + gate[:, 2:3] * o_win


def _attn_sample_call(page_table, q, cm, kv_new, cache_sel, cache_win, win_new, gate_logits, layer, *,
                      pages_per_step, dec_t):
    dec_b, n_pages = page_table.shape
    rows = q.shape[1]
    nbp = cm.shape[1]
    wb = cache_win.shape[2]
    past_len = n_pages * PAGE_SIZE
    assert nbp == LANE and past_len >= wb and wb >= WINDOW

    def per_b(shape):
        return pl.BlockSpec((1,) + shape, lambda b, j, pt: (b, 0, 0))

    grid_spec = pltpu.PrefetchScalarGridSpec(
        num_scalar_prefetch=1,
        grid=(dec_b, n_pages // pages_per_step),
        in_specs=[per_b((rows, LANE)), per_b((nbp, KV_W)), per_b(kv_new.shape[1:]),
                  pl.BlockSpec((1, 1, wb, KV_W), lambda b, j, pt: (layer, b, 0, 0)),
                  per_b(win_new.shape[1:]), per_b((rows, LANE))]
                 + [_page_spec(k, pages_per_step, layer) for k in range(pages_per_step)],
        out_specs=per_b((rows, LANE)),
        scratch_shapes=[pltpu.VMEM((rows, 1), F32), pltpu.VMEM((rows, 1), F32), pltpu.VMEM((rows, LANE), F32),
                        pltpu.VMEM((rows, nbp), BF16), pltpu.VMEM((rows, LANE), F32)],
    )
    return pl.pallas_call(
        functools.partial(_attn_sample_kernel, pages_per_step=pages_per_step, dec_t=dec_t,
                          past_len=past_len, nbp=nbp),
        grid_spec=grid_spec,
        out_shape=jax.ShapeDtypeStruct((dec_b, rows, LANE), F32),
        compiler_params=_cparams(("arbitrary", "arbitrary")),
    )(page_table, q, cm, kv_new, cache_win, win_new, gate_logits, *([cache_sel] * pages_per_step))


def _split3(a):
    hi = a.astype(BF16)
    r1 = a - hi.astype(F32)
    mid = r1.astype(BF16)
    lo = (r1 - mid.astype(F32)).astype(BF16)
    return hi, mid, lo


def _log_sigmoid(x):
    return jnp.minimum(x, 0.0) - jnp.log1p(jnp.exp(-jnp.abs(x)))


def _mlstm_kernel(qkv_ref, sm_ref, smt_ref, mo_ref, bc_ref, br_ref, g_ref, c0_ref, n0_ref, m0_ref,
                  o_ref, c_ref, n_ref, m_ref, *, chunk, valid_len):
    @pl.when(pl.program_id(1) == 0)
    def _():
        c_ref[...] = c0_ref[...]
        n_ref[...] = n0_ref[...]
        m_ref[...] = m0_ref[...]

    ng = 2 * M_HEADS
    gate_c = sm_ref[:, GATE_COLS:GATE_COLS + ng] + bc_ref[...]
    gate_r = smt_ref[0] + br_ref[...]
    lf_c = _log_sigmoid(gate_c)
    lf_r = _log_sigmoid(gate_r)
    ig_c, ig_r = gate_c, gate_r
    if valid_len < chunk:
        tc = lax.broadcasted_iota(I32, (chunk, ng), 0)
        tr = lax.broadcasted_iota(I32, (ng, chunk), 1)
        lf_c = jnp.where(tc < valid_len, lf_c, 0.0)
        lf_r = jnp.where(tr < valid_len, lf_r, 0.0)
        ig_c = jnp.where(tc < valid_len, ig_c, MASK_NEG)
        ig_r = jnp.where(tr < valid_len, ig_r, MASK_NEG)

    t_i = lax.broadcasted_iota(I32, (chunk, chunk), 0)
    s_i = lax.broadcasted_iota(I32, (chunk, chunk), 1)
    causal = s_i <= t_i
    tri = jnp.where(causal, 1.0, 0.0).astype(BF16)
    tri_t = jnp.where(t_i <= s_i, 1.0, 0.0).astype(BF16)
    b_c = sum(jnp.dot(tri, part, preferred_element_type=F32) for part in _split3(lf_c))
    b_r = sum(jnp.dot(part, tri_t, preferred_element_type=F32) for part in _split3(lf_r))

    nqk = M_HEADS * M_DK
    for n in range(M_HEADS):
        q = qkv_ref[:, n * M_DK:(n + 1) * M_DK]
        k = qkv_ref[:, nqk + n * M_DK:nqk + (n + 1) * M_DK]
        v = qkv_ref[:, 2 * nqk + n * M_DV:2 * nqk + (n + 1) * M_DV]
        qb, kb, vb = q.astype(BF16), k.astype(BF16), v.astype(BF16)
        fcol = M_HEADS + n
        bc = b_c[:, fcol:fcol + 1]
        br = b_r[fcol:fcol + 1, :]
        igc = ig_c[:, n:n + 1]
        igr = ig_r[n:n + 1, :]
        b_end = bc[chunk - 1:chunk, :]
        m_prev = m_ref[0, :, n:n + 1]
        c_prev = c_ref[0, n]
        n_prev = n_ref[0, n:n + 1, :]

        a = bc + m_prev
        d = jnp.where(causal, bc + (igr - br), -jnp.inf)
        m_t = jnp.maximum(a, jnp.max(d, axis=1, keepdims=True))
        w = _nt(qb, kb) * jnp.exp(d - m_t)
        aw = jnp.exp(a - m_t)
        num = (jnp.dot(w.astype(BF16), vb, preferred_element_type=F32)
               + aw * jnp.dot(qb, c_prev.astype(BF16), preferred_element_type=F32))
        den = jnp.sum(w, axis=1, keepdims=True) + aw * jnp.sum(q * n_prev, axis=1, keepdims=True)
        hcell = num / jnp.maximum(jnp.abs(den), jnp.exp(-m_t))

        hn = hcell * lax.rsqrt(jnp.mean(hcell * hcell, axis=-1, keepdims=True) + EPS) * g_ref[:, n * M_DV:(n + 1) * M_DV]
        o_ref[:, n * M_DV:(n + 1) * M_DV] = (jax.nn.sigmoid(mo_ref[:, n * M_DV:(n + 1) * M_DV]) * hn).astype(o_ref.dtype)

        wl = b_end - bc + igc
        m_new = jnp.maximum(b_end + m_prev, jnp.max(wl, axis=0, keepdims=True))
        decay = jnp.exp(b_end + m_prev - m_new)
        kws = k * jnp.exp(wl - m_new)
        c_ref[0, n] = decay * c_prev + lax.dot_general(kws.astype(BF16), vb, (((0,), (0,)), ((), ())),
                                                       preferred_element_type=F32)
        n_ref[0, n:n + 1, :] = decay * n_prev + jnp.sum(kws, axis=0, keepdims=True)
        m_ref[0, :, n:n + 1] = m_new


def _mlstm_call(qkv, sm, smt, mo, b_if, norm_g, c0, n0, m0, *, batch, seq, chunk, valid_len):
    nc = seq // chunk
    ng = 2 * M_HEADS

    def rowspec(width):
        return pl.BlockSpec((chunk, width), lambda b, c: (b * nc + c, 0))

    def const(shape):
        return pl.BlockSpec(shape, lambda b, c: (0,) * len(shape))

    state_specs = [pl.BlockSpec((1, M_HEADS, M_DK, M_DV), lambda b, c: (b, 0, 0, 0)),
                   pl.BlockSpec((1, M_HEADS, M_DK), lambda b, c: (b, 0, 0)),
                   pl.BlockSpec((1, 1, M_HEADS), lambda b, c: (b, 0, 0))]
    return pl.pallas_call(
        functools.partial(_mlstm_kernel, chunk=chunk, valid_len=valid_len),
        grid=(batch, nc),
        in_specs=[rowspec(M_QKV_W), rowspec(LANE), pl.BlockSpec((1, ng, chunk), lambda b, c: (b, 0, c)),
                  rowspec(M_W), const((1, ng)), const((ng, 1)), const((1, M_W))] + state_specs,
        out_specs=[rowspec(M_W)] + state_specs,
        out_shape=[jax.ShapeDtypeStruct((batch * seq, M_W), BF16),
                   jax.ShapeDtypeStruct((batch, M_HEADS, M_DK, M_DV), F32),
                   jax.ShapeDtypeStruct((batch, M_HEADS, M_DK), F32),
                   jax.ShapeDtypeStruct((batch, 1, M_HEADS), F32)],
        compiler_params=_cparams(("arbitrary", "arbitrary")),
    )(qkv, sm, smt, mo, b_if.reshape(1, ng), b_if.reshape(ng, 1), norm_g.reshape(1, M_W), c0, n0, m0)


def _ffn_kernel(x_ref, a_ref, mh_ref, wo_ref, g1_ref, sc_ref, sh_ref, g2_ref, n2_ref, wu_ref, wc_ref, bc_ref,
                wd_ref, hist_ref, *refs, tm, tiles_per_seq, shift, ff, ch, final):
    if final:
        fg_ref, xo_ref, cs_ref, y_ref, carry_scr, up_scr = refs
    else:
        xo_ref, cs_ref, carry_scr, up_scr = refs
    hist_rows = (CONV_W - 1) * shift
    hoff = up_scr.shape[0] - tm

    @pl.when(pl.program_id(0) % tiles_per_seq == 0)
    def _():
        carry_scr[...] = hist_ref[0]

    y = (jnp.dot(a_ref[...], wo_ref[:A_W, :], preferred_element_type=F32)
         + jnp.dot(mh_ref[...], wo_ref[A_W:, :], preferred_element_type=F32))
    x1 = x_ref[...] + g1_ref[0] * y
    xn = x1 * lax.rsqrt(jnp.mean(x1 * x1, axis=-1, keepdims=True) + EPS) * n2_ref[...]
    xb = (xn * (1.0 + sc_ref[0]) + sh_ref[0]).astype(BF16)

    acc = jnp.zeros(x1.shape, F32)
    for c in range(ff // ch):
        lo, hi = c * ch, (c + 1) * ch
        u = jnp.dot(xb, wu_ref[:, lo:hi], preferred_element_type=F32)
        gt = jnp.dot(xb, wu_ref[:, ff + lo:ff + hi], preferred_element_type=F32)
        up_scr[hoff - hist_rows:hoff, :] = carry_scr[:, lo:hi]
        up_scr[hoff:, :] = u
        tail = u[tm - hist_rows:, :]
        carry_scr[:, lo:hi] = tail
        cs_ref[0, :, lo:hi] = tail
        conv = bc_ref[:, lo:hi]
        for jj in range(CONV_W - 1):
            start = hoff - (CONV_W - 1 - jj) * shift
            conv = conv + wc_ref[jj:jj + 1, lo:hi] * up_scr[start:start + tm, :]
        conv = conv + wc_ref[CONV_W - 1:CONV_W, lo:hi] * u
        hid = conv * jax.nn.sigmoid(conv) * gt
        acc = acc + jnp.dot(hid.astype(BF16), wd_ref[lo:hi, :], preferred_element_type=F32)

    x2 = x1 + g2_ref[0] * acc
    xo_ref[...] = x2
    if final:
        y_ref[...] = x2 * lax.rsqrt(jnp.mean(x2 * x2, axis=-1, keepdims=True) + EPS) * fg_ref[...]


def _ffn_call(x, a_out, m_out, wo, g1, sc, sh, g2, n2, wu, wc, bc, wd, hist, final_g, *,
              tm, tiles_per_seq, shift, per_row_mod, ch):
    rows, d = x.shape
    ff = wd.shape[0]
    groups, hist_rows, _ = hist.shape
    assert hist_rows == (CONV_W - 1) * shift and tm >= hist_rows and ff % ch == 0
    final = final_g is not None
    hoff = -(-hist_rows // SUBLANE) * SUBLANE
    mod_spec = _mod_specs(per_row_mod, tm, d, tiles_per_seq)

    def rowspec(width):
        return pl.BlockSpec((tm, width), lambda i: (i, 0))

    def const(arr):
        return _resident(arr.shape, lambda i: (0,) * arr.ndim)

    hist_spec = pl.BlockSpec((1, hist_rows, ff), lambda i: (i // tiles_per_seq, 0, 0))
    in_specs = [rowspec(d), rowspec(A_W), rowspec(M_W), const(wo), mod_spec, mod_spec, mod_spec, mod_spec,
                pl.BlockSpec((1, d), lambda i: (0, 0)), const(wu), const(wc), const(bc), const(wd), hist_spec]
    args = [x, a_out, m_out, wo, g1, sc, sh, g2, n2, wu, wc, bc, wd, hist]
    out_specs = [rowspec(d), hist_spec]
    out_shape = [jax.ShapeDtypeStruct((rows, d), F32), jax.ShapeDtypeStruct(hist.shape, F32)]
    if final:
        in_specs.append(pl.BlockSpec((1, d), lambda i: (0, 0)))
        args.append(final_g)
        out_specs.append(rowspec(d))
        out_shape.append(jax.ShapeDtypeStruct((rows, d), F32))
    return pl.pallas_call(
        functools.partial(_ffn_kernel, tm=tm, tiles_per_seq=tiles_per_seq, shift=shift, ff=ff, ch=ch, final=final),
        grid=(rows // tm,),
        in_specs=in_specs,
        out_specs=out_specs,
        out_shape=out_shape,
        scratch_shapes=[pltpu.VMEM((hist_rows, ff), F32), pltpu.VMEM((hoff + tm, ch), F32)],
        compiler_params=_cparams(("arbitrary",)),
    )(*args)


def _prep_w_in(w_in):
    sizes = (A_W, KV_W, KV_W, KV_W, GATE_COLS, M_HEADS * M_DK, M_HEADS * M_DK, M_W, M_W, 2 * M_HEADS)
    points = [int(s) for s in np.cumsum(sizes)[:-1]]
    a_q, a_kc, a_ks, a_kw, a_g, m_q, m_k, m_v, m_o, m_if = jnp.split(w_in, points, axis=-1)
    pad = jnp.zeros(w_in.shape[:-1] + (LANE - GATE_COLS - 2 * M_HEADS,), w_in.dtype)
    return jnp.concatenate([a_q, a_kc, a_ks, a_kw, m_q, m_k, m_v, m_o, a_g, m_if, pad], axis=-1).astype(BF16)


def _rope_tables(pos):
    half = A_HD // 2
    freq = ROPE_THETA ** (-2.0 * jnp.arange(half, dtype=F32) / A_HD)
    ang = pos.astype(F32)[:, None] * freq[None, :]
    cos, sin = jnp.cos(ang), jnp.sin(ang)
    return jnp.concatenate([cos, cos, cos, cos], axis=-1), jnp.concatenate([-sin, sin, -sin, sin], axis=-1)


def _kv_out(kv, lead):
    return kv.reshape(lead + (2, A_KV, A_HD))


def kernel(x_prompt, x_sample, cache_cmp_kv, cache_sel_kv, cache_win_kv, state_mlstm_C, state_mlstm_n,
           state_mlstm_m, state_ffn_conv, page_table, c_prompt, c_sample, norm1_g, norm2_g, w_mod, b_mod,
           w_in, b_if, mlstm_norm_g, w_out, w_up, w_conv, b_conv, w_down, final_g):
    batch, seq, d = x_prompt.shape
    dec_b, dec_t, _ = x_sample.shape
    depth = w_in.shape[0]
    ff = w_down.shape[1]
    n_phys = cache_cmp_kv.shape[1]
    n_pages = page_table.shape[1]
    past_len = n_pages * PAGE_SIZE
    wb = cache_win_kv.shape[2]
    rows_s = dec_t * dec_b
    assert A_KV * dec_t == SUBLANE and dec_t >= CONV_W - 1 and dec_b % SUBLANE == 0

    tm_p = min(512, seq)
    tq, tk = 128, 256
    chunk_p = min(256, seq)
    ch = ff // 2
    pages_per_step = 8
    t_pad = BF16_ROWS

    wi = _prep_w_in(w_in)
    wo, wu, wd = w_out.astype(BF16), w_up.astype(BF16), w_down.astype(BF16)
    fg = final_g.reshape(1, d)

    n_c = batch + dec_b
    c_all = jnp.concatenate([c_prompt, c_sample, jnp.zeros((-n_c % SUBLANE, d), F32)], axis=0)
    mod = _mod_call(c_all, w_mod, b_mod)

    cos_p, sin_p = _rope_tables(jnp.arange(seq))
    cos_s, sin_s = [jnp.repeat(t, dec_b, axis=0) for t in _rope_tables(past_len + jnp.arange(dec_t))]

    cache_cmp = cache_cmp_kv.reshape(depth, n_phys, PAGE_SIZE, KV_W)
    cache_sel = cache_sel_kv.reshape(depth, n_phys, PAGE_SIZE, KV_W)
    cache_win = cache_win_kv.reshape(depth, dec_b, wb, KV_W)

    def to_bt(a):
        a = a.reshape(dec_t, dec_b, a.shape[-1]).transpose(1, 0, 2)
        return jnp.pad(a, ((0, 0), (0, t_pad - dec_t), (0, 0)))

    xp = x_prompt.reshape(batch * seq, d)
    xs = x_sample.transpose(1, 0, 2).reshape(rows_s, d)
    outs_p = [[] for _ in range(7)]
    outs_s = [[] for _ in range(7)]
    y_p = y_s = None
    for l in range(depth):
        last = l == depth - 1
        mods = [mod[l, :, k * d:(k + 1) * d] for k in range(6)]
        sh1_p, sc1_p, g1_p, sh2_p, sc2_p, g2_p = [m[:batch].reshape(batch, 1, d) for m in mods]
        sh1_s, sc1_s, g1_s, sh2_s, sc2_s, g2_s = [jnp.tile(m[batch:n_c], (dec_t, 1)).reshape(1, rows_s, d)
                                                  for m in mods]
        n1, n2 = norm1_g[l].reshape(1, d), norm2_g[l].reshape(1, d)
        wc, bc = w_conv[l], b_conv[l].reshape(1, ff)

        tiles_p = seq // tm_p
        (q, kvc, kvs, kvw, mqkv, mo, sm, ksa, vsb, kwb, vwb, cm) = _inproj_call(
            xp, sc1_p, sh1_p, n1, wi[l], cos_p, sin_p, tm=tm_p, tiles_per_seq=tiles_p, per_row_mod=False,
            attn_layouts=True)
        a_out = _attn_prompt_call(q, cm, ksa, vsb, kwb, vwb, sm, batch=batch, seq=seq, tq=tq, tk=tk)
        smt = sm[:, GATE_COLS:GATE_COLS + 2 * M_HEADS].reshape(batch, seq, 2 * M_HEADS).transpose(0, 2, 1)
        m_out, st_c, st_n, st_m = _mlstm_call(
            mqkv, sm, smt, mo, b_if[l], mlstm_norm_g[l],
            jnp.zeros((batch, M_HEADS, M_DK, M_DV), F32), jnp.zeros((batch, M_HEADS, M_DK), F32),
            jnp.zeros((batch, 1, M_HEADS), F32), batch=batch, seq=seq, chunk=chunk_p, valid_len=chunk_p)
        res = _ffn_call(xp, a_out, m_out, wo[l], g1_p, sc2_p, sh2_p, g2_p, n2, wu[l], wc, bc, wd[l],
                        jnp.zeros((batch, CONV_W - 1, ff), F32), fg if last else None,
                        tm=tm_p, tiles_per_seq=tiles_p, shift=1, per_row_mod=False, ch=ch)
        xp, conv_p = res[0], res[1]
        if last:
            y_p = res[2]
        win_len = min(WINDOW, seq)
        for lst, arr in zip(outs_p, (_kv_out(kvc, (batch, seq)), _kv_out(kvs, (batch, seq)),
                                     _kv_out(kvw, (batch, seq))[:, seq - win_len:], st_c, st_n,
                                     st_m.reshape(batch, M_HEADS), conv_p)):
            lst.append(arr)

        (q, kvc, kvs, kvw, mqkv, mo, sm) = _inproj_call(
            xs, sc1_s, sh1_s, n1, wi[l], cos_s, sin_s, tm=rows_s, tiles_per_seq=1, per_row_mod=True,
            attn_layouts=False)
        cm_past = _cmp_means_call(cache_cmp, page_table, l, pages_per_step=2 * pages_per_step)
        q5 = q.reshape(dec_t, dec_b, A_KV, A_G, LANE).transpose(1, 3, 2, 0, 4)
        q5 = jnp.concatenate([q5[:, :, :1], jnp.roll(q5[:, :, 1:], A_HD, axis=-1)], axis=2)
        gl = sm[:, :GATE_COLS].reshape(dec_t, dec_b, A_KV, A_G, 3).transpose(1, 3, 2, 0, 4)
        gl = jnp.pad(gl.reshape(dec_b, A_HEADS * dec_t, 3), ((0, 0), (0, 0), (0, LANE - 3)))
        o_s = _attn_sample_call(page_table, q5.reshape(dec_b, A_HEADS * dec_t, LANE), cm_past, to_bt(kvs),
                                cache_sel, cache_win, to_bt(kvw), gl, l,
                                pages_per_step=pages_per_step, dec_t=dec_t)
        o6 = o_s.reshape(dec_b, A_G, A_KV, dec_t, A_KV, A_HD)
        a_out = jnp.stack([o6[:, :, h, :, h] for h in range(A_KV)], axis=1)
        a_out = a_out.transpose(3, 0, 1, 2, 4).reshape(rows_s, A_W).astype(BF16)
        sm_bt = to_bt(sm)
        m_out, st_c, st_n, st_m = _mlstm_call(
            to_bt(mqkv).reshape(dec_b * t_pad, M_QKV_W), sm_bt.reshape(dec_b * t_pad, LANE),
            sm_bt[:, :, GATE_COLS:GATE_COLS + 2 * M_HEADS].transpose(0, 2, 1), to_bt(mo).reshape(dec_b * t_pad, M_W),
            b_if[l], mlstm_norm_g[l], state_mlstm_C[l], state_mlstm_n[l],
            state_mlstm_m[l].reshape(dec_b, 1, M_HEADS), batch=dec_b, seq=t_pad, chunk=t_pad, valid_len=dec_t)
        m_out = m_out.reshape(dec_b, t_pad, M_W)[:, :dec_t].transpose(1, 0, 2).reshape(rows_s, M_W)
        hist = state_ffn_conv[l].transpose(1, 0, 2).reshape(1, (CONV_W - 1) * dec_b, ff)
        res = _ffn_call(xs, a_out, m_out, wo[l], g1_s, sc2_s, sh2_s, g2_s, n2, wu[l], wc, bc, wd[l],
                        hist, fg if last else None,
                        tm=rows_s, tiles_per_seq=1, shift=dec_b, per_row_mod=True, ch=ch)
        xs, conv_s = res[0], res[1]
        if last:
            y_s = res[2]

        def s_kv(a):
            return _kv_out(a.reshape(dec_t, dec_b, KV_W).transpose(1, 0, 2), (dec_b, dec_t))

        win_s = jnp.concatenate([cache_win[l], kvw.reshape(dec_t, dec_b, KV_W).transpose(1, 0, 2)], axis=1)[:, -wb:]
        for lst, arr in zip(outs_s, (s_kv(kvc), s_kv(kvs), _kv_out(win_s, (dec_b, wb)), st_c, st_n,
                                     st_m.reshape(dec_b, M_HEADS),
                                     conv_s.reshape(CONV_W - 1, dec_b, ff).transpose(1, 0, 2))):
            lst.append(arr)

    y_prompt = y_p.reshape(batch, seq, d)
    y_sample = y_s.reshape(dec_t, dec_b, d).transpose(1, 0, 2)
    return ((y_prompt, y_sample) + tuple(jnp.stack(a, axis=0) for a in outs_p)
            + tuple(jnp.stack(a, axis=0) for a in outs_s))
```

```python
import functools

import numpy as np
import jax
import jax.numpy as jnp
from jax import lax
from jax.experimental import pallas as pl
from jax.experimental.pallas import tpu as pltpu

F32 = jnp.float32
BF16 = jnp.bfloat16
I32 = jnp.int32

A_HD = 64
A_HEADS = 8
A_KV = 2
A_G = A_HEADS // A_KV
CMP_BLOCK = 64
TOPK = 16
WINDOW = 512
ROPE_THETA = 10000.0
FORCE = float(A_G + 1)
M_HEADS = 4
M_DK = 64
M_DV = 128
CONV_W = 3
EPS = 1e-6
PAGE_SIZE = 128

LANE = 128
SUBLANE = 8
BF16_ROWS = 16
VMEM_LIMIT = 56 * 1024 * 1024

A_W = A_HEADS * A_HD
KV_W = 2 * A_KV * A_HD
M_QKV_W = 2 * M_HEADS * M_DK + M_HEADS * M_DV
M_W = M_HEADS * M_DV
OFF_Q = 0
OFF_KC = OFF_Q + A_W
OFF_KS = OFF_KC + KV_W
OFF_KW = OFF_KS + KV_W
OFF_M = OFF_KW + KV_W
OFF_MO = OFF_M + M_QKV_W
OFF_SM = OFF_MO + M_W
IN_W_PAD = OFF_SM + LANE
GATE_COLS = 3 * A_HEADS

LOG2E = 1.4426950408889634
SEL_OFF_BIAS = -(2.0 ** 30)
MASK_NEG = -1e30


def _cparams(sem):
    return pltpu.CompilerParams(dimension_semantics=sem, vmem_limit_bytes=VMEM_LIMIT)


def _resident(shape, index_map):
    return pl.BlockSpec(shape, index_map, pipeline_mode=pl.Buffered(1))


def _mod_kernel(c_ref, w_ref, b_ref, o_ref):
    c = c_ref[...]
    a = (c * jax.nn.sigmoid(c)).astype(BF16)
    o_ref[0] = jnp.dot(a, w_ref[0].astype(BF16), preferred_element_type=F32) + b_ref[0]


def _mod_call(c_all, w_mod, b_mod):
    depth, d, n = w_mod.shape
    rows = c_all.shape[0]
    tn = 1536
    return pl.pallas_call(
        _mod_kernel,
        grid=(depth, n // tn),
        in_specs=[pl.BlockSpec((rows, d), lambda l, j: (0, 0)),
                  pl.BlockSpec((1, d, tn), lambda l, j: (l, 0, j)),
                  pl.BlockSpec((1, 1, tn), lambda l, j: (l, 0, j))],
        out_specs=pl.BlockSpec((1, rows, tn), lambda l, j: (l, 0, j)),
        out_shape=jax.ShapeDtypeStruct((depth, rows, n), F32),
        compiler_params=_cparams(("arbitrary", "arbitrary")),
    )(c_all, w_mod, b_mod.reshape(depth, 1, n))


def _inproj_kernel(x_ref, sc_ref, sh_ref, g_ref, w_ref, cos_ref, sin_ref,
                   q_ref, kvc_ref, kvs_ref, kvw_ref, mqkv_ref, mo_ref, sm_ref, *extra_refs,
                   tm, tiles_per_seq, attn_layouts):
    if attn_layouts:
        ksa_ref, vsb_ref, kwb_ref, vwb_ref, cm_ref = extra_refs
    x = x_ref[...]
    xn = x * lax.rsqrt(jnp.mean(x * x, axis=-1, keepdims=True) + EPS) * g_ref[...]
    xn = xn * (1.0 + sc_ref[0]) + sh_ref[0]
    xb = xn.astype(BF16)
    cos = cos_ref[...]
    sin = sin_ref[...]
    lane = lax.broadcasted_iota(I32, (tm, LANE), 1)
    lo_half = (lane % A_HD) < (A_HD // 2)
    first_head = lane < A_HD

    def proj(lo, hi):
        return jnp.dot(xb, w_ref[:, lo:hi], preferred_element_type=F32)

    def rope(v):
        rot = jnp.where(lo_half, pltpu.roll(v, LANE - A_HD // 2, 1), pltpu.roll(v, A_HD // 2, 1))
        return v * cos + rot * sin

    qa = proj(OFF_Q, OFF_Q + A_W)
    q_scale = A_HD ** -0.5 * (LOG2E if attn_layouts else 1.0)
    for s in range(A_W // LANE):
        r = rope(qa[:, s * LANE:(s + 1) * LANE]) * q_scale
        if attn_layouts:
            q_ref[s * LANE:(s + 1) * LANE, :] = r.T.astype(BF16)
        else:
            q_ref[:, (2 * s) * LANE:(2 * s + 1) * LANE] = jnp.where(first_head, r, 0.0).astype(BF16)
            q_ref[:, (2 * s + 1) * LANE:(2 * s + 2) * LANE] = jnp.where(
                first_head, pltpu.roll(r, A_HD, 1), 0.0).astype(BF16)

    def kv(lo, out_ref):
        a = proj(lo, lo + KV_W)
        k, v = rope(a[:, :LANE]), a[:, LANE:]
        if attn_layouts:
            k_t, v_t = k.T, v.T
            out_ref[0, :LANE, :] = k_t
            out_ref[0, LANE:, :] = v_t
        else:
            k_t = v_t = None
            out_ref[:, :LANE] = k
            out_ref[:, LANE:] = v
        return k, v, v_t

    kc, vc, _ = kv(OFF_KC, kvc_ref)
    ks, vs, vs_t = kv(OFF_KS, kvs_ref)
    kw, vw, vw_t = kv(OFF_KW, kvw_ref)

    if attn_layouts:
        nblk = tm // CMP_BLOCK
        cm_ref[:, :LANE] = jnp.sum(kc.reshape(nblk, CMP_BLOCK, LANE), axis=1) * (1.0 / CMP_BLOCK)
        cm_ref[:, LANE:] = jnp.sum(vc.reshape(nblk, CMP_BLOCK, LANE), axis=1) * (1.0 / CMP_BLOCK)
        row = lax.broadcasted_iota(I32, (tm, LANE), 0)
        blk = (pl.program_id(0) % tiles_per_seq) * nblk + row // CMP_BLOCK
        onehot = jnp.where(lane - A_HD == blk, 1.0, 0.0)
        ksa_ref[:, :LANE] = jnp.where(first_head, ks, onehot).astype(BF16)
        ksa_ref[:, LANE:] = jnp.where(first_head, pltpu.roll(ks, A_HD, 1), onehot).astype(BF16)
        kwb_ref[:, :LANE] = kw.astype(BF16)
        kwb_ref[:, LANE:] = pltpu.roll(kw, A_HD, 1).astype(BF16)
        for ref, v_t in ((vsb_ref, vs_t), (vwb_ref, vw_t)):
            width = ref.shape[-1]
            for j in range(tm // width):
                ref[0, j] = v_t[:, j * width:(j + 1) * width].astype(BF16)

    m = proj(OFF_M, OFF_MO)
    nqk = M_HEADS * M_DK
    mqkv_ref[:, :nqk] = m[:, :nqk]
    mqkv_ref[:, nqk:2 * nqk] = m[:, nqk:2 * nqk] * (M_DK ** -0.5)
    mqkv_ref[:, 2 * nqk:] = m[:, 2 * nqk:]
    mo_ref[...] = proj(OFF_MO, OFF_SM)
    sm_ref[...] = proj(OFF_SM, IN_W_PAD)


def _mod_specs(per_row_mod, tm, d, tiles_per_seq):
    if per_row_mod:
        return pl.BlockSpec((1, tm, d), lambda i: (0, i, 0))
    return pl.BlockSpec((1, 1, d), lambda i: (i // tiles_per_seq, 0, 0))


def _inproj_call(x, sc, sh, g, w, cos, sin, *, tm, tiles_per_seq, per_row_mod, attn_layouts, v_tiles=None):
    rows, d = x.shape
    n_tiles = rows // tm
    n_seq = n_tiles // tiles_per_seq
    seq = tm * tiles_per_seq
    mod_spec = _mod_specs(per_row_mod, tm, d, tiles_per_seq)
    if per_row_mod:
        tab_spec = pl.BlockSpec((tm, LANE), lambda i: (i, 0))
    else:
        tab_spec = pl.BlockSpec((tm, LANE), lambda i: (i % tiles_per_seq, 0))

    def rowspec(width):
        return pl.BlockSpec((tm, width), lambda i: (i, 0))

    if attn_layouts:
        kv_t = jax.ShapeDtypeStruct((n_seq, KV_W, seq), F32)
        kv_t_spec = pl.BlockSpec((1, KV_W, tm), lambda i: (i // tiles_per_seq, 0, i % tiles_per_seq))
        out_shapes = [jax.ShapeDtypeStruct((A_W, rows), BF16), kv_t, kv_t, kv_t]
        out_specs = [pl.BlockSpec((A_W, tm), lambda i: (0, i)), kv_t_spec, kv_t_spec, kv_t_spec]
    else:
        out_shapes = [jax.ShapeDtypeStruct((rows, 2 * A_W), BF16)] + [jax.ShapeDtypeStruct((rows, KV_W), F32)] * 3
        out_specs = [rowspec(2 * A_W), rowspec(KV_W), rowspec(KV_W), rowspec(KV_W)]
    out_shapes += [
        jax.ShapeDtypeStruct((rows, M_QKV_W), F32),
        jax.ShapeDtypeStruct((rows, M_W), F32),
        jax.ShapeDtypeStruct((rows, LANE), F32),
    ]
    out_specs += [rowspec(M_QKV_W), rowspec(M_W), rowspec(LANE)]
    if attn_layouts:
        def v_tiled(width):
            per_tile = tm // width
            return (jax.ShapeDtypeStruct((n_seq, seq // width, LANE, width), BF16),
                    pl.BlockSpec((1, per_tile, LANE, width),
                                 lambda i: (i // tiles_per_seq, i % tiles_per_seq, 0, 0)))
        (vs_shape, vs_spec), (vw_shape, vw_spec) = v_tiled(v_tiles[0]), v_tiled(v_tiles[1])
        out_shapes += [
            jax.ShapeDtypeStruct((rows, 2 * LANE), BF16),
            vs_shape,
            jax.ShapeDtypeStruct((rows, 2 * LANE), BF16),
            vw_shape,
            jax.ShapeDtypeStruct((rows // CMP_BLOCK, KV_W), F32),
        ]
        out_specs += [rowspec(2 * LANE), vs_spec, rowspec(2 * LANE), vw_spec,
                      pl.BlockSpec((tm // CMP_BLOCK, KV_W), lambda i: (i, 0))]
    return pl.pallas_call(
        functools.partial(_inproj_kernel, tm=tm, tiles_per_seq=tiles_per_seq, attn_layouts=attn_layouts),
        grid=(n_tiles,),
        in_specs=[rowspec(d), mod_spec, mod_spec, pl.BlockSpec((1, d), lambda i: (0, 0)),
                  _resident(w.shape, lambda i: (0, 0)), tab_spec, tab_spec],
        out_specs=out_specs,
        out_shape=out_shapes,
        compiler_params=_cparams(("arbitrary",)),
    )(x, sc, sh, g, w, cos, sin)


def _nt(a, b):
    return lax.dot_general(a, b, (((1,), (1,)), ((), ())), preferred_element_type=F32)


def _softmax_update(s, v, carry, v_feature_major=False):
    m, l, acc = carry
    m_new = jnp.maximum(m, jnp.max(s, axis=-1, keepdims=True))
    alpha = jnp.exp(m - m_new)
    p = jnp.exp(s - m_new)
    l = alpha * l + jnp.sum(p, axis=-1, keepdims=True)
    pb = p.astype(BF16)
    pv = _nt(pb, v) if v_feature_major else jnp.dot(pb, v, preferred_element_type=F32)
    return m_new, l, alpha * acc + pv


def _softmax_init(rows, width):
    return (jnp.full((rows, 1), MASK_NEG, F32), jnp.zeros((rows, 1), F32), jnp.zeros((rows, width), F32))


def _masked_softmax(s, ok, axis, base2=False):
    s = jnp.where(ok, s, -jnp.inf)
    smax = jnp.max(s, axis=axis, keepdims=True)
    e = (jnp.exp2 if base2 else jnp.exp)(s - jnp.where(jnp.isfinite(smax), smax, 0.0))
    den = jnp.sum(e, axis=axis, keepdims=True)
    return e / jnp.where(den > 0, den, 1.0)


def _stable_rank_sublanes(score, sub8):
    nb, n = score.shape
    groups = [score[SUBLANE * r:SUBLANE * (r + 1), :] for r in range(nb // SUBLANE)]
    cnts = [jnp.zeros((SUBLANE, n), F32) for _ in groups]
    for ib in range(nb):
        s_i = jnp.broadcast_to(score[ib:ib + 1, :], (SUBLANE, n))
        for r, grp in enumerate(groups):
            if SUBLANE * r > ib:
                one = jnp.where(s_i >= grp, 1.0, 0.0)
            elif SUBLANE * (r + 1) <= ib:
                one = jnp.where(s_i > grp, 1.0, 0.0)
            else:
                one = jnp.where(sub8 > ib - SUBLANE * r, jnp.where(s_i >= grp, 1.0, 0.0),
                                jnp.where(s_i > grp, 1.0, 0.0))
            cnts[r] = cnts[r] + one
    return jnp.concatenate(cnts, axis=0)


def _softmax_update_t(s_t, v_ext, carry):
    m, acc = carry
    m_new = jnp.maximum(m, jnp.max(s_t, axis=0, keepdims=True))
    p = jnp.exp2(s_t - m_new).astype(BF16)
    acc = jnp.exp2(m - m_new) * acc + jnp.dot(v_ext, p, preferred_element_type=F32)
    return m_new, acc


def _softmax_init_t(n, cols):
    return (jnp.full((1, cols), MASK_NEG, F32), jnp.zeros((n, cols), F32))


def _attn_prompt_kernel(qt_ref, cm_ref, ksa_ref, vst_ref, kwb_ref, vwt_ref, sm_ref, o_ref, *, tq, tk, nb):
    i = pl.program_id(1)
    q0 = i * tq
    cols = A_G * tq

    gates_t = jax.nn.sigmoid(sm_ref[...].T[:GATE_COLS, :])
    cm = cm_ref[0]
    cm_pad = jnp.concatenate([cm[:, LANE:], jnp.zeros((LANE - nb, LANE), F32)], axis=0)
    cm_vt = cm_pad.T.astype(BF16)

    blk_t = lax.broadcasted_iota(I32, (nb, tq), 0)
    qpos_t = q0 + lax.broadcasted_iota(I32, (nb, tq), 1)
    cur_t = qpos_t // CMP_BLOCK
    forced_t = (blk_t == 0) | (blk_t == cur_t)
    allowed_t = blk_t <= cur_t
    blk_t4 = lax.broadcasted_iota(I32, (nb, cols), 0)
    qpos_t4 = q0 + (lax.broadcasted_iota(I32, (nb, cols), 1) & (tq - 1))
    cmp_ok_t4 = (blk_t4 + 1) * CMP_BLOCK <= qpos_t4 + 1
    kk_k = lax.broadcasted_iota(I32, (tk, cols), 0)
    qq_k = lax.broadcasted_iota(I32, (tk, cols), 1) & (tq - 1)
    ones_k = jnp.ones((BF16_ROWS, tk), BF16)
    ones_q = jnp.ones((BF16_ROWS, tq), BF16)
    kk_q = lax.broadcasted_iota(I32, (tq, cols), 0)
    qq_q = lax.broadcasted_iota(I32, (tq, cols), 1) & (tq - 1)
    sub8 = lax.broadcasted_iota(I32, (SUBLANE, tq), 0)
    zeros_hd = jnp.zeros((A_HD, tq), BF16)

    def stacked(q_heads, lower):
        return jnp.concatenate([jnp.concatenate([qg, lower], axis=0) for qg in q_heads], axis=1)

    q0t, q_aug, o_cmp = [], [], []
    for h in range(A_KV):
        q_heads = [qt_ref[(A_G * h + g) * A_HD:(A_G * h + g + 1) * A_HD, :] for g in range(A_G)]
        q0t.append(stacked(q_heads, zeros_hd))
        cm_k = cm[:, :LANE] if h == 0 else pltpu.roll(cm[:, :LANE], A_HD, 1)
        p_t = _masked_softmax(jnp.dot(cm_k.astype(BF16), q0t[h], preferred_element_type=F32), cmp_ok_t4, 0,
                              base2=True)
        p_pad = jnp.concatenate([p_t, jnp.zeros((LANE - nb, cols), F32)], axis=0).astype(BF16)
        o_cmp.append(jnp.dot(cm_vt[h * A_HD:(h + 1) * A_HD], p_pad, preferred_element_type=F32))

        imp = p_t[:, 0:tq]
        for g in range(1, A_G):
            imp = imp + p_t[:, g * tq:(g + 1) * tq]
        score = jnp.where(forced_t, FORCE, jnp.where(allowed_t, imp, -1.0))
        cnt = _stable_rank_sublanes(score, sub8)
        bias_t = jnp.where(cnt < TOPK, jnp.where(score >= 0, 0.0, SEL_OFF_BIAS), SEL_OFF_BIAS)
        if nb < A_HD:
            bias_t = jnp.concatenate([bias_t, jnp.zeros((A_HD - nb, tq), F32)], axis=0)
        q_aug.append(stacked(q_heads, bias_t.astype(BF16)))

    def sel_tile(kt, carries, masked):
        k0 = pl.multiple_of(kt * tk, tk)
        out = []
        for h in range(A_KV):
            s = jnp.dot(ksa_ref[0, pl.ds(k0, tk), h * LANE:(h + 1) * LANE], q_aug[h],
                        preferred_element_type=F32)
            if masked:
                s = jnp.where(k0 + kk_k <= q0 + qq_k, s, MASK_NEG)
            v_ext = jnp.concatenate([vst_ref[0, kt, h * A_HD:(h + 1) * A_HD, :], ones_k], axis=0)
            out.append(_softmax_update_t(s, v_ext, carries[h]))
        return tuple(out)

    n_full = q0 // tk
    carries = lax.fori_loop(0, n_full, lambda kt, c: sel_tile(kt, c, False),
                            tuple(_softmax_init_t(A_HD + BF16_ROWS, cols) for _ in range(A_KV)))
    carries = sel_tile(n_full, carries, True)
    o_sel = [acc[:A_HD] / acc[A_HD:A_HD + 1] for _, acc in carries]

    n_wt = WINDOW // tq + 1
    o_win = []
    for h in range(A_KV):
        scores, tiles = [], []
        for wt in range(n_wt):
            start = q0 - wt * tq
            tile = jnp.maximum(i - wt, 0)
            startc = pl.multiple_of(tile * tq, tq)
            s = jnp.dot(kwb_ref[0, pl.ds(startc, tq), h * LANE:(h + 1) * LANE], q0t[h],
                        preferred_element_type=F32)
            if wt == 0:
                s = jnp.where(kk_q <= qq_q, s, MASK_NEG)
            else:
                if wt == n_wt - 1:
                    s = jnp.where(kk_q > qq_q, s, MASK_NEG)
                s = s + jnp.where(start >= 0, 0.0, MASK_NEG)
            scores.append(s)
            tiles.append(tile)
        m = functools.reduce(jnp.maximum, [jnp.max(s, axis=0, keepdims=True) for s in scores])
        acc = functools.reduce(jnp.add, [
            jnp.dot(jnp.concatenate([vwt_ref[0, tile, h * A_HD:(h + 1) * A_HD, :], ones_q], axis=0),
                    jnp.exp2(s - m).astype(BF16), preferred_element_type=F32)
            for tile, s in zip(tiles, scores)])
        o_win.append(acc[:A_HD] / acc[A_HD:A_HD + 1])

    mixes = []
    for h in range(A_KV):
        for g in range(A_G):
            head = A_G * h + g
            sl = (slice(None), slice(g * tq, (g + 1) * tq))
            mixes.append(gates_t[3 * head:3 * head + 1, :] * o_cmp[h][sl] + gates_t[3 * head + 1:3 * head + 2, :] * o_sel[h][sl]
                         + gates_t[3 * head + 2:3 * head + 3, :] * o_win[h][sl])
    o_ref[...] = jnp.concatenate(mixes, axis=0).T.astype(o_ref.dtype)


def _attn_prompt_call(qt, cm, ksa, vst, kwb, vwt, sm, *, batch, seq, tq, tk):
    nb = seq // CMP_BLOCK
    tiles = seq // tq
    assert nb <= A_HD and tq % LANE == 0 and tk % tq == 0 and seq % tk == 0 and WINDOW % tq == 0

    def full(width):
        return pl.BlockSpec((1, seq, width), lambda b, i: (b, 0, 0))

    def full_t(arr):
        return pl.BlockSpec((1,) + arr.shape[1:], lambda b, i: (b, 0, 0, 0))

    return pl.pallas_call(
        functools.partial(_attn_prompt_kernel, tq=tq, tk=tk, nb=nb),
        grid=(batch, tiles),
        in_specs=[pl.BlockSpec((A_W, tq), lambda b, i: (0, b * tiles + i)),
                  pl.BlockSpec((1, nb, KV_W), lambda b, i: (b, 0, 0)),
                  full(2 * LANE), full_t(vst), full(2 * LANE), full_t(vwt),
                  pl.BlockSpec((tq, LANE), lambda b, i: (b * tiles + i, 0))],
        out_specs=pl.BlockSpec((tq, A_W), lambda b, i: (b * tiles + i, 0)),
        out_shape=jax.ShapeDtypeStruct((batch * seq, A_W), BF16),
        compiler_params=_cparams(("arbitrary", "arbitrary")),
    )(qt, cm.reshape(batch, nb, KV_W), ksa.reshape(batch, seq, 2 * LANE), vst, kwb.reshape(batch, seq, 2 * LANE),
      vwt, sm)


def _page_spec(k, pages_per_step, layer):
    def index(b, j, pt):
        return (layer, pt[b, j * pages_per_step + k], 0, 0, 0, 0)
    return pl.BlockSpec((1, 1, 2, A_KV, A_HD, PAGE_SIZE), index)


def _page_kv(ref):
    n_pos = ref.shape[-1]
    return ref[0, 0, 0].reshape(A_KV * A_HD, n_pos), ref[0, 0, 1].reshape(A_KV * A_HD, n_pos)


def _cmp_means_kernel(pt_ref, *refs, n_pages):
    del pt_ref
    o_ref = refs[n_pages]
    per_page = PAGE_SIZE // CMP_BLOCK
    nbp = n_pages * per_page
    pos_blk = lax.broadcasted_iota(I32, (PAGE_SIZE, nbp), 0) // CMP_BLOCK
    blk = lax.broadcasted_iota(I32, (PAGE_SIZE, nbp), 1)
    acc = jnp.zeros((KV_W, nbp), F32)
    for k in range(n_pages):
        page = refs[k][0, 0].reshape(KV_W, PAGE_SIZE)
        place = jnp.where(blk == per_page * k + pos_blk, 1.0, 0.0).astype(BF16)
        for part in _split3(page):
            acc = acc + jnp.dot(part, place, preferred_element_type=F32)
    o_ref[0] = acc * (1.0 / CMP_BLOCK)


def _cmp_means_call(cache, page_table, layer):
    dec_b, n_pages = page_table.shape
    nbp = n_pages * (PAGE_SIZE // CMP_BLOCK)
    grid_spec = pltpu.PrefetchScalarGridSpec(
        num_scalar_prefetch=1,
        grid=(dec_b, 1),
        in_specs=[_page_spec(k, n_pages, layer) for k in range(n_pages)],
        out_specs=pl.BlockSpec((1, KV_W, nbp), lambda b, j, pt: (b, 0, 0)),
    )
    return pl.pallas_call(
        functools.partial(_cmp_means_kernel, n_pages=n_pages),
        grid_spec=grid_spec,
        out_shape=jax.ShapeDtypeStruct((dec_b, KV_W, nbp), F32),
        compiler_params=_cparams(("arbitrary", "arbitrary")),
    )(page_table, *([cache] * n_pages))


def _attn_sample_kernel(pt_ref, q_ref, cm_ref, kn_ref, win_ref, wn_ref, gl_ref, *refs,
                        pages_per_step, dec_t, past_len, nbp):
    del pt_ref
    page_refs = refs[:pages_per_step]
    o_ref = refs[pages_per_step]
    m_scr, l_scr, acc_scr, bias_scr, ocmp_scr = refs[pages_per_step + 1:]
    j = pl.program_id(1)
    rows = q_ref.shape[1]
    grp = A_KV * dec_t
    tk = pages_per_step * PAGE_SIZE
    qb = q_ref[0]
    t_row = lax.broadcasted_iota(I32, (rows, 1), 0) % dec_t

    @pl.when(j == 0)
    def _():
        cm_t = cm_ref[0]
        blk = lax.broadcasted_iota(I32, (rows, nbp), 1)
        ok = (blk + 1) * CMP_BLOCK <= past_len + t_row + 1
        p = _masked_softmax(jnp.dot(qb, cm_t[:LANE].astype(BF16), preferred_element_type=F32), ok, 1)
        ocmp_scr[...] = _nt(p.astype(BF16), cm_t[LANE:].astype(BF16))
        imp = p[0:grp]
        for g in range(1, A_G):
            imp = imp + p[g * grp:(g + 1) * grp]
        blk8 = lax.broadcasted_iota(I32, (grp, nbp), 1)
        cur = (past_len + lax.broadcasted_iota(I32, (grp, nbp), 0) % dec_t) // CMP_BLOCK
        score = jnp.where(blk8 == 0, FORCE, jnp.where(blk8 == cur, FORCE, jnp.where(blk8 <= cur, imp, -1.0)))
        cnt = jnp.where(FORCE > score, 1.0, 0.0)
        for ib in range(nbp):
            c_i = score[:, ib:ib + 1]
            cnt = cnt + jnp.where(blk8 > ib, jnp.where(c_i >= score, 1.0, 0.0), jnp.where(c_i > score, 1.0, 0.0))
        bias = jnp.where(cnt < TOPK, jnp.where(score >= 0, 0.0, SEL_OFF_BIAS), SEL_OFF_BIAS)
        bias_scr[...] = jnp.concatenate([bias] * A_G, axis=0).astype(BF16)
        m_scr[...] = jnp.full(m_scr.shape, MASK_NEG, F32)
        l_scr[...] = jnp.zeros(l_scr.shape, F32)
        acc_scr[...] = jnp.zeros(acc_scr.shape, F32)

    pages = [_page_kv(r) for r in page_refs]
    keys_t = jnp.concatenate([k.astype(BF16) for k, _ in pages], axis=1)
    vals_t = jnp.concatenate([v.astype(BF16) for _, v in pages], axis=1)
    blk_e = lax.broadcasted_iota(I32, (nbp, tk), 0)
    key_blk = j * (tk // CMP_BLOCK) + lax.broadcasted_iota(I32, (nbp, tk), 1) // CMP_BLOCK
    expand = jnp.where(blk_e == key_blk, 1.0, 0.0).astype(BF16)
    s = (jnp.dot(qb, keys_t, preferred_element_type=F32)
         + jnp.dot(bias_scr[...], expand, preferred_element_type=F32))
    m, l, acc = _softmax_update(s, vals_t, (m_scr[...], l_scr[...], acc_scr[...]), v_feature_major=True)
    m_scr[...] = m
    l_scr[...] = l
    acc_scr[...] = acc

    @pl.when(j == pl.num_programs(1) - 1)
    def _():
        new_ok = lax.broadcasted_iota(I32, (rows, kn_ref.shape[1]), 1) <= t_row

        def new_rows(carry, ref):
            kv = ref[0]
            s_n = jnp.where(new_ok, _nt(qb, kv[:, :LANE].astype(BF16)), MASK_NEG)
            return _softmax_update(s_n, kv[:, LANE:].astype(BF16), carry)

        _, l_s, acc_s = new_rows((m_scr[...], l_scr[...], acc_scr[...]), kn_ref)
        o_sel = acc_s / l_s

        kw_t, vw_t = _page_kv(win_ref)
        wb = kw_t.shape[1]
        jw = lax.broadcasted_iota(I32, (rows, wb), 1)
        s_w = jnp.dot(qb, kw_t.astype(BF16), preferred_element_type=F32)
        s_w = jnp.where(jw > t_row + (wb - WINDOW), s_w, MASK_NEG)
        carry = _softmax_update(s_w, vw_t.astype(BF16), _softmax_init(rows, LANE), v_feature_major=True)
        _, l_w, acc_w = new_rows(carry, wn_ref)
        o_win = acc_w / l_w

        gate = jax.nn.sigmoid(gl_ref[0])
        o_ref[0] = gate[:, 0:1] * ocmp_scr[...] + gate[:, 1:2] * o_sel + gate[:, 2:3] * o_win


def _attn_sample_call(page_table, q, cm, kv_new, cache_sel, cache_win, win_new, gate_logits, layer, *,
                      pages_per_step, dec_t):
    dec_b, n_pages = page_table.shape
    rows = q.shape[1]
    nbp = cm.shape[2]
    wb = cache_win.shape[-1]
    past_len = n_pages * PAGE_SIZE
    assert nbp == LANE and past_len >= wb and wb >= WINDOW

    def per_b(shape):
        return pl.BlockSpec((1,) + shape, lambda b, j, pt: (b, 0, 0))

    grid_spec = pltpu.PrefetchScalarGridSpec(
        num_scalar_prefetch=1,
        grid=(dec_b, n_pages // pages_per_step),
        in_specs=[per_b((rows, LANE)), per_b((KV_W, nbp)), per_b(kv_new.shape[1:]),
                  pl.BlockSpec((1, 1, 2, A_KV, A_HD, wb), lambda b, j, pt: (layer, b, 0, 0, 0, 0)),
                  per_b(win_new.shape[1:]), per_b((rows, LANE))]
                 + [_page_spec(k, pages_per_step, layer) for k in range(pages_per_step)],
        out_specs=per_b((rows, LANE)),
        scratch_shapes=[pltpu.VMEM((rows, 1), F32), pltpu.VMEM((rows, 1), F32), pltpu.VMEM((rows, LANE), F32),
                        pltpu.VMEM((rows, nbp), BF16), pltpu.VMEM((rows, LANE), F32)],
    )
    return pl.pallas_call(
        functools.partial(_attn_sample_kernel, pages_per_step=pages_per_step, dec_t=dec_t,
                          past_len=past_len, nbp=nbp),
        grid_spec=grid_spec,
        out_shape=jax.ShapeDtypeStruct((dec_b, rows, LANE), F32),
        compiler_params=_cparams(("arbitrary", "arbitrary")),
    )(page_table, q, cm, kv_new, cache_win, win_new, gate_logits, *([cache_sel] * pages_per_step))


def _split3(a):
    hi = a.astype(BF16)
    r1 = a - hi.astype(F32)
    mid = r1.astype(BF16)
    lo = (r1 - mid.astype(F32)).astype(BF16)
    return hi, mid, lo


def _log_sigmoid(x):
    return jnp.minimum(x, 0.0) - jnp.log1p(jnp.exp(-jnp.abs(x)))


def _mlstm_kernel(qkv_ref, sm_ref, smt_ref, mo_ref, bc_ref, br_ref, g_ref, c0_ref, n0_ref, m0_ref,
                  o_ref, c_ref, n_ref, m_ref, *, chunk, valid_len):
    @pl.when(pl.program_id(1) == 0)
    def _():
        c_ref[...] = c0_ref[...]
        n_ref[...] = n0_ref[...]
        m_ref[...] = m0_ref[...]

    ng = 2 * M_HEADS
    gate_c = sm_ref[:, GATE_COLS:GATE_COLS + ng] + bc_ref[...]
    gate_r = smt_ref[0] + br_ref[...]
    lf_c = _log_sigmoid(gate_c)
    lf_r = _log_sigmoid(gate_r)
    ig_c, ig_r = gate_c, gate_r
    if valid_len < chunk:
        tc = lax.broadcasted_iota(I32, (chunk, ng), 0)
        tr = lax.broadcasted_iota(I32, (ng, chunk), 1)
        lf_c = jnp.where(tc < valid_len, lf_c, 0.0)
        lf_r = jnp.where(tr < valid_len, lf_r, 0.0)
        ig_c = jnp.where(tc < valid_len, ig_c, MASK_NEG)
        ig_r = jnp.where(tr < valid_len, ig_r, MASK_NEG)

    t_i = lax.broadcasted_iota(I32, (chunk, chunk), 0)
    s_i = lax.broadcasted_iota(I32, (chunk, chunk), 1)
    causal = s_i <= t_i
    tri = jnp.where(causal, 1.0, 0.0).astype(BF16)
    tri_t = jnp.where(t_i <= s_i, 1.0, 0.0).astype(BF16)
    b_c = sum(jnp.dot(tri, part, preferred_element_type=F32) for part in _split3(lf_c))
    b_r = sum(jnp.dot(part, tri_t, preferred_element_type=F32) for part in _split3(lf_r))

    nqk = M_HEADS * M_DK
    for n in range(M_HEADS):
        q = qkv_ref[:, n * M_DK:(n + 1) * M_DK]
        k = qkv_ref[:, nqk + n * M_DK:nqk + (n + 1) * M_DK]
        v = qkv_ref[:, 2 * nqk + n * M_DV:2 * nqk + (n + 1) * M_DV]
        qb, kb, vb = q.astype(BF16), k.astype(BF16), v.astype(BF16)
        fcol = M_HEADS + n
        bc = b_c[:, fcol:fcol + 1]
        br = b_r[fcol:fcol + 1, :]
        igc = ig_c[:, n:n + 1]
        igr = ig_r[n:n + 1, :]
        b_end = bc[chunk - 1:chunk, :]
        m_prev = m_ref[0, :, n:n + 1]
        c_prev = c_ref[0, n]
        n_prev = n_ref[0, n:n + 1, :]

        a = bc + m_prev
        d = jnp.where(causal, bc + (igr - br), -jnp.inf)
        m_t = jnp.maximum(a, jnp.max(d, axis=1, keepdims=True))
        w = _nt(qb, kb) * jnp.exp(d - m_t)
        aw = jnp.exp(a - m_t)
        num = (jnp.dot(w.astype(BF16), vb, preferred_element_type=F32)
               + aw * jnp.dot(qb, c_prev.astype(BF16), preferred_element_type=F32))
        den = jnp.sum(w, axis=1, keepdims=True) + aw * jnp.sum(q * n_prev, axis=1, keepdims=True)
        hcell = num / jnp.maximum(jnp.abs(den), jnp.exp(-m_t))

        hn = hcell * lax.rsqrt(jnp.mean(hcell * hcell, axis=-1, keepdims=True) + EPS) * g_ref[:, n * M_DV:(n + 1) * M_DV]
        o_ref[:, n * M_DV:(n + 1) * M_DV] = (jax.nn.sigmoid(mo_ref[:, n * M_DV:(n + 1) * M_DV]) * hn).astype(o_ref.dtype)

        wl = b_end - bc + igc
        m_new = jnp.maximum(b_end + m_prev, jnp.max(wl, axis=0, keepdims=True))
        decay = jnp.exp(b_end + m_prev - m_new)
        kws = k * jnp.exp(wl - m_new)
        c_ref[0, n] = decay * c_prev + lax.dot_general(kws.astype(BF16), vb, (((0,), (0,)), ((), ())),
                                                       preferred_element_type=F32)
        n_ref[0, n:n + 1, :] = decay * n_prev + jnp.sum(kws, axis=0, keepdims=True)
        m_ref[0, :, n:n + 1] = m_new


def _mlstm_call(qkv, sm, smt, mo, b_if, norm_g, c0, n0, m0, *, batch, seq, chunk, valid_len):
    nc = seq // chunk
    ng = 2 * M_HEADS

    def rowspec(width):
        return pl.BlockSpec((chunk, width), lambda b, c: (b * nc + c, 0))

    def const(shape):
        return pl.BlockSpec(shape, lambda b, c: (0,) * len(shape))

    state_specs = [pl.BlockSpec((1, M_HEADS, M_DK, M_DV), lambda b, c: (b, 0, 0, 0)),
                   pl.BlockSpec((1, M_HEADS, M_DK), lambda b, c: (b, 0, 0)),
                   pl.BlockSpec((1, 1, M_HEADS), lambda b, c: (b, 0, 0))]
    return pl.pallas_call(
        functools.partial(_mlstm_kernel, chunk=chunk, valid_len=valid_len),
        grid=(batch, nc),
        in_specs=[rowspec(M_QKV_W), rowspec(LANE), pl.BlockSpec((1, ng, chunk), lambda b, c: (b, 0, c)),
                  rowspec(M_W), const((1, ng)), const((ng, 1)), const((1, M_W))] + state_specs,
        out_specs=[rowspec(M_W)] + state_specs,
        out_shape=[jax.ShapeDtypeStruct((batch * seq, M_W), BF16),
                   jax.ShapeDtypeStruct((batch, M_HEADS, M_DK, M_DV), F32),
                   jax.ShapeDtypeStruct((batch, M_HEADS, M_DK), F32),
                   jax.ShapeDtypeStruct((batch, 1, M_HEADS), F32)],
        compiler_params=_cparams(("arbitrary", "arbitrary")),
    )(qkv, sm, smt, mo, b_if.reshape(1, ng), b_if.reshape(ng, 1), norm_g.reshape(1, M_W), c0, n0, m0)


def _ffn_kernel(x_ref, a_ref, mh_ref, wo_ref, g1_ref, sc_ref, sh_ref, g2_ref, n2_ref, wu_ref, wc_ref, bc_ref,
                wd_ref, hist_ref, *refs, tm, tiles_per_seq, shift, ff, ch, final):
    if final:
        fg_ref, xo_ref, cs_ref, y_ref, carry_scr, up_scr = refs
    else:
        xo_ref, cs_ref, carry_scr, up_scr = refs
    hist_rows = (CONV_W - 1) * shift
    hoff = up_scr.shape[0] - tm

    @pl.when(pl.program_id(0) % tiles_per_seq == 0)
    def _():
        carry_scr[...] = hist_ref[0]

    y = (jnp.dot(a_ref[...], wo_ref[:A_W, :], preferred_element_type=F32)
         + jnp.dot(mh_ref[...], wo_ref[A_W:, :], preferred_element_type=F32))
    x1 = x_ref[...] + g1_ref[0] * y
    xn = x1 * lax.rsqrt(jnp.mean(x1 * x1, axis=-1, keepdims=True) + EPS) * n2_ref[...]
    xb = (xn * (1.0 + sc_ref[0]) + sh_ref[0]).astype(BF16)

    acc = jnp.zeros(x1.shape, F32)
    for c in range(ff // ch):
        lo, hi = c * ch, (c + 1) * ch
        u = jnp.dot(xb, wu_ref[:, lo:hi], preferred_element_type=F32)
        gt = jnp.dot(xb, wu_ref[:, ff + lo:ff + hi], preferred_element_type=F32)
        up_scr[hoff - hist_rows:hoff, :] = carry_scr[:, lo:hi]
        up_scr[hoff:, :] = u
        tail = u[tm - hist_rows:, :]
        carry_scr[:, lo:hi] = tail
        cs_ref[0, :, lo:hi] = tail
        conv = bc_ref[:, lo:hi]
        for jj in range(CONV_W - 1):
            start = hoff - (CONV_W - 1 - jj) * shift
            conv = conv + wc_ref[jj:jj + 1, lo:hi] * up_scr[start:start + tm, :]
        conv = conv + wc_ref[CONV_W - 1:CONV_W, lo:hi] * u
        hid = conv * jax.nn.sigmoid(conv) * gt
        acc = acc + jnp.dot(hid.astype(BF16), wd_ref[lo:hi, :], preferred_element_type=F32)

    x2 = x1 + g2_ref[0] * acc
    xo_ref[...] = x2
    if final:
        y_ref[...] = x2 * lax.rsqrt(jnp.mean(x2 * x2, axis=-1, keepdims=True) + EPS) * fg_ref[...]


def _ffn_call(x, a_out, m_out, wo, g1, sc, sh, g2, n2, wu, wc, bc, wd, hist, final_g, *,
              tm, tiles_per_seq, shift, per_row_mod, ch):
    rows, d = x.shape
    ff = wd.shape[0]
    groups, hist_rows, _ = hist.shape
    assert hist_rows == (CONV_W - 1) * shift and tm >= hist_rows and ff % ch == 0
    final = final_g is not None
    hoff = -(-hist_rows // SUBLANE) * SUBLANE
    mod_spec = _mod_specs(per_row_mod, tm, d, tiles_per_seq)

    def rowspec(width):
        return pl.BlockSpec((tm, width), lambda i: (i, 0))

    def const(arr):
        return _resident(arr.shape, lambda i: (0,) * arr.ndim)

    hist_spec = pl.BlockSpec((1, hist_rows, ff), lambda i: (i // tiles_per_seq, 0, 0))
    in_specs = [rowspec(d), rowspec(A_W), rowspec(M_W), const(wo), mod_spec, mod_spec, mod_spec, mod_spec,
                pl.BlockSpec((1, d), lambda i: (0, 0)), const(wu), const(wc), const(bc), const(wd), hist_spec]
    args = [x, a_out, m_out, wo, g1, sc, sh, g2, n2, wu, wc, bc, wd, hist]
    out_specs = [rowspec(d), hist_spec]
    out_shape = [jax.ShapeDtypeStruct((rows, d), F32), jax.ShapeDtypeStruct(hist.shape, F32)]
    if final:
        in_specs.append(pl.BlockSpec((1, d), lambda i: (0, 0)))
        args.append(final_g)
        out_specs.append(rowspec(d))
        out_shape.append(jax.ShapeDtypeStruct((rows, d), F32))
    return pl.pallas_call(
        functools.partial(_ffn_kernel, tm=tm, tiles_per_seq=tiles_per_seq, shift=shift, ff=ff, ch=ch, final=final),
        grid=(rows // tm,),
        in_specs=in_specs,
        out_specs=out_specs,
        out_shape=out_shape,
        scratch_shapes=[pltpu.VMEM((hist_rows, ff), F32), pltpu.VMEM((hoff + tm, ch), F32)],
        compiler_params=_cparams(("arbitrary",)),
    )(*args)


def _prep_w_in(w_in):
    sizes = (A_W, KV_W, KV_W, KV_W, GATE_COLS, M_HEADS * M_DK, M_HEADS * M_DK, M_W, M_W, 2 * M_HEADS)
    points = [int(s) for s in np.cumsum(sizes)[:-1]]
    a_q, a_kc, a_ks, a_kw, a_g, m_q, m_k, m_v, m_o, m_if = jnp.split(w_in, points, axis=-1)
    pad = jnp.zeros(w_in.shape[:-1] + (LANE - GATE_COLS - 2 * M_HEADS,), w_in.dtype)
    return jnp.concatenate([a_q, a_kc, a_ks, a_kw, m_q, m_k, m_v, m_o, a_g, m_if, pad], axis=-1).astype(BF16)


def _rope_tables(pos):
    half = A_HD // 2
    freq = ROPE_THETA ** (-2.0 * jnp.arange(half, dtype=F32) / A_HD)
    ang = pos.astype(F32)[:, None] * freq[None, :]
    cos, sin = jnp.cos(ang), jnp.sin(ang)
    return jnp.concatenate([cos, cos, cos, cos], axis=-1), jnp.concatenate([-sin, sin, -sin, sin], axis=-1)


def _kv_out(kv, lead):
    return kv.reshape(lead + (2, A_KV, A_HD))


def kernel(x_prompt, x_sample, cache_cmp_kv, cache_sel_kv, cache_win_kv, state_mlstm_C, state_mlstm_n,
           state_mlstm_m, state_ffn_conv, page_table, c_prompt, c_sample, norm1_g, norm2_g, w_mod, b_mod,
           w_in, b_if, mlstm_norm_g, w_out, w_up, w_conv, b_conv, w_down, final_g):
    batch, seq, d = x_prompt.shape
    dec_b, dec_t, _ = x_sample.shape
    depth = w_in.shape[0]
    ff = w_down.shape[1]
    n_pages = page_table.shape[1]
    past_len = n_pages * PAGE_SIZE
    wb = cache_win_kv.shape[2]
    rows_s = dec_t * dec_b
    assert A_KV * dec_t == SUBLANE and dec_t >= CONV_W - 1 and dec_b % SUBLANE == 0

    tm_p = min(512, seq)
    tq, tk = 512, min(512, seq)
    chunk_p = min(256, seq)
    ch = ff // 2
    pages_per_step = 8
    t_pad = BF16_ROWS

    wi = _prep_w_in(w_in)
    wo, wu, wd = w_out.astype(BF16), w_up.astype(BF16), w_down.astype(BF16)
    fg = final_g.reshape(1, d)

    n_c = batch + dec_b
    c_all = jnp.concatenate([c_prompt, c_sample, jnp.zeros((-n_c % SUBLANE, d), F32)], axis=0)
    mod = _mod_call(c_all, w_mod, b_mod)

    cos_p, sin_p = _rope_tables(jnp.arange(seq))
    cos_s, sin_s = [jnp.repeat(t, dec_b, axis=0) for t in _rope_tables(past_len + jnp.arange(dec_t))]

    cache_cmp = cache_cmp_kv.transpose(0, 1, 3, 4, 5, 2)
    cache_sel = cache_sel_kv.transpose(0, 1, 3, 4, 5, 2)
    cache_win = cache_win_kv.transpose(0, 1, 3, 4, 5, 2)

    def kv_from_t(kv_t):
        return kv_t.reshape(kv_t.shape[0], 2, A_KV, A_HD, kv_t.shape[-1]).transpose(0, 4, 1, 2, 3)

    def to_bt(a):
        a = a.reshape(dec_t, dec_b, a.shape[-1]).transpose(1, 0, 2)
        return jnp.pad(a, ((0, 0), (0, t_pad - dec_t), (0, 0)))

    xp = x_prompt.reshape(batch * seq, d)
    xs = x_sample.transpose(1, 0, 2).reshape(rows_s, d)
    outs_p = [[] for _ in range(7)]
    outs_s = [[] for _ in range(7)]
    y_p = y_s = None
    for l in range(depth):
        last = l == depth - 1
        mods = [mod[l, :, k * d:(k + 1) * d] for k in range(6)]
        sh1_p, sc1_p, g1_p, sh2_p, sc2_p, g2_p = [m[:batch].reshape(batch, 1, d) for m in mods]
        sh1_s, sc1_s, g1_s, sh2_s, sc2_s, g2_s = [jnp.tile(m[batch:n_c], (dec_t, 1)).reshape(1, rows_s, d)
                                                  for m in mods]
        n1, n2 = norm1_g[l].reshape(1, d), norm2_g[l].reshape(1, d)
        wc, bc = w_conv[l], b_conv[l].reshape(1, ff)

        tiles_p = seq // tm_p
        (qt, kvc_t, kvs_t, kvw_t, mqkv, mo, sm, ksa, vst, kwb, vwt, cm) = _inproj_call(
            xp, sc1_p, sh1_p, n1, wi[l], cos_p, sin_p, tm=tm_p, tiles_per_seq=tiles_p, per_row_mod=False,
            attn_layouts=True, v_tiles=(tk, tq))
        a_out = _attn_prompt_call(qt, cm, ksa, vst, kwb, vwt, sm, batch=batch, seq=seq, tq=tq, tk=tk)
        smt = sm[:, GATE_COLS:GATE_COLS + 2 * M_HEADS].reshape(batch, seq, 2 * M_HEADS).transpose(0, 2, 1)
        m_out, st_c, st_n, st_m = _mlstm_call(
            mqkv, sm, smt, mo, b_if[l], mlstm_norm_g[l],
            jnp.zeros((batch, M_HEADS, M_DK, M_DV), F32), jnp.zeros((batch, M_HEADS, M_DK), F32),
            jnp.zeros((batch, 1, M_HEADS), F32), batch=batch, seq=seq, chunk=chunk_p, valid_len=chunk_p)
        res = _ffn_call(xp, a_out, m_out, wo[l], g1_p, sc2_p, sh2_p, g2_p, n2, wu[l], wc, bc, wd[l],
                        jnp.zeros((batch, CONV_W - 1, ff), F32), fg if last else None,
                        tm=tm_p, tiles_per_seq=tiles_p, shift=1, per_row_mod=False, ch=ch)
        xp, conv_p = res[0], res[1]
        if last:
            y_p = res[2]
        win_len = min(WINDOW, seq)
        for lst, arr in zip(outs_p, (kv_from_t(kvc_t), kv_from_t(kvs_t), kv_from_t(kvw_t[:, :, seq - win_len:]),
                                     st_c, st_n, st_m.reshape(batch, M_HEADS), conv_p)):
            lst.append(arr)

        (q, kvc, kvs, kvw, mqkv, mo, sm) = _inproj_call(
            xs, sc1_s, sh1_s, n1, wi[l], cos_s, sin_s, tm=rows_s, tiles_per_seq=1, per_row_mod=True,
            attn_layouts=False)
        cm_past = _cmp_means_call(cache_cmp, page_table, l)
        q5 = q.reshape(dec_t, dec_b, A_KV, A_G, LANE).transpose(1, 3, 2, 0, 4)
        q5 = jnp.concatenate([q5[:, :, :1], jnp.roll(q5[:, :, 1:], A_HD, axis=-1)], axis=2)
        gl = sm[:, :GATE_COLS].reshape(dec_t, dec_b, A_KV, A_G, 3).transpose(1, 3, 2, 0, 4)
        gl = jnp.pad(gl.reshape(dec_b, A_HEADS * dec_t, 3), ((0, 0), (0, 0), (0, LANE - 3)))
        o_s = _attn_sample_call(page_table, q5.reshape(dec_b, A_HEADS * dec_t, LANE), cm_past, to_bt(kvs),
                                cache_sel, cache_win, to_bt(kvw), gl, l,
                                pages_per_step=pages_per_step, dec_t=dec_t)
        o6 = o_s.reshape(dec_b, A_G, A_KV, dec_t, A_KV, A_HD)
        a_out = jnp.stack([o6[:, :, h, :, h] for h in range(A_KV)], axis=1)
        a_out = a_out.transpose(3, 0, 1, 2, 4).reshape(rows_s, A_W).astype(BF16)
        sm_bt = to_bt(sm)
        m_out, st_c, st_n, st_m = _mlstm_call(
            to_bt(mqkv).reshape(dec_b * t_pad, M_QKV_W), sm_bt.reshape(dec_b * t_pad, LANE),
            sm_bt[:, :, GATE_COLS:GATE_COLS + 2 * M_HEADS].transpose(0, 2, 1), to_bt(mo).reshape(dec_b * t_pad, M_W),
            b_if[l], mlstm_norm_g[l], state_mlstm_C[l], state_mlstm_n[l],
            state_mlstm_m[l].reshape(dec_b, 1, M_HEADS), batch=dec_b, seq=t_pad, chunk=t_pad, valid_len=dec_t)
        m_out = m_out.reshape(dec_b, t_pad, M_W)[:, :dec_t].transpose(1, 0, 2).reshape(rows_s, M_W)
        hist = state_ffn_conv[l].transpose(1, 0, 2).reshape(1, (CONV_W - 1) * dec_b, ff)
        res = _ffn_call(xs, a_out, m_out, wo[l], g1_s, sc2_s, sh2_s, g2_s, n2, wu[l], wc, bc, wd[l],
                        hist, fg if last else None,
                        tm=rows_s, tiles_per_seq=1, shift=dec_b, per_row_mod=True, ch=ch)
        xs, conv_s = res[0], res[1]
        if last:
            y_s = res[2]

        def s_kv(a):
            return _kv_out(a.reshape(dec_t, dec_b, KV_W).transpose(1, 0, 2), (dec_b, dec_t))

        kvw_new_t = kvw.reshape(dec_t, dec_b, 2, A_KV, A_HD).transpose(1, 2, 3, 4, 0)
        win_t = jnp.concatenate([cache_win[l], kvw_new_t], axis=-1)[..., -wb:]
        for lst, arr in zip(outs_s, (s_kv(kvc), s_kv(kvs), win_t.transpose(0, 4, 1, 2, 3), st_c, st_n,
                                     st_m.reshape(dec_b, M_HEADS),
                                     conv_s.reshape(CONV_W - 1, dec_b, ff).transpose(1, 0, 2))):
            lst.append(arr)

    y_prompt = y_p.reshape(batch, seq, d)
    y_sample = y_s.reshape(dec_t, dec_b, d).transpose(1, 0, 2)
    return ((y_prompt, y_sample) + tuple(jnp.stack(a, axis=0) for a in outs_p)
            + tuple(jnp.stack(a, axis=0) for a in outs_s))
```

```python
import functools

import numpy as np
import jax
import jax.numpy as jnp
from jax import lax
from jax.experimental import pallas as pl
from jax.experimental.pallas import tpu as pltpu

F32 = jnp.float32
BF16 = jnp.bfloat16
I32 = jnp.int32

A_HD = 64
A_HEADS = 8
A_KV = 2
A_G = A_HEADS // A_KV
CMP_BLOCK = 64
TOPK = 16
WINDOW = 512
ROPE_THETA = 10000.0
FORCE = float(A_G + 1)
M_HEADS = 4
M_DK = 64
M_DV = 128
CONV_W = 3
EPS = 1e-6
PAGE_SIZE = 128

LANE = 128
SUBLANE = 8
BF16_ROWS = 16
MXU_TILE = 256
VMEM_LIMIT = 56 * 1024 * 1024

A_W = A_HEADS * A_HD
KV_W = 2 * A_KV * A_HD
M_QKV_W = 2 * M_HEADS * M_DK + M_HEADS * M_DV
M_W = M_HEADS * M_DV
OFF_Q = 0
OFF_KC = OFF_Q + A_W
OFF_KS = OFF_KC + KV_W
OFF_KW = OFF_KS + KV_W
OFF_M = OFF_KW + KV_W
OFF_MO = OFF_M + M_QKV_W
OFF_SM = OFF_MO + M_W
IN_W_PAD = OFF_SM + LANE
GATE_COLS = 3 * A_HEADS

LOG2E = 1.4426950408889634
SEL_OFF_BIAS = -(2.0 ** 30)
MASK_NEG = -1e30


def _cparams(sem):
    return pltpu.CompilerParams(dimension_semantics=sem, vmem_limit_bytes=VMEM_LIMIT)


def _resident(shape, index_map):
    return pl.BlockSpec(shape, index_map, pipeline_mode=pl.Buffered(1))


def _mod_kernel(c_ref, w_ref, b_ref, o_ref):
    c = c_ref[...]
    a = (c * jax.nn.sigmoid(c)).astype(BF16)
    o_ref[0] = jnp.dot(a, w_ref[0].astype(BF16), preferred_element_type=F32) + b_ref[0]


def _mod_call(c_all, w_mod, b_mod):
    depth, d, n = w_mod.shape
    rows = c_all.shape[0]
    tn = 1536
    return pl.pallas_call(
        _mod_kernel,
        grid=(depth, n // tn),
        in_specs=[pl.BlockSpec((rows, d), lambda l, j: (0, 0)),
                  pl.BlockSpec((1, d, tn), lambda l, j: (l, 0, j)),
                  pl.BlockSpec((1, 1, tn), lambda l, j: (l, 0, j))],
        out_specs=pl.BlockSpec((1, rows, tn), lambda l, j: (l, 0, j)),
        out_shape=jax.ShapeDtypeStruct((depth, rows, n), F32),
        compiler_params=_cparams(("arbitrary", "arbitrary")),
    )(c_all, w_mod, b_mod.reshape(depth, 1, n))


def _inproj_kernel(x_ref, sc_ref, sh_ref, g_ref, w_ref, cos_ref, sin_ref, *refs, tm, tiles_per_seq, attn_layouts):
    if attn_layouts:
        (q_ref, kvc_ref, kvs_ref, kvw_ref, mqt_ref, mk_ref, mvt_ref, mot_ref, sm_ref,
         ksa_ref, vsb_ref, kwb_ref, vwb_ref, cm_ref) = refs[3:]
    else:
        q_ref, kvc_ref, kvs_ref, kvw_ref, mqkv_ref, mo_ref, sm_ref = refs
    x = x_ref[...]
    xn = x * lax.rsqrt(jnp.mean(x * x, axis=-1, keepdims=True) + EPS) * g_ref[...]
    xn = xn * (1.0 + sc_ref[0]) + sh_ref[0]
    xb = xn.astype(BF16)
    cos = cos_ref[...]
    sin = sin_ref[...]
    lane = lax.broadcasted_iota(I32, (tm, LANE), 1)
    lo_half = (lane % A_HD) < (A_HD // 2)
    first_head = lane < A_HD

    def proj(lo, hi):
        return jnp.dot(xb, w_ref[:, lo:hi], preferred_element_type=F32)

    def rope(v):
        rot = jnp.where(lo_half, pltpu.roll(v, LANE - A_HD // 2, 1), pltpu.roll(v, A_HD // 2, 1))
        return v * cos + rot * sin

    qa = proj(OFF_Q, OFF_Q + A_W)
    q_scale = A_HD ** -0.5 * (LOG2E if attn_layouts else 1.0)
    for s in range(A_W // LANE):
        r = rope(qa[:, s * LANE:(s + 1) * LANE]) * q_scale
        if attn_layouts:
            q_ref[s * LANE:(s + 1) * LANE, :] = r.T.astype(BF16)
        else:
            q_ref[:, (2 * s) * LANE:(2 * s + 1) * LANE] = jnp.where(first_head, r, 0.0).astype(BF16)
            q_ref[:, (2 * s + 1) * LANE:(2 * s + 2) * LANE] = jnp.where(
                first_head, pltpu.roll(r, A_HD, 1), 0.0).astype(BF16)

    def kv(lo, out_ref):
        a = proj(lo, lo + KV_W)
        k, v = rope(a[:, :LANE]), a[:, LANE:]
        if attn_layouts:
            k_t, v_t = k.T, v.T
            out_ref[0, 0, :LANE, :] = k_t
            out_ref[0, 0, LANE:, :] = v_t
        else:
            k_t = v_t = None
            out_ref[:, :LANE] = k
            out_ref[:, LANE:] = v
        return k, v, v_t

    kc, vc, _ = kv(OFF_KC, kvc_ref)
    ks, vs, vs_t = kv(OFF_KS, kvs_ref)
    kw, vw, vw_t = kv(OFF_KW, kvw_ref)

    if attn_layouts:
        nblk = tm // CMP_BLOCK
        cm_ref[:, :LANE] = jnp.sum(kc.reshape(nblk, CMP_BLOCK, LANE), axis=1) * (1.0 / CMP_BLOCK)
        cm_ref[:, LANE:] = jnp.sum(vc.reshape(nblk, CMP_BLOCK, LANE), axis=1) * (1.0 / CMP_BLOCK)
        row = lax.broadcasted_iota(I32, (tm, LANE), 0)
        blk = (pl.program_id(0) % tiles_per_seq) * nblk + row // CMP_BLOCK
        onehot = jnp.where(lane - A_HD == blk, 1.0, 0.0)
        ksa_ref[:, :LANE] = jnp.where(first_head, ks, onehot).astype(BF16)
        ksa_ref[:, LANE:] = jnp.where(first_head, pltpu.roll(ks, A_HD, 1), onehot).astype(BF16)
        kwb_ref[:, :LANE] = kw.astype(BF16)
        kwb_ref[:, LANE:] = pltpu.roll(kw, A_HD, 1).astype(BF16)
        for ref, v_t in ((vsb_ref, vs_t), (vwb_ref, vw_t)):
            width = ref.shape[-1]
            for j in range(tm // width):
                ref[0, j] = v_t[:, j * width:(j + 1) * width].astype(BF16)

    m = proj(OFF_M, OFF_MO)
    mo = proj(OFF_MO, OFF_SM)
    nqk = M_HEADS * M_DK
    mk = m[:, nqk:2 * nqk] * (M_DK ** -0.5)
    if attn_layouts:
        for s in range(nqk // LANE):
            mqt_ref[s * LANE:(s + 1) * LANE, :] = m[:, s * LANE:(s + 1) * LANE].T.astype(BF16)
        mk_ref[...] = mk.astype(BF16)
        for s in range(M_W // LANE):
            mvt_ref[s * LANE:(s + 1) * LANE, :] = m[:, 2 * nqk + s * LANE:2 * nqk + (s + 1) * LANE].T
            mot_ref[s * LANE:(s + 1) * LANE, :] = mo[:, s * LANE:(s + 1) * LANE].T
    else:
        mqkv_ref[:, :nqk] = m[:, :nqk]
        mqkv_ref[:, nqk:2 * nqk] = mk
        mqkv_ref[:, 2 * nqk:] = m[:, 2 * nqk:]
        mo_ref[...] = mo
    sm_ref[...] = proj(OFF_SM, IN_W_PAD)


def _mod_specs(per_row_mod, tm, d, tiles_per_seq):
    if per_row_mod:
        return pl.BlockSpec((1, tm, d), lambda i: (0, i, 0))
    return pl.BlockSpec((1, 1, d), lambda i: (i // tiles_per_seq, 0, 0))


def _inproj_call(x, sc, sh, g, w, cos, sin, *, tm, tiles_per_seq, per_row_mod, attn_layouts, v_tiles=None,
                 kv_stacks=(), layer=0):
    rows, d = x.shape
    n_tiles = rows // tm
    n_seq = n_tiles // tiles_per_seq
    seq = tm * tiles_per_seq
    mod_spec = _mod_specs(per_row_mod, tm, d, tiles_per_seq)
    if per_row_mod:
        tab_spec = pl.BlockSpec((tm, LANE), lambda i: (i, 0))
    else:
        tab_spec = pl.BlockSpec((tm, LANE), lambda i: (i % tiles_per_seq, 0))

    def rowspec(width):
        return pl.BlockSpec((tm, width), lambda i: (i, 0))

    if attn_layouts:
        kv_t = jax.ShapeDtypeStruct(kv_stacks[0].shape, F32)
        kv_t_spec = pl.BlockSpec((1, 1, KV_W, tm), lambda i: (layer, i // tiles_per_seq, 0, i % tiles_per_seq))
        out_shapes = [jax.ShapeDtypeStruct((A_W, rows), BF16), kv_t, kv_t, kv_t]
        out_specs = [pl.BlockSpec((A_W, tm), lambda i: (0, i)), kv_t_spec, kv_t_spec, kv_t_spec]
    else:
        out_shapes = [jax.ShapeDtypeStruct((rows, 2 * A_W), BF16)] + [jax.ShapeDtypeStruct((rows, KV_W), F32)] * 3
        out_specs = [rowspec(2 * A_W), rowspec(KV_W), rowspec(KV_W), rowspec(KV_W)]
    if attn_layouts:
        def feat_major(n, dtype):
            return jax.ShapeDtypeStruct((n, rows), dtype), pl.BlockSpec((n, tm), lambda i: (0, i))
        nqk = M_HEADS * M_DK
        pairs = [feat_major(nqk, BF16),
                 (jax.ShapeDtypeStruct((rows, nqk), BF16), rowspec(nqk)),
                 feat_major(M_W, F32), feat_major(M_W, F32)]
    else:
        pairs = [(jax.ShapeDtypeStruct((rows, M_QKV_W), F32), rowspec(M_QKV_W)),
                 (jax.ShapeDtypeStruct((rows, M_W), F32), rowspec(M_W))]
    pairs.append((jax.ShapeDtypeStruct((rows, LANE), F32), rowspec(LANE)))
    out_shapes += [p[0] for p in pairs]
    out_specs += [p[1] for p in pairs]
    if attn_layouts:
        def v_tiled(width):
            per_tile = tm // width
            return (jax.ShapeDtypeStruct((n_seq, seq // width, LANE, width), BF16),
                    pl.BlockSpec((1, per_tile, LANE, width),
                                 lambda i: (i // tiles_per_seq, i % tiles_per_seq, 0, 0)))
        (vs_shape, vs_spec), (vw_shape, vw_spec) = v_tiled(v_tiles[0]), v_tiled(v_tiles[1])
        out_shapes += [
            jax.ShapeDtypeStruct((rows, 2 * LANE), BF16),
            vs_shape,
            jax.ShapeDtypeStruct((rows, 2 * LANE), BF16),
            vw_shape,
            jax.ShapeDtypeStruct((rows // CMP_BLOCK, KV_W), F32),
        ]
        out_specs += [rowspec(2 * LANE), vs_spec, rowspec(2 * LANE), vw_spec,
                      pl.BlockSpec((tm // CMP_BLOCK, KV_W), lambda i: (i, 0))]
    return pl.pallas_call(
        functools.partial(_inproj_kernel, tm=tm, tiles_per_seq=tiles_per_seq, attn_layouts=attn_layouts),
        grid=(n_tiles,),
        in_specs=[rowspec(d), mod_spec, mod_spec, pl.BlockSpec((1, d), lambda i: (0, 0)),
                  _resident(w.shape, lambda i: (0, 0)), tab_spec, tab_spec]
                 + [pl.BlockSpec(memory_space=pl.ANY)] * len(kv_stacks),
        out_specs=out_specs,
        out_shape=out_shapes,
        input_output_aliases={7 + k: 1 + k for k in range(len(kv_stacks))},
        compiler_params=_cparams(("arbitrary",)),
    )(x, sc, sh, g, w, cos, sin, *kv_stacks)


def _nt(a, b):
    return lax.dot_general(a, b, (((1,), (1,)), ((), ())), preferred_element_type=F32)


def _softmax_update(s, v, carry, v_feature_major=False):
    m, l, acc = carry
    m_new = jnp.maximum(m, jnp.max(s, axis=-1, keepdims=True))
    alpha = jnp.exp(m - m_new)
    p = jnp.exp(s - m_new)
    l = alpha * l + jnp.sum(p, axis=-1, keepdims=True)
    pb = p.astype(BF16)
    pv = _nt(pb, v) if v_feature_major else jnp.dot(pb, v, preferred_element_type=F32)
    return m_new, l, alpha * acc + pv


def _softmax_init(rows, width):
    return (jnp.full((rows, 1), MASK_NEG, F32), jnp.zeros((rows, 1), F32), jnp.zeros((rows, width), F32))


def _masked_softmax(s, ok, axis, base2=False):
    s = jnp.where(ok, s, -jnp.inf)
    smax = jnp.max(s, axis=axis, keepdims=True)
    e = (jnp.exp2 if base2 else jnp.exp)(s - jnp.where(jnp.isfinite(smax), smax, 0.0))
    den = jnp.sum(e, axis=axis, keepdims=True)
    return e / jnp.where(den > 0, den, 1.0)


def _stable_rank_sublanes(score, sub8):
    nb, n = score.shape
    groups = [score[SUBLANE * r:SUBLANE * (r + 1), :] for r in range(nb // SUBLANE)]
    cnts = [jnp.zeros((SUBLANE, n), F32) for _ in groups]
    for ib in range(nb):
        s_i = jnp.broadcast_to(score[ib:ib + 1, :], (SUBLANE, n))
        for r, grp in enumerate(groups):
            if SUBLANE * r > ib:
                one = jnp.where(s_i >= grp, 1.0, 0.0)
            elif SUBLANE * (r + 1) <= ib:
                one = jnp.where(s_i > grp, 1.0, 0.0)
            else:
                one = jnp.where(sub8 > ib - SUBLANE * r, jnp.where(s_i >= grp, 1.0, 0.0),
                                jnp.where(s_i > grp, 1.0, 0.0))
            cnts[r] = cnts[r] + one
    return jnp.concatenate(cnts, axis=0)


def _softmax_update_t(s_t, v_ext, carry):
    m, acc = carry
    m_new = jnp.maximum(m, jnp.max(s_t, axis=0, keepdims=True))
    p = jnp.exp2(s_t - m_new).astype(BF16)
    acc = jnp.exp2(m - m_new) * acc + jnp.dot(v_ext, p, preferred_element_type=F32)
    return m_new, acc


def _softmax_init_t(n, cols):
    return (jnp.full((1, cols), MASK_NEG, F32), jnp.zeros((n, cols), F32))


def _attn_prompt_kernel(qt_ref, cm_ref, ksa_ref, vst_ref, kwb_ref, vwt_ref, sm_ref, o_ref, *, tq, tk, nb):
    i = pl.program_id(1)
    q0 = i * tq
    cols = A_G * tq

    gates_t = jax.nn.sigmoid(sm_ref[...].T[:GATE_COLS, :])
    cm = cm_ref[0]
    cm_pad = jnp.concatenate([cm[:, LANE:], jnp.zeros((LANE - nb, LANE), F32)], axis=0)
    cm_vt = cm_pad.T.astype(BF16)

    blk_t = lax.broadcasted_iota(I32, (nb, tq), 0)
    qpos_t = q0 + lax.broadcasted_iota(I32, (nb, tq), 1)
    cur_t = qpos_t // CMP_BLOCK
    forced_t = (blk_t == 0) | (blk_t == cur_t)
    allowed_t = blk_t <= cur_t
    blk_t4 = lax.broadcasted_iota(I32, (nb, cols), 0)
    qpos_t4 = q0 + (lax.broadcasted_iota(I32, (nb, cols), 1) & (tq - 1))
    cmp_ok_t4 = (blk_t4 + 1) * CMP_BLOCK <= qpos_t4 + 1
    cg = max(tq, cols // 2)
    n_cg = cols // cg
    kk_k = lax.broadcasted_iota(I32, (tk, cg), 0)
    qq_k = lax.broadcasted_iota(I32, (tk, cg), 1) & (tq - 1)
    ones_k = jnp.ones((BF16_ROWS, tk), BF16)
    ones_q = jnp.ones((BF16_ROWS, tq), BF16)
    kk_q = lax.broadcasted_iota(I32, (tq, cg), 0)
    qq_q = lax.broadcasted_iota(I32, (tq, cg), 1) & (tq - 1)
    sub8 = lax.broadcasted_iota(I32, (SUBLANE, tq), 0)
    zeros_hd = jnp.zeros((A_HD, tq), BF16)

    def stacked(q_heads, lower):
        return jnp.concatenate([jnp.concatenate([qg, lower], axis=0) for qg in q_heads], axis=1)

    q0t, q_aug, o_cmp = [], [], []
    for h in range(A_KV):
        q_heads = [qt_ref[(A_G * h + g) * A_HD:(A_G * h + g + 1) * A_HD, :] for g in range(A_G)]
        q0t.append(stacked(q_heads, zeros_hd))
        cm_k = cm[:, :LANE] if h == 0 else pltpu.roll(cm[:, :LANE], A_HD, 1)
        p_t = _masked_softmax(jnp.dot(cm_k.astype(BF16), q0t[h], preferred_element_type=F32), cmp_ok_t4, 0,
                              base2=True)
        p_pad = jnp.concatenate([p_t, jnp.zeros((LANE - nb, cols), F32)], axis=0).astype(BF16)
        o_cmp.append(jnp.dot(cm_vt[h * A_HD:(h + 1) * A_HD], p_pad, preferred_element_type=F32))

        imp = p_t[:, 0:tq]
        for g in range(1, A_G):
            imp = imp + p_t[:, g * tq:(g + 1) * tq]
        score = jnp.where(forced_t, FORCE, jnp.where(allowed_t, imp, -1.0))
        cnt = _stable_rank_sublanes(score, sub8)
        bias_t = jnp.where(cnt < TOPK, jnp.where(score >= 0, 0.0, SEL_OFF_BIAS), SEL_OFF_BIAS)
        if nb < A_HD:
            bias_t = jnp.concatenate([bias_t, jnp.zeros((A_HD - nb, tq), F32)], axis=0)
        q_aug.append(stacked(q_heads, bias_t.astype(BF16)))

    def sel_tile(kt, carries, masked):
        k0 = pl.multiple_of(kt * tk, tk)
        scores = []
        for h in range(A_KV):
            for c in range(n_cg):
                s = jnp.dot(ksa_ref[0, pl.ds(k0, tk), h * LANE:(h + 1) * LANE],
                            q_aug[h][:, c * cg:(c + 1) * cg], preferred_element_type=F32)
                if masked:
                    s = jnp.where(k0 + kk_k <= q0 + qq_k, s, MASK_NEG)
                scores.append(s)
        out = []
        for h in range(A_KV):
            v_ext = jnp.concatenate([vst_ref[0, kt, h * A_HD:(h + 1) * A_HD, :], ones_k], axis=0)
            for c in range(n_cg):
                out.append(_softmax_update_t(scores[h * n_cg + c], v_ext, carries[h * n_cg + c]))
        return tuple(out)

    n_full = q0 // tk
    carries = lax.fori_loop(0, n_full, lambda kt, c: sel_tile(kt, c, False),
                            tuple(_softmax_init_t(A_HD + BF16_ROWS, cg) for _ in range(A_KV * n_cg)))
    carries = sel_tile(n_full, carries, True)
    o_sel = [jnp.concatenate([acc[:A_HD] / acc[A_HD:A_HD + 1] for _, acc in carries[h * n_cg:(h + 1) * n_cg]], axis=1)
             for h in range(A_KV)]

    n_wt = WINDOW // tq + 1
    tiles = [jnp.maximum(i - wt, 0) for wt in range(n_wt)]
    win_scores = []
    for h in range(A_KV):
        for c in range(n_cg):
            per_tile = []
            for wt in range(n_wt):
                startc = pl.multiple_of(tiles[wt] * tq, tq)
                s = jnp.dot(kwb_ref[0, pl.ds(startc, tq), h * LANE:(h + 1) * LANE],
                            q0t[h][:, c * cg:(c + 1) * cg], preferred_element_type=F32)
                if wt == 0:
                    s = jnp.where(kk_q <= qq_q, s, MASK_NEG)
                else:
                    if wt == n_wt - 1:
                        s = jnp.where(kk_q > qq_q, s, MASK_NEG)
                    s = s + jnp.where(q0 - wt * tq >= 0, 0.0, MASK_NEG)
                per_tile.append(s)
            win_scores.append(per_tile)
    o_win = []
    for h in range(A_KV):
        v_ext = [jnp.concatenate([vwt_ref[0, tile, h * A_HD:(h + 1) * A_HD, :], ones_q], axis=0) for tile in tiles]
        outs = []
        for c in range(n_cg):
            per_tile = win_scores[h * n_cg + c]
            m = functools.reduce(jnp.maximum, [jnp.max(s, axis=0, keepdims=True) for s in per_tile])
            acc = functools.reduce(jnp.add, [jnp.dot(v, jnp.exp2(s - m).astype(BF16), preferred_element_type=F32)
                                             for v, s in zip(v_ext, per_tile)])
            outs.append(acc[:A_HD] / acc[A_HD:A_HD + 1])
        o_win.append(jnp.concatenate(outs, axis=1))

    mixes = []
    for h in range(A_KV):
        for g in range(A_G):
            head = A_G * h + g
            sl = (slice(None), slice(g * tq, (g + 1) * tq))
            mixes.append(gates_t[3 * head:3 * head + 1, :] * o_cmp[h][sl] + gates_t[3 * head + 1:3 * head + 2, :] * o_sel[h][sl]
                         + gates_t[3 * head + 2:3 * head + 3, :] * o_win[h][sl])
    o_ref[...] = jnp.concatenate(mixes, axis=0).T.astype(o_ref.dtype)


def _attn_prompt_call(qt, cm, ksa, vst, kwb, vwt, sm, *, batch, seq, tq, tk):
    nb = seq // CMP_BLOCK
    tiles = seq // tq
    assert nb <= A_HD and tq % LANE == 0 and tk % tq == 0 and seq % tk == 0 and WINDOW % tq == 0

    def full(width):
        return pl.BlockSpec((1, seq, width), lambda b, i: (b, 0, 0))

    def full_t(arr):
        return pl.BlockSpec((1,) + arr.shape[1:], lambda b, i: (b, 0, 0, 0))

    return pl.pallas_call(
        functools.partial(_attn_prompt_kernel, tq=tq, tk=tk, nb=nb),
        grid=(batch, tiles),
        in_specs=[pl.BlockSpec((A_W, tq), lambda b, i: (0, b * tiles + i)),
                  pl.BlockSpec((1, nb, KV_W), lambda b, i: (b, 0, 0)),
                  full(2 * LANE), full_t(vst), full(2 * LANE), full_t(vwt),
                  pl.BlockSpec((tq, LANE), lambda b, i: (b * tiles + i, 0))],
        out_specs=pl.BlockSpec((tq, A_W), lambda b, i: (b * tiles + i, 0)),
        out_shape=jax.ShapeDtypeStruct((batch * seq, A_W), BF16),
        compiler_params=_cparams(("arbitrary", "arbitrary")),
    )(qt, cm.reshape(batch, nb, KV_W), ksa.reshape(batch, seq, 2 * LANE), vst, kwb.reshape(batch, seq, 2 * LANE),
      vwt, sm)


def _page_spec(k, pages_per_step, layer):
    def index(b, j, pt):
        return (layer, pt[b, j * pages_per_step + k], 0, 0, 0, 0)
    return pl.BlockSpec((1, 1, 2, A_KV, A_HD, PAGE_SIZE), index)


def _page_kv(ref):
    n_pos = ref.shape[-1]
    return ref[0, 0, 0].reshape(A_KV * A_HD, n_pos), ref[0, 0, 1].reshape(A_KV * A_HD, n_pos)


def _cmp_means_kernel(pt_ref, *refs, n_pages):
    del pt_ref
    o_ref = refs[n_pages]
    per_page = PAGE_SIZE // CMP_BLOCK
    nbp = n_pages * per_page
    pos_blk = lax.broadcasted_iota(I32, (PAGE_SIZE, nbp), 0) // CMP_BLOCK
    blk = lax.broadcasted_iota(I32, (PAGE_SIZE, nbp), 1)
    acc = jnp.zeros((KV_W, nbp), F32)
    for k in range(n_pages):
        page = refs[k][0, 0].reshape(KV_W, PAGE_SIZE)
        place = jnp.where(blk == per_page * k + pos_blk, 1.0, 0.0).astype(BF16)
        for part in _split_bf16(page, 2):
            acc = acc + jnp.dot(part, place, preferred_element_type=F32)
    o_ref[0] = acc * (1.0 / CMP_BLOCK)


def _cmp_means_call(cache, page_table, layer):
    dec_b, n_pages = page_table.shape
    nbp = n_pages * (PAGE_SIZE // CMP_BLOCK)
    grid_spec = pltpu.PrefetchScalarGridSpec(
        num_scalar_prefetch=1,
        grid=(dec_b, 1),
        in_specs=[_page_spec(k, n_pages, layer) for k in range(n_pages)],
        out_specs=pl.BlockSpec((1, KV_W, nbp), lambda b, j, pt: (b, 0, 0)),
    )
    return pl.pallas_call(
        functools.partial(_cmp_means_kernel, n_pages=n_pages),
        grid_spec=grid_spec,
        out_shape=jax.ShapeDtypeStruct((dec_b, KV_W, nbp), F32),
        compiler_params=_cparams(("arbitrary", "arbitrary")),
    )(page_table, *([cache] * n_pages))


def _attn_sample_kernel(pt_ref, q_ref, cm_ref, kn_ref, win_ref, wn_ref, gl_ref, *refs,
                        pages_per_step, dec_t, past_len, nbp):
    del pt_ref
    page_refs = refs[:pages_per_step]
    o_ref = refs[pages_per_step]
    m_scr, l_scr, acc_scr, bias_scr, expand_scr, ocmp_scr = refs[pages_per_step + 1:]
    j = pl.program_id(1)
    rows = q_ref.shape[1]
    grp = A_KV * dec_t
    tk = pages_per_step * PAGE_SIZE
    blocks_per_step = tk // CMP_BLOCK
    qb = q_ref[0]
    t_row = lax.broadcasted_iota(I32, (rows, 1), 0) % dec_t

    @pl.when(j == 0)
    def _():
        cm_t = cm_ref[0]
        blk = lax.broadcasted_iota(I32, (rows, nbp), 1)
        ok = (blk + 1) * CMP_BLOCK <= past_len + t_row + 1
        p = _masked_softmax(jnp.dot(qb, cm_t[:LANE].astype(BF16), preferred_element_type=F32), ok, 1)
        ocmp_scr[...] = _nt(p.astype(BF16), cm_t[LANE:].astype(BF16))
        imp = p[0:grp]
        for g in range(1, A_G):
            imp = imp + p[g * grp:(g + 1) * grp]
        blk8 = lax.broadcasted_iota(I32, (grp, nbp), 1)
        cur = (past_len + lax.broadcasted_iota(I32, (grp, nbp), 0) % dec_t) // CMP_BLOCK
        score = jnp.where(blk8 == 0, FORCE, jnp.where(blk8 == cur, FORCE, jnp.where(blk8 <= cur, imp, -1.0)))
        cnt = jnp.where(FORCE > score, 1.0, 0.0)
        for ib in range(nbp):
            c_i = score[:, ib:ib + 1]
            cnt = cnt + jnp.where(blk8 > ib, jnp.where(c_i >= score, 1.0, 0.0), jnp.where(c_i > score, 1.0, 0.0))
        bias = jnp.where(cnt < TOPK, jnp.where(score >= 0, 0.0, SEL_OFF_BIAS), SEL_OFF_BIAS)
        bias = jnp.concatenate([bias] * A_G, axis=0)
        for jj in range(bias_scr.shape[0]):
            bias_scr[jj] = pltpu.roll(bias, (nbp - jj * blocks_per_step) % nbp, 1).astype(BF16)
        blk_e = lax.broadcasted_iota(I32, (nbp, tk), 0)
        key_blk = lax.broadcasted_iota(I32, (nbp, tk), 1) // CMP_BLOCK
        expand_scr[...] = jnp.where(blk_e == key_blk, 1.0, 0.0).astype(BF16)
        m_scr[...] = jnp.full(m_scr.shape, MASK_NEG, F32)
        l_scr[...] = jnp.zeros(l_scr.shape, F32)
        acc_scr[...] = jnp.zeros(acc_scr.shape, F32)

    pages = [_page_kv(r) for r in page_refs]
    keys_t = jnp.concatenate([k.astype(BF16) for k, _ in pages], axis=1)
    vals_t = jnp.concatenate([v.astype(BF16) for _, v in pages], axis=1)
    s = (jnp.dot(qb, keys_t, preferred_element_type=F32)
         + jnp.dot(bias_scr[j], expand_scr[...], preferred_element_type=F32))
    m, l, acc = _softmax_update(s, vals_t, (m_scr[...], l_scr[...], acc_scr[...]), v_feature_major=True)
    m_scr[...] = m
    l_scr[...] = l
    acc_scr[...] = acc

    @pl.when(j == pl.num_programs(1) - 1)
    def _():
        new_ok = lax.broadcasted_iota(I32, (rows, kn_ref.shape[1]), 1) <= t_row

        def new_rows(carry, ref):
            kv = ref[0]
            s_n = jnp.where(new_ok, _nt(qb, kv[:, :LANE].astype(BF16)), MASK_NEG)
            return _softmax_update(s_n, kv[:, LANE:].astype(BF16), carry)

        _, l_s, acc_s = new_rows((m_scr[...], l_scr[...], acc_scr[...]), kn_ref)
        o_sel = acc_s / l_s

        kw_t, vw_t = _page_kv(win_ref)
        wb = kw_t.shape[1]
        jw = lax.broadcasted_iota(I32, (rows, wb), 1)
        s_w = jnp.dot(qb, kw_t.astype(BF16), preferred_element_type=F32)
        s_w = jnp.where(jw > t_row + (wb - WINDOW), s_w, MASK_NEG)
        carry = _softmax_update(s_w, vw_t.astype(BF16), _softmax_init(rows, LANE), v_feature_major=True)
        _, l_w, acc_w = new_rows(carry, wn_ref)
        o_win = acc_w / l_w

        gate = jax.nn.sigmoid(gl_ref[0])
        o_ref[0] = gate[:, 0:1] * ocmp_scr[...] + gate[:, 1:2] * o_sel + gate[:, 2:3] * o_win


def _attn_sample_call(page_table, q, cm, kv_new, cache_sel, cache_win, win_new, gate_logits, layer, *,
                      pages_per_step, dec_t):
    dec_b, n_pages = page_table.shape
    rows = q.shape[1]
    nbp = cm.shape[2]
    wb = cache_win.shape[-1]
    past_len = n_pages * PAGE_SIZE
    assert nbp == LANE and past_len >= wb and wb >= WINDOW

    def per_b(shape):
        return pl.BlockSpec((1,) + shape, lambda b, j, pt: (b, 0, 0))

    grid_spec = pltpu.PrefetchScalarGridSpec(
        num_scalar_prefetch=1,
        grid=(dec_b, n_pages // pages_per_step),
        in_specs=[per_b((rows, LANE)), per_b((KV_W, nbp)), per_b(kv_new.shape[1:]),
                  pl.BlockSpec((1, 1, 2, A_KV, A_HD, wb), lambda b, j, pt: (layer, b, 0, 0, 0, 0)),
                  per_b(win_new.shape[1:]), per_b((rows, LANE))]
                 + [_page_spec(k, pages_per_step, layer) for k in range(pages_per_step)],
        out_specs=per_b((rows, LANE)),
        scratch_shapes=[pltpu.VMEM((rows, 1), F32), pltpu.VMEM((rows, 1), F32), pltpu.VMEM((rows, LANE), F32),
                        pltpu.VMEM((n_pages // pages_per_step, rows, nbp), BF16),
                        pltpu.VMEM((nbp, pages_per_step * PAGE_SIZE), BF16), pltpu.VMEM((rows, LANE), F32)],
    )
    return pl.pallas_call(
        functools.partial(_attn_sample_kernel, pages_per_step=pages_per_step, dec_t=dec_t,
                          past_len=past_len, nbp=nbp),
        grid_spec=grid_spec,
        out_shape=jax.ShapeDtypeStruct((dec_b, rows, LANE), F32),
        compiler_params=_cparams(("arbitrary", "arbitrary")),
    )(page_table, q, cm, kv_new, cache_win, win_new, gate_logits, *([cache_sel] * pages_per_step))


def _split_bf16(a, terms):
    parts = []
    for _ in range(terms):
        piece = a.astype(BF16)
        parts.append(piece)
        a = a - piece.astype(F32)
    return parts


def _log_sigmoid(x):
    return jnp.minimum(x, 0.0) - jnp.log1p(jnp.exp(-jnp.abs(x)))


def _mlstm_kernel(qkv_ref, sm_ref, smt_ref, mo_ref, bc_ref, br_ref, g_ref, c0_ref, n0_ref, m0_ref,
                  o_ref, c_ref, n_ref, m_ref, *, chunk, valid_len):
    @pl.when(pl.program_id(1) == 0)
    def _():
        c_ref[...] = c0_ref[...]
        n_ref[...] = n0_ref[...]
        m_ref[...] = m0_ref[...]

    ng = 2 * M_HEADS
    gate_c = sm_ref[:, GATE_COLS:GATE_COLS + ng] + bc_ref[...]
    gate_r = smt_ref[0] + br_ref[...]
    lf_c = _log_sigmoid(gate_c)
    lf_r = _log_sigmoid(gate_r)
    ig_c, ig_r = gate_c, gate_r
    if valid_len < chunk:
        tc = lax.broadcasted_iota(I32, (chunk, ng), 0)
        tr = lax.broadcasted_iota(I32, (ng, chunk), 1)
        lf_c = jnp.where(tc < valid_len, lf_c, 0.0)
        lf_r = jnp.where(tr < valid_len, lf_r, 0.0)
        ig_c = jnp.where(tc < valid_len, ig_c, MASK_NEG)
        ig_r = jnp.where(tr < valid_len, ig_r, MASK_NEG)

    t_i = lax.broadcasted_iota(I32, (chunk, chunk), 0)
    s_i = lax.broadcasted_iota(I32, (chunk, chunk), 1)
    causal = s_i <= t_i
    tri = jnp.where(causal, 1.0, 0.0).astype(BF16)
    tri_t = jnp.where(t_i <= s_i, 1.0, 0.0).astype(BF16)
    b_c = sum(jnp.dot(tri, part, preferred_element_type=F32) for part in _split_bf16(lf_c, 3))
    b_r = sum(jnp.dot(part, tri_t, preferred_element_type=F32) for part in _split_bf16(lf_r, 3))

    nqk = M_HEADS * M_DK
    for n in range(M_HEADS):
        q = qkv_ref[:, n * M_DK:(n + 1) * M_DK]
        k = qkv_ref[:, nqk + n * M_DK:nqk + (n + 1) * M_DK]
        v = qkv_ref[:, 2 * nqk + n * M_DV:2 * nqk + (n + 1) * M_DV]
        qb, kb, vb = q.astype(BF16), k.astype(BF16), v.astype(BF16)
        fcol = M_HEADS + n
        bc = b_c[:, fcol:fcol + 1]
        br = b_r[fcol:fcol + 1, :]
        igc = ig_c[:, n:n + 1]
        igr = ig_r[n:n + 1, :]
        b_end = bc[chunk - 1:chunk, :]
        m_prev = m_ref[0, :, n:n + 1]
        c_prev = c_ref[0, n]
        n_prev = n_ref[0, n:n + 1, :]

        a = bc + m_prev
        d = jnp.where(causal, bc + (igr - br), -jnp.inf)
        m_t = jnp.maximum(a, jnp.max(d, axis=1, keepdims=True))
        w = _nt(qb, kb) * jnp.exp(d - m_t)
        aw = jnp.exp(a - m_t)
        num = (jnp.dot(w.astype(BF16), vb, preferred_element_type=F32)
               + aw * jnp.dot(qb, c_prev.astype(BF16), preferred_element_type=F32))
        den = jnp.sum(w, axis=1, keepdims=True) + aw * jnp.sum(q * n_prev, axis=1, keepdims=True)
        hcell = num / jnp.maximum(jnp.abs(den), jnp.exp(-m_t))

        hn = hcell * lax.rsqrt(jnp.mean(hcell * hcell, axis=-1, keepdims=True) + EPS) * g_ref[:, n * M_DV:(n + 1) * M_DV]
        o_ref[:, n * M_DV:(n + 1) * M_DV] = (jax.nn.sigmoid(mo_ref[:, n * M_DV:(n + 1) * M_DV]) * hn).astype(o_ref.dtype)

        wl = b_end - bc + igc
        m_new = jnp.maximum(b_end + m_prev, jnp.max(wl, axis=0, keepdims=True))
        decay = jnp.exp(b_end + m_prev - m_new)
        kws = k * jnp.exp(wl - m_new)
        c_ref[0, n] = decay * c_prev + lax.dot_general(kws.astype(BF16), vb, (((0,), (0,)), ((), ())),
                                                       preferred_element_type=F32)
        n_ref[0, n:n + 1, :] = decay * n_prev + jnp.sum(kws, axis=0, keepdims=True)
        m_ref[0, :, n:n + 1] = m_new


def _mlstm_call(qkv, sm, smt, mo, b_if, norm_g, c0, n0, m0, *, batch, seq, chunk, valid_len):
    nc = seq // chunk
    ng = 2 * M_HEADS

    def rowspec(width):
        return pl.BlockSpec((chunk, width), lambda b, c: (b * nc + c, 0))

    def const(shape):
        return pl.BlockSpec(shape, lambda b, c: (0,) * len(shape))

    state_specs = [pl.BlockSpec((1, M_HEADS, M_DK, M_DV), lambda b, c: (b, 0, 0, 0)),
                   pl.BlockSpec((1, M_HEADS, M_DK), lambda b, c: (b, 0, 0)),
                   pl.BlockSpec((1, 1, M_HEADS), lambda b, c: (b, 0, 0))]
    return pl.pallas_call(
        functools.partial(_mlstm_kernel, chunk=chunk, valid_len=valid_len),
        grid=(batch, nc),
        in_specs=[rowspec(M_QKV_W), rowspec(LANE), pl.BlockSpec((1, ng, chunk), lambda b, c: (b, 0, c)),
                  rowspec(M_W), const((1, ng)), const((ng, 1)), const((1, M_W))] + state_specs,
        out_specs=[rowspec(M_W)] + state_specs,
        out_shape=[jax.ShapeDtypeStruct((batch * seq, M_W), BF16),
                   jax.ShapeDtypeStruct((batch, M_HEADS, M_DK, M_DV), F32),
                   jax.ShapeDtypeStruct((batch, M_HEADS, M_DK), F32),
                   jax.ShapeDtypeStruct((batch, 1, M_HEADS), F32)],
        compiler_params=_cparams(("arbitrary", "arbitrary")),
    )(qkv, sm, smt, mo, b_if.reshape(1, ng), b_if.reshape(ng, 1), norm_g.reshape(1, M_W), c0, n0, m0)


def _mlstm_t_kernel(qt_ref, k_ref, vt_ref, mot_ref, sm_ref, smt_ref, bc_ref, br_ref, g_ref, o_ref, ct_ref, n_ref, m_ref,
                    *, chunk):
    @pl.when(pl.program_id(1) == 0)
    def _():
        ct_ref[...] = jnp.zeros(ct_ref.shape, F32)
        n_ref[...] = jnp.zeros(n_ref.shape, F32)
        m_ref[...] = jnp.zeros(m_ref.shape, F32)

    ng = 2 * M_HEADS
    gate_c = sm_ref[:, GATE_COLS:GATE_COLS + ng] + bc_ref[...]
    gate_r = smt_ref[0] + br_ref[...]
    s_i = lax.broadcasted_iota(I32, (chunk, chunk), 0)
    t_i = lax.broadcasted_iota(I32, (chunk, chunk), 1)
    causal = s_i <= t_i
    tri_c = jnp.where(t_i <= s_i, 1.0, 0.0).astype(BF16)
    tri_r = jnp.where(causal, 1.0, 0.0).astype(BF16)
    b_c = sum(jnp.dot(tri_c, part, preferred_element_type=F32) for part in _split_bf16(_log_sigmoid(gate_c), 3))
    b_r = sum(jnp.dot(part, tri_r, preferred_element_type=F32) for part in _split_bf16(_log_sigmoid(gate_r), 3))
    r_c = gate_c[:, :M_HEADS] - b_c[:, M_HEADS:]
    g_full = jnp.concatenate([g_ref[...]] * (chunk // LANE), axis=1)

    outs = []
    for n in range(M_HEADS):
        qt = qt_ref[n * M_DK:(n + 1) * M_DK, :]
        kb = k_ref[:, n * M_DK:(n + 1) * M_DK]
        vt = vt_ref[n * M_DV:(n + 1) * M_DV, :]
        b_row = b_r[M_HEADS + n:M_HEADS + n + 1, :]
        ig_row = gate_r[n:n + 1, :]
        b_end = b_c[chunk - 1:chunk, M_HEADS + n:M_HEADS + n + 1]
        m_prev = m_ref[0, :, n:n + 1]
        ct_prev = ct_ref[0, n]
        n_prev = n_ref[0, n:n + 1, :]

        a = b_row + m_prev
        d_t = jnp.where(causal, r_c[:, n:n + 1] + b_row, -jnp.inf)
        m_t = jnp.maximum(a, jnp.max(d_t, axis=0, keepdims=True))
        w_t = jnp.dot(kb, qt, preferred_element_type=F32) * jnp.exp(d_t - m_t)
        aw = jnp.exp(a - m_t)
        num = (jnp.dot(vt.astype(BF16), w_t.astype(BF16), preferred_element_type=F32)
               + aw * jnp.dot(ct_prev.astype(BF16), qt, preferred_element_type=F32))
        den = (jnp.sum(w_t, axis=0, keepdims=True)
               + aw * jnp.dot(n_prev.astype(BF16), qt, preferred_element_type=F32))
        hcell = num / jnp.maximum(jnp.abs(den), jnp.exp(-m_t))
        hn = (hcell * lax.rsqrt(jnp.mean(hcell * hcell, axis=0, keepdims=True) + EPS)
              * g_full[n * M_DV:(n + 1) * M_DV, :])
        outs.append(jax.nn.sigmoid(mot_ref[n * M_DV:(n + 1) * M_DV, :]) * hn)

        wl = b_end - b_row + ig_row
        m_new = jnp.maximum(b_end + m_prev, jnp.max(wl, axis=1, keepdims=True))
        decay = jnp.exp(b_end + m_prev - m_new)
        ws = jnp.exp(wl - m_new)
        ct_ref[0, n] = decay * ct_prev + jnp.dot((vt * ws).astype(BF16), kb, preferred_element_type=F32)
        n_ref[0, n:n + 1, :] = decay * n_prev + jnp.dot(ws.astype(BF16), kb, preferred_element_type=F32)
        m_ref[0, :, n:n + 1] = m_new
    o_ref[...] = jnp.concatenate(outs, axis=0).T.astype(o_ref.dtype)


def _mlstm_t_call(qt, k, vt, mot, sm, smt, b_if, norm_g, *, batch, seq, chunk):
    nc = seq // chunk
    ng = 2 * M_HEADS
    nqk = M_HEADS * M_DK
    assert chunk % LANE == 0

    def colspec(n):
        return pl.BlockSpec((n, chunk), lambda b, c: (0, b * nc + c))

    def rowspec(width):
        return pl.BlockSpec((chunk, width), lambda b, c: (b * nc + c, 0))

    def const(shape):
        return pl.BlockSpec(shape, lambda b, c: (0,) * len(shape))

    state_specs = [pl.BlockSpec((1, M_HEADS, M_DV, M_DK), lambda b, c: (b, 0, 0, 0)),
                   pl.BlockSpec((1, M_HEADS, M_DK), lambda b, c: (b, 0, 0)),
                   pl.BlockSpec((1, 1, M_HEADS), lambda b, c: (b, 0, 0))]
    return pl.pallas_call(
        functools.partial(_mlstm_t_kernel, chunk=chunk),
        grid=(batch, nc),
        in_specs=[colspec(nqk), rowspec(nqk), colspec(M_W), colspec(M_W), rowspec(LANE),
                  pl.BlockSpec((1, ng, chunk), lambda b, c: (b, 0, c)),
                  const((1, ng)), const((ng, 1)), const((M_W, LANE))],
        out_specs=[rowspec(M_W)] + state_specs,
        out_shape=[jax.ShapeDtypeStruct((batch * seq, M_W), BF16),
                   jax.ShapeDtypeStruct((batch, M_HEADS, M_DV, M_DK), F32),
                   jax.ShapeDtypeStruct((batch, M_HEADS, M_DK), F32),
                   jax.ShapeDtypeStruct((batch, 1, M_HEADS), F32)],
        compiler_params=_cparams(("arbitrary", "arbitrary")),
    )(qt, k, vt, mot, sm, smt, b_if.reshape(1, ng), b_if.reshape(ng, 1),
      jnp.broadcast_to(norm_g.reshape(M_W, 1), (M_W, LANE)))


def _ffn_kernel(x_ref, a_ref, mh_ref, wo_ref, g1_ref, sc_ref, sh_ref, g2_ref, n2_ref, wu_ref, wc_ref, bc_ref,
                wd_ref, hist_ref, *refs, tm, tiles_per_seq, shift, ff, ch, final):
    if final:
        fg_ref, xo_ref, cs_ref, y_ref, carry_scr, up_scr = refs
    else:
        xo_ref, cs_ref, carry_scr, up_scr = refs
    hist_rows = (CONV_W - 1) * shift
    hoff = up_scr.shape[0] - tm

    @pl.when(pl.program_id(0) % tiles_per_seq == 0)
    def _():
        carry_scr[...] = hist_ref[0]

    y = (jnp.dot(a_ref[...], wo_ref[:A_W, :], preferred_element_type=F32)
         + jnp.dot(mh_ref[...], wo_ref[A_W:, :], preferred_element_type=F32))
    x1 = x_ref[...] + g1_ref[0] * y
    xn = x1 * lax.rsqrt(jnp.mean(x1 * x1, axis=-1, keepdims=True) + EPS) * n2_ref[...]
    xb = (xn * (1.0 + sc_ref[0]) + sh_ref[0]).astype(BF16)

    acc = jnp.zeros(x1.shape, F32)
    for lo, hi in ch:
        w = hi - lo
        u = jnp.dot(xb, wu_ref[:, lo:hi], preferred_element_type=F32)
        gt = jnp.dot(xb, wu_ref[:, ff + lo:ff + hi], preferred_element_type=F32)
        up_scr[hoff - hist_rows:hoff, :w] = carry_scr[:, lo:hi]
        up_scr[hoff:, :w] = u
        tail = u[tm - hist_rows:, :]
        carry_scr[:, lo:hi] = tail
        cs_ref[0, :, lo:hi] = tail
        conv = bc_ref[:, lo:hi]
        for jj in range(CONV_W - 1):
            start = hoff - (CONV_W - 1 - jj) * shift
            conv = conv + wc_ref[jj:jj + 1, lo:hi] * up_scr[start:start + tm, :w]
        conv = conv + wc_ref[CONV_W - 1:CONV_W, lo:hi] * u
        hid = conv * jax.nn.sigmoid(conv) * gt
        acc = acc + jnp.dot(hid.astype(BF16), wd_ref[lo:hi, :], preferred_element_type=F32)

    x2 = x1 + g2_ref[0] * acc
    xo_ref[...] = x2
    if final:
        y_ref[...] = x2 * lax.rsqrt(jnp.mean(x2 * x2, axis=-1, keepdims=True) + EPS) * fg_ref[...]


def _ffn_call(x, a_out, m_out, wo, g1, sc, sh, g2, n2, wu, wc, bc, wd, hist, final_g, *,
              tm, tiles_per_seq, shift, per_row_mod, ch):
    rows, d = x.shape
    ff = wd.shape[0]
    groups, hist_rows, _ = hist.shape
    assert hist_rows == (CONV_W - 1) * shift and tm >= hist_rows
    assert ch[0][0] == 0 and ch[-1][1] == ff and all(a[1] == b[0] for a, b in zip(ch, ch[1:]))
    final = final_g is not None
    hoff = -(-hist_rows // SUBLANE) * SUBLANE
    mod_spec = _mod_specs(per_row_mod, tm, d, tiles_per_seq)

    def rowspec(width):
        return pl.BlockSpec((tm, width), lambda i: (i, 0))

    def const(arr):
        return _resident(arr.shape, lambda i: (0,) * arr.ndim)

    hist_spec = pl.BlockSpec((1, hist_rows, ff), lambda i: (i // tiles_per_seq, 0, 0))
    in_specs = [rowspec(d), rowspec(A_W), rowspec(M_W), const(wo), mod_spec, mod_spec, mod_spec, mod_spec,
                pl.BlockSpec((1, d), lambda i: (0, 0)), const(wu), const(wc), const(bc), const(wd), hist_spec]
    args = [x, a_out, m_out, wo, g1, sc, sh, g2, n2, wu, wc, bc, wd, hist]
    out_specs = [rowspec(d), hist_spec]
    out_shape = [jax.ShapeDtypeStruct((rows, d), F32), jax.ShapeDtypeStruct(hist.shape, F32)]
    if final:
        in_specs.append(pl.BlockSpec((1, d), lambda i: (0, 0)))
        args.append(final_g)
        out_specs.append(rowspec(d))
        out_shape.append(jax.ShapeDtypeStruct((rows, d), F32))
    return pl.pallas_call(
        functools.partial(_ffn_kernel, tm=tm, tiles_per_seq=tiles_per_seq, shift=shift, ff=ff, ch=ch, final=final),
        grid=(rows // tm,),
        in_specs=in_specs,
        out_specs=out_specs,
        out_shape=out_shape,
        scratch_shapes=[pltpu.VMEM((hist_rows, ff), F32),
                        pltpu.VMEM((hoff + tm, max(hi - lo for lo, hi in ch)), F32)],
        compiler_params=_cparams(("arbitrary",)),
    )(*args)


def _prep_w_in(w_in):
    sizes = (A_W, KV_W, KV_W, KV_W, GATE_COLS, M_HEADS * M_DK, M_HEADS * M_DK, M_W, M_W, 2 * M_HEADS)
    points = [int(s) for s in np.cumsum(sizes)[:-1]]
    a_q, a_kc, a_ks, a_kw, a_g, m_q, m_k, m_v, m_o, m_if = jnp.split(w_in, points, axis=-1)
    pad = jnp.zeros(w_in.shape[:-1] + (LANE - GATE_COLS - 2 * M_HEADS,), w_in.dtype)
    return jnp.concatenate([a_q, a_kc, a_ks, a_kw, m_q, m_k, m_v, m_o, a_g, m_if, pad], axis=-1).astype(BF16)


def _rope_tables(pos):
    half = A_HD // 2
    freq = ROPE_THETA ** (-2.0 * jnp.arange(half, dtype=F32) / A_HD)
    ang = pos.astype(F32)[:, None] * freq[None, :]
    cos, sin = jnp.cos(ang), jnp.sin(ang)
    return jnp.concatenate([cos, cos, cos, cos], axis=-1), jnp.concatenate([-sin, sin, -sin, sin], axis=-1)


def _kv_out(kv, lead):
    return kv.reshape(lead + (2, A_KV, A_HD))


def kernel(x_prompt, x_sample, cache_cmp_kv, cache_sel_kv, cache_win_kv, state_mlstm_C, state_mlstm_n,
           state_mlstm_m, state_ffn_conv, page_table, c_prompt, c_sample, norm1_g, norm2_g, w_mod, b_mod,
           w_in, b_if, mlstm_norm_g, w_out, w_up, w_conv, b_conv, w_down, final_g):
    batch, seq, d = x_prompt.shape
    dec_b, dec_t, _ = x_sample.shape
    depth = w_in.shape[0]
    ff = w_down.shape[1]
    n_pages = page_table.shape[1]
    past_len = n_pages * PAGE_SIZE
    wb = cache_win_kv.shape[2]
    rows_s = dec_t * dec_b
    assert A_KV * dec_t == SUBLANE and dec_t >= CONV_W - 1 and dec_b % SUBLANE == 0

    tm_p = min(512, seq)
    tq, tk = 512, min(512, seq)
    chunk_p = min(256, seq)
    ff_split = -(-ff // (2 * MXU_TILE)) * MXU_TILE
    ch = ((0, ff_split), (ff_split, ff))
    pages_per_step = 16
    t_pad = BF16_ROWS

    wi = _prep_w_in(w_in)
    wo, wu, wd = w_out.astype(BF16), w_up.astype(BF16), w_down.astype(BF16)
    fg = final_g.reshape(1, d)

    n_c = batch + dec_b
    c_all = jnp.concatenate([c_prompt, c_sample, jnp.zeros((-n_c % SUBLANE, d), F32)], axis=0)
    mod = _mod_call(c_all, w_mod, b_mod)

    cos_p, sin_p = _rope_tables(jnp.arange(seq))
    cos_s, sin_s = [jnp.repeat(t, dec_b, axis=0) for t in _rope_tables(past_len + jnp.arange(dec_t))]

    cache_cmp = cache_cmp_kv.transpose(0, 1, 3, 4, 5, 2)
    cache_sel = cache_sel_kv.transpose(0, 1, 3, 4, 5, 2)
    cache_win = cache_win_kv.transpose(0, 1, 3, 4, 5, 2)

    def kv_from_t(kv_t):
        return kv_t.reshape(kv_t.shape[:2] + (2, A_KV, A_HD, kv_t.shape[-1])).transpose(0, 1, 5, 2, 3, 4)

    def to_bt(a):
        a = a.reshape(dec_t, dec_b, a.shape[-1]).transpose(1, 0, 2)
        return jnp.pad(a, ((0, 0), (0, t_pad - dec_t), (0, 0)))

    xp = x_prompt.reshape(batch * seq, d)
    xs = x_sample.transpose(1, 0, 2).reshape(rows_s, d)
    outs_p = [[] for _ in range(4)]
    outs_s = [[] for _ in range(7)]
    kv_stacks = [jnp.zeros((depth, batch, KV_W, seq), F32) for _ in range(3)]
    y_p = y_s = None
    for l in range(depth):
        last = l == depth - 1
        mods = [mod[l, :, k * d:(k + 1) * d] for k in range(6)]
        sh1_p, sc1_p, g1_p, sh2_p, sc2_p, g2_p = [m[:batch].reshape(batch, 1, d) for m in mods]
        sh1_s, sc1_s, g1_s, sh2_s, sc2_s, g2_s = [jnp.tile(m[batch:n_c], (dec_t, 1)).reshape(1, rows_s, d)
                                                  for m in mods]
        n1, n2 = norm1_g[l].reshape(1, d), norm2_g[l].reshape(1, d)
        wc, bc = w_conv[l], b_conv[l].reshape(1, ff)

        tiles_p = seq // tm_p
        (qt, *kv_stacks, mqt, mk, mvt, mot, sm, ksa, vst, kwb, vwt, cm) = _inproj_call(
            xp, sc1_p, sh1_p, n1, wi[l], cos_p, sin_p, tm=tm_p, tiles_per_seq=tiles_p, per_row_mod=False,
            attn_layouts=True, v_tiles=(tk, tq), kv_stacks=kv_stacks, layer=l)
        a_out = _attn_prompt_call(qt, cm, ksa, vst, kwb, vwt, sm, batch=batch, seq=seq, tq=tq, tk=tk)
        smt = sm[:, GATE_COLS:GATE_COLS + 2 * M_HEADS].reshape(batch, seq, 2 * M_HEADS).transpose(0, 2, 1)
        m_out, st_ct, st_n, st_m = _mlstm_t_call(mqt, mk, mvt, mot, sm, smt, b_if[l], mlstm_norm_g[l],
                                                 batch=batch, seq=seq, chunk=chunk_p)
        st_c = st_ct.transpose(0, 1, 3, 2)
        res = _ffn_call(xp, a_out, m_out, wo[l], g1_p, sc2_p, sh2_p, g2_p, n2, wu[l], wc, bc, wd[l],
                        jnp.zeros((batch, CONV_W - 1, ff), F32), fg if last else None,
                        tm=tm_p, tiles_per_seq=tiles_p, shift=1, per_row_mod=False, ch=ch)
        xp, conv_p = res[0], res[1]
        if last:
            y_p = res[2]
        for lst, arr in zip(outs_p, (st_c, st_n, st_m.reshape(batch, M_HEADS), conv_p)):
            lst.append(arr)

        (q, kvc, kvs, kvw, mqkv, mo, sm) = _inproj_call(
            xs, sc1_s, sh1_s, n1, wi[l], cos_s, sin_s, tm=rows_s, tiles_per_seq=1, per_row_mod=True,
            attn_layouts=False)
        cm_past = _cmp_means_call(cache_cmp, page_table, l)
        q5 = q.reshape(dec_t, dec_b, A_KV, A_G, LANE).transpose(1, 3, 2, 0, 4)
        q5 = jnp.concatenate([q5[:, :, :1], jnp.roll(q5[:, :, 1:], A_HD, axis=-1)], axis=2)
        gl = sm[:, :GATE_COLS].reshape(dec_t, dec_b, A_KV, A_G, 3).transpose(1, 3, 2, 0, 4)
        gl = jnp.pad(gl.reshape(dec_b, A_HEADS * dec_t, 3), ((0, 0), (0, 0), (0, LANE - 3)))
        o_s = _attn_sample_call(page_table, q5.reshape(dec_b, A_HEADS * dec_t, LANE), cm_past, to_bt(kvs),
                                cache_sel, cache_win, to_bt(kvw), gl, l,
                                pages_per_step=pages_per_step, dec_t=dec_t)
        o6 = o_s.reshape(dec_b, A_G, A_KV, dec_t, A_KV, A_HD)
        a_out = jnp.stack([o6[:, :, h, :, h] for h in range(A_KV)], axis=1)
        a_out = a_out.transpose(3, 0, 1, 2, 4).reshape(rows_s, A_W).astype(BF16)
        sm_bt = to_bt(sm)
        m_out, st_c, st_n, st_m = _mlstm_call(
            to_bt(mqkv).reshape(dec_b * t_pad, M_QKV_W), sm_bt.reshape(dec_b * t_pad, LANE),
            sm_bt[:, :, GATE_COLS:GATE_COLS + 2 * M_HEADS].transpose(0, 2, 1), to_bt(mo).reshape(dec_b * t_pad, M_W),
            b_if[l], mlstm_norm_g[l], state_mlstm_C[l], state_mlstm_n[l],
            state_mlstm_m[l].reshape(dec_b, 1, M_HEADS), batch=dec_b, seq=t_pad, chunk=t_pad, valid_len=dec_t)
        m_out = m_out.reshape(dec_b, t_pad, M_W)[:, :dec_t].transpose(1, 0, 2).reshape(rows_s, M_W)
        hist = state_ffn_conv[l].transpose(1, 0, 2).reshape(1, (CONV_W - 1) * dec_b, ff)
        res = _ffn_call(xs, a_out, m_out, wo[l], g1_s, sc2_s, sh2_s, g2_s, n2, wu[l], wc, bc, wd[l],
                        hist, fg if last else None,
                        tm=rows_s, tiles_per_seq=1, shift=dec_b, per_row_mod=True, ch=ch)
        xs, conv_s = res[0], res[1]
        if last:
            y_s = res[2]

        def s_kv(a):
            return _kv_out(a.reshape(dec_t, dec_b, KV_W).transpose(1, 0, 2), (dec_b, dec_t))

        kvw_new_t = kvw.reshape(dec_t, dec_b, 2, A_KV, A_HD).transpose(1, 2, 3, 4, 0)
        win_t = jnp.concatenate([cache_win[l], kvw_new_t], axis=-1)[..., -wb:]
        for lst, arr in zip(outs_s, (s_kv(kvc), s_kv(kvs), win_t.transpose(0, 4, 1, 2, 3), st_c, st_n,
                                     st_m.reshape(dec_b, M_HEADS),
                                     conv_s.reshape(CONV_W - 1, dec_b, ff).transpose(1, 0, 2))):
            lst.append(arr)

    y_prompt = y_p.reshape(batch, seq, d)
    y_sample = y_s.reshape(dec_t, dec_b, d).transpose(1, 0, 2)
    win_len = min(WINDOW, seq)
    kvc_all, kvs_all, kvw_all = kv_stacks
    return ((y_prompt, y_sample, kv_from_t(kvc_all), kv_from_t(kvs_all), kv_from_t(kvw_all[..., seq - win_len:]))
            + tuple(jnp.stack(a, axis=0) for a in outs_p) + tuple(jnp.stack(a, axis=0) for a in outs_s))
```

```python
import functools

import numpy as np
import jax
import jax.numpy as jnp
from jax import lax
from jax.experimental import pallas as pl
from jax.experimental.pallas import tpu as pltpu

F32 = jnp.float32
BF16 = jnp.bfloat16
I32 = jnp.int32

A_HD = 64
A_HEADS = 8
A_KV = 2
A_G = A_HEADS // A_KV
CMP_BLOCK = 64
TOPK = 16
WINDOW = 512
ROPE_THETA = 10000.0
FORCE = float(A_G + 1)
M_HEADS = 4
M_DK = 64
M_DV = 128
CONV_W = 3
EPS = 1e-6
PAGE_SIZE = 128

LANE = 128
SUBLANE = 8
BF16_ROWS = 16
MXU_TILE = 256
VMEM_LIMIT = 56 * 1024 * 1024

A_W = A_HEADS * A_HD
KV_W = 2 * A_KV * A_HD
M_QKV_W = 2 * M_HEADS * M_DK + M_HEADS * M_DV
M_W = M_HEADS * M_DV
OFF_Q = 0
OFF_KC = OFF_Q + A_W
OFF_KS = OFF_KC + KV_W
OFF_KW = OFF_KS + KV_W
OFF_M = OFF_KW + KV_W
OFF_MO = OFF_M + M_QKV_W
OFF_SM = OFF_MO + M_W
IN_W_PAD = OFF_SM + LANE
GATE_COLS = 3 * A_HEADS

LOG2E = 1.4426950408889634
RANK_STEP = 16
SEL_OFF_BIAS = -(2.0 ** 30)
MASK_NEG = -1e30


def _cparams(sem):
    return pltpu.CompilerParams(dimension_semantics=sem, vmem_limit_bytes=VMEM_LIMIT)


def _resident(shape, index_map):
    return pl.BlockSpec(shape, index_map, pipeline_mode=pl.Buffered(1))


def _mod_kernel(c_ref, w_ref, b_ref, o_ref):
    c = c_ref[...]
    a = (c * jax.nn.sigmoid(c)).astype(BF16)
    o_ref[0] = jnp.dot(a, w_ref[0].astype(BF16), preferred_element_type=F32) + b_ref[0]


def _mod_call(c_all, w_mod, b_mod):
    depth, d, n = w_mod.shape
    rows = c_all.shape[0]
    tn = 1536
    return pl.pallas_call(
        _mod_kernel,
        grid=(depth, n // tn),
        in_specs=[pl.BlockSpec((rows, d), lambda l, j: (0, 0)),
                  pl.BlockSpec((1, d, tn), lambda l, j: (l, 0, j)),
                  pl.BlockSpec((1, 1, tn), lambda l, j: (l, 0, j))],
        out_specs=pl.BlockSpec((1, rows, tn), lambda l, j: (l, 0, j)),
        out_shape=jax.ShapeDtypeStruct((depth, rows, n), F32),
        compiler_params=_cparams(("arbitrary", "arbitrary")),
    )(c_all, w_mod, b_mod.reshape(depth, 1, n))


def _inproj_kernel(x_ref, sc_ref, sh_ref, g_ref, w_ref, cos_ref, sin_ref, *refs, tm, tiles_per_seq, attn_layouts):
    if attn_layouts:
        (q_ref, kvc_ref, kvs_ref, kvw_ref, mqt_ref, mk_ref, mvt_ref, mot_ref, sm_ref, smt_ref,
         ksa_ref, vsb_ref, kwb_ref, vwb_ref, cm_ref) = refs[3:]
    else:
        q_ref, kvc_ref, kvs_ref, kvw_ref, mqkv_ref, mo_ref, sm_ref = refs
    x = x_ref[...]
    xn = x * lax.rsqrt(jnp.mean(x * x, axis=-1, keepdims=True) + EPS) * g_ref[...]
    xn = xn * (1.0 + sc_ref[0]) + sh_ref[0]
    xb = xn.astype(BF16)
    cos = cos_ref[...]
    sin = sin_ref[...]
    lane = lax.broadcasted_iota(I32, (tm, LANE), 1)
    lo_half = (lane % A_HD) < (A_HD // 2)
    first_head = lane < A_HD

    def proj(lo, hi):
        return jnp.dot(xb, w_ref[:, lo:hi], preferred_element_type=F32)

    def rope(v):
        rot = jnp.where(lo_half, pltpu.roll(v, LANE - A_HD // 2, 1), pltpu.roll(v, A_HD // 2, 1))
        return v * cos + rot * sin

    qa = proj(OFF_Q, OFF_Q + A_W)
    q_scale = A_HD ** -0.5 * (LOG2E if attn_layouts else 1.0)
    for s in range(A_W // LANE):
        r = rope(qa[:, s * LANE:(s + 1) * LANE]) * q_scale
        if attn_layouts:
            q_ref[s * LANE:(s + 1) * LANE, :] = r.T.astype(BF16)
        else:
            q_ref[:, (2 * s) * LANE:(2 * s + 1) * LANE] = jnp.where(first_head, r, 0.0).astype(BF16)
            q_ref[:, (2 * s + 1) * LANE:(2 * s + 2) * LANE] = jnp.where(
                first_head, pltpu.roll(r, A_HD, 1), 0.0).astype(BF16)

    def kv(lo, out_ref):
        a = proj(lo, lo + KV_W)
        k, v = rope(a[:, :LANE]), a[:, LANE:]
        if attn_layouts:
            k_t, v_t = k.T, v.T
            out_ref[0, 0, :LANE, :] = k_t
            out_ref[0, 0, LANE:, :] = v_t
        else:
            k_t = v_t = None
            out_ref[:, :LANE] = k
            out_ref[:, LANE:] = v
        return k, v, v_t

    kc, vc, _ = kv(OFF_KC, kvc_ref)
    ks, vs, vs_t = kv(OFF_KS, kvs_ref)
    kw, vw, vw_t = kv(OFF_KW, kvw_ref)

    if attn_layouts:
        nblk = tm // CMP_BLOCK
        cm_ref[:, :LANE] = jnp.sum(kc.reshape(nblk, CMP_BLOCK, LANE), axis=1) * (1.0 / CMP_BLOCK)
        cm_ref[:, LANE:] = jnp.sum(vc.reshape(nblk, CMP_BLOCK, LANE), axis=1) * (1.0 / CMP_BLOCK)
        row = lax.broadcasted_iota(I32, (tm, LANE), 0)
        blk = (pl.program_id(0) % tiles_per_seq) * nblk + row // CMP_BLOCK
        onehot = jnp.where(lane - A_HD == blk, 1.0, 0.0)
        ksa_ref[:, :LANE] = jnp.where(first_head, ks, onehot).astype(BF16)
        ksa_ref[:, LANE:] = jnp.where(first_head, pltpu.roll(ks, A_HD, 1), onehot).astype(BF16)
        kwb_ref[:, :LANE] = kw.astype(BF16)
        kwb_ref[:, LANE:] = pltpu.roll(kw, A_HD, 1).astype(BF16)
        for ref, v_t in ((vsb_ref, vs_t), (vwb_ref, vw_t)):
            width = ref.shape[-1]
            for j in range(tm // width):
                ref[0, j] = v_t[:, j * width:(j + 1) * width].astype(BF16)

    m = proj(OFF_M, OFF_MO)
    mo = proj(OFF_MO, OFF_SM)
    nqk = M_HEADS * M_DK
    mk = m[:, nqk:2 * nqk] * (M_DK ** -0.5)
    if attn_layouts:
        for s in range(nqk // LANE):
            mqt_ref[s * LANE:(s + 1) * LANE, :] = m[:, s * LANE:(s + 1) * LANE].T.astype(BF16)
        mk_ref[...] = mk.astype(BF16)
        for s in range(M_W // LANE):
            mvt_ref[s * LANE:(s + 1) * LANE, :] = m[:, 2 * nqk + s * LANE:2 * nqk + (s + 1) * LANE].T
            mot_ref[s * LANE:(s + 1) * LANE, :] = mo[:, s * LANE:(s + 1) * LANE].T
    else:
        mqkv_ref[:, :nqk] = m[:, :nqk]
        mqkv_ref[:, nqk:2 * nqk] = mk
        mqkv_ref[:, 2 * nqk:] = m[:, 2 * nqk:]
        mo_ref[...] = mo
    sm = proj(OFF_SM, IN_W_PAD)
    sm_ref[...] = sm
    if attn_layouts:
        smt_ref[...] = sm.T[GATE_COLS:GATE_COLS + 2 * M_HEADS, :]


def _mod_specs(per_row_mod, tm, d, tiles_per_seq):
    if per_row_mod:
        return pl.BlockSpec((1, tm, d), lambda i: (0, i, 0))
    return pl.BlockSpec((1, 1, d), lambda i: (i // tiles_per_seq, 0, 0))


def _inproj_call(x, sc, sh, g, w, cos, sin, *, tm, tiles_per_seq, per_row_mod, attn_layouts, v_tiles=None,
                 kv_stacks=(), layer=0):
    rows, d = x.shape
    n_tiles = rows // tm
    n_seq = n_tiles // tiles_per_seq
    seq = tm * tiles_per_seq
    mod_spec = _mod_specs(per_row_mod, tm, d, tiles_per_seq)
    if per_row_mod:
        tab_spec = pl.BlockSpec((tm, LANE), lambda i: (i, 0))
    else:
        tab_spec = pl.BlockSpec((tm, LANE), lambda i: (i % tiles_per_seq, 0))

    def rowspec(width):
        return pl.BlockSpec((tm, width), lambda i: (i, 0))

    if attn_layouts:
        kv_t = jax.ShapeDtypeStruct(kv_stacks[0].shape, F32)
        kv_t_spec = pl.BlockSpec((1, 1, KV_W, tm), lambda i: (layer, i // tiles_per_seq, 0, i % tiles_per_seq))
        out_shapes = [jax.ShapeDtypeStruct((A_W, rows), BF16), kv_t, kv_t, kv_t]
        out_specs = [pl.BlockSpec((A_W, tm), lambda i: (0, i)), kv_t_spec, kv_t_spec, kv_t_spec]
    else:
        out_shapes = [jax.ShapeDtypeStruct((rows, 2 * A_W), BF16)] + [jax.ShapeDtypeStruct((rows, KV_W), F32)] * 3
        out_specs = [rowspec(2 * A_W), rowspec(KV_W), rowspec(KV_W), rowspec(KV_W)]
    if attn_layouts:
        def feat_major(n, dtype):
            return jax.ShapeDtypeStruct((n, rows), dtype), pl.BlockSpec((n, tm), lambda i: (0, i))
        nqk = M_HEADS * M_DK
        pairs = [feat_major(nqk, BF16),
                 (jax.ShapeDtypeStruct((rows, nqk), BF16), rowspec(nqk)),
                 feat_major(M_W, F32), feat_major(M_W, F32)]
    else:
        pairs = [(jax.ShapeDtypeStruct((rows, M_QKV_W), F32), rowspec(M_QKV_W)),
                 (jax.ShapeDtypeStruct((rows, M_W), F32), rowspec(M_W))]
    pairs.append((jax.ShapeDtypeStruct((rows, LANE), F32), rowspec(LANE)))
    if attn_layouts:
        pairs.append(feat_major(2 * M_HEADS, F32))
    out_shapes += [p[0] for p in pairs]
    out_specs += [p[1] for p in pairs]
    if attn_layouts:
        def v_tiled(width):
            per_tile = tm // width
            return (jax.ShapeDtypeStruct((n_seq, seq // width, LANE, width), BF16),
                    pl.BlockSpec((1, per_tile, LANE, width),
                                 lambda i: (i // tiles_per_seq, i % tiles_per_seq, 0, 0)))
        (vs_shape, vs_spec), (vw_shape, vw_spec) = v_tiled(v_tiles[0]), v_tiled(v_tiles[1])
        out_shapes += [
            jax.ShapeDtypeStruct((rows, 2 * LANE), BF16),
            vs_shape,
            jax.ShapeDtypeStruct((rows, 2 * LANE), BF16),
            vw_shape,
            jax.ShapeDtypeStruct((rows // CMP_BLOCK, KV_W), F32),
        ]
        out_specs += [rowspec(2 * LANE), vs_spec, rowspec(2 * LANE), vw_spec,
                      pl.BlockSpec((tm // CMP_BLOCK, KV_W), lambda i: (i, 0))]
    return pl.pallas_call(
        functools.partial(_inproj_kernel, tm=tm, tiles_per_seq=tiles_per_seq, attn_layouts=attn_layouts),
        grid=(n_tiles,),
        in_specs=[rowspec(d), mod_spec, mod_spec, pl.BlockSpec((1, d), lambda i: (0, 0)),
                  _resident(w.shape, lambda i: (0, 0)), tab_spec, tab_spec]
                 + [pl.BlockSpec(memory_space=pl.ANY)] * len(kv_stacks),
        out_specs=out_specs,
        out_shape=out_shapes,
        input_output_aliases={7 + k: 1 + k for k in range(len(kv_stacks))},
        compiler_params=_cparams(("arbitrary",)),
    )(x, sc, sh, g, w, cos, sin, *kv_stacks)


def _nt(a, b):
    return lax.dot_general(a, b, (((1,), (1,)), ((), ())), preferred_element_type=F32)


def _softmax_update(s, v, carry, v_feature_major=False):
    m, l, acc = carry
    m_new = jnp.maximum(m, jnp.max(s, axis=-1, keepdims=True))
    alpha = jnp.exp(m - m_new)
    p = jnp.exp(s - m_new)
    l = alpha * l + jnp.sum(p, axis=-1, keepdims=True)
    pb = p.astype(BF16)
    pv = _nt(pb, v) if v_feature_major else jnp.dot(pb, v, preferred_element_type=F32)
    return m_new, l, alpha * acc + pv


def _softmax_init(rows, width):
    return (jnp.full((rows, 1), MASK_NEG, F32), jnp.zeros((rows, 1), F32), jnp.zeros((rows, width), F32))


def _masked_softmax(s, ok, axis, base2=False):
    s = jnp.where(ok, s, -jnp.inf)
    smax = jnp.max(s, axis=axis, keepdims=True)
    e = (jnp.exp2 if base2 else jnp.exp)(s - jnp.where(jnp.isfinite(smax), smax, 0.0))
    den = jnp.sum(e, axis=axis, keepdims=True)
    return e / jnp.where(den > 0, den, 1.0)


def _stable_rank_sublanes(score, sub8):
    nb, n = score.shape
    groups = [score[SUBLANE * r:SUBLANE * (r + 1), :] for r in range(nb // SUBLANE)]
    cnts = [jnp.zeros((SUBLANE, n), F32) for _ in groups]
    for ib in range(nb):
        s_i = jnp.broadcast_to(score[ib:ib + 1, :], (SUBLANE, n))
        for r, grp in enumerate(groups):
            if SUBLANE * r > ib:
                one = jnp.where(s_i >= grp, 1.0, 0.0)
            elif SUBLANE * (r + 1) <= ib:
                one = jnp.where(s_i > grp, 1.0, 0.0)
            else:
                one = jnp.where(sub8 > ib - SUBLANE * r, jnp.where(s_i >= grp, 1.0, 0.0),
                                jnp.where(s_i > grp, 1.0, 0.0))
            cnts[r] = cnts[r] + one
    return jnp.concatenate(cnts, axis=0)


def _causal_rank(score, sub8, n_allowed):
    nb, n = score.shape

    def ranked(rows):
        def branch(sc):
            if rows <= TOPK:
                return jnp.zeros((nb, n), F32)
            cnt = _stable_rank_sublanes(sc[:rows], sub8)
            return cnt if rows == nb else jnp.concatenate([cnt, jnp.zeros((nb - rows, n), F32)], axis=0)
        return branch

    sizes = list(range(RANK_STEP, nb, RANK_STEP)) + [nb]
    which = jnp.minimum((n_allowed + RANK_STEP - 1) // RANK_STEP, len(sizes)) - 1
    return lax.switch(which, [ranked(r) for r in sizes], score)


def _softmax_update_t(s_t, v_ext, carry):
    m, acc = carry
    m_new = jnp.maximum(m, jnp.max(s_t, axis=0, keepdims=True))
    p = jnp.exp2(s_t - m_new).astype(BF16)
    acc = jnp.exp2(m - m_new) * acc + jnp.dot(v_ext, p, preferred_element_type=F32)
    return m_new, acc


def _softmax_init_t(n, cols):
    return (jnp.full((1, cols), MASK_NEG, F32), jnp.zeros((n, cols), F32))


def _attn_prompt_kernel(qt_ref, cm_ref, ksa_ref, vst_ref, kwb_ref, vwt_ref, sm_ref, o_ref, *, tq, tk, nb):
    i = pl.program_id(1)
    q0 = i * tq
    cols = A_G * tq

    gates_t = jax.nn.sigmoid(sm_ref[...].T[:GATE_COLS, :])
    cm = cm_ref[0]
    cm_pad = jnp.concatenate([cm[:, LANE:], jnp.zeros((LANE - nb, LANE), F32)], axis=0)
    cm_vt = cm_pad.T.astype(BF16)

    blk_t = lax.broadcasted_iota(I32, (nb, tq), 0)
    qpos_t = q0 + lax.broadcasted_iota(I32, (nb, tq), 1)
    cur_t = qpos_t // CMP_BLOCK
    forced_t = (blk_t == 0) | (blk_t == cur_t)
    allowed_t = blk_t <= cur_t
    blk_t4 = lax.broadcasted_iota(I32, (nb, cols), 0)
    qpos_t4 = q0 + (lax.broadcasted_iota(I32, (nb, cols), 1) & (tq - 1))
    cmp_ok_t4 = (blk_t4 + 1) * CMP_BLOCK <= qpos_t4 + 1
    cg = max(tq, cols // 2)
    n_cg = cols // cg
    kk_k = lax.broadcasted_iota(I32, (tk, cg), 0)
    qq_k = lax.broadcasted_iota(I32, (tk, cg), 1) & (tq - 1)
    ones_k = jnp.ones((BF16_ROWS, tk), BF16)
    ones_q = jnp.ones((BF16_ROWS, tq), BF16)
    kk_q = lax.broadcasted_iota(I32, (tq, cg), 0)
    qq_q = lax.broadcasted_iota(I32, (tq, cg), 1) & (tq - 1)
    sub8 = lax.broadcasted_iota(I32, (SUBLANE, tq), 0)
    zeros_hd = jnp.zeros((A_HD, tq), BF16)

    def stacked(q_heads, lower):
        return jnp.concatenate([jnp.concatenate([qg, lower], axis=0) for qg in q_heads], axis=1)

    q0t, q_aug, o_cmp = [], [], []
    for h in range(A_KV):
        q_heads = [qt_ref[(A_G * h + g) * A_HD:(A_G * h + g + 1) * A_HD, :] for g in range(A_G)]
        q0t.append(stacked(q_heads, zeros_hd))
        cm_k = cm[:, :LANE] if h == 0 else pltpu.roll(cm[:, :LANE], A_HD, 1)
        p_t = _masked_softmax(jnp.dot(cm_k.astype(BF16), q0t[h], preferred_element_type=F32), cmp_ok_t4, 0,
                              base2=True)
        p_pad = jnp.concatenate([p_t, jnp.zeros((LANE - nb, cols), F32)], axis=0).astype(BF16)
        o_cmp.append(jnp.dot(cm_vt[h * A_HD:(h + 1) * A_HD], p_pad, preferred_element_type=F32))

        imp = p_t[:, 0:tq]
        for g in range(1, A_G):
            imp = imp + p_t[:, g * tq:(g + 1) * tq]
        score = jnp.where(forced_t, FORCE, jnp.where(allowed_t, imp, -1.0))
        cnt = _causal_rank(score, sub8, (q0 + tq - 1) // CMP_BLOCK + 1)
        bias_t = jnp.where(cnt < TOPK, jnp.where(score >= 0, 0.0, SEL_OFF_BIAS), SEL_OFF_BIAS)
        if nb < A_HD:
            bias_t = jnp.concatenate([bias_t, jnp.zeros((A_HD - nb, tq), F32)], axis=0)
        q_aug.append(stacked(q_heads, bias_t.astype(BF16)))

    def sel_scores(kt, masked):
        k0 = pl.multiple_of(kt * tk, tk)
        scores = []
        for h in range(A_KV):
            for c in range(n_cg):
                s = jnp.dot(ksa_ref[0, pl.ds(k0, tk), h * LANE:(h + 1) * LANE],
                            q_aug[h][:, c * cg:(c + 1) * cg], preferred_element_type=F32)
                if masked:
                    s = jnp.where(k0 + kk_k <= q0 + qq_k, s, MASK_NEG)
                scores.append(s)
        return scores

    def sel_update(kt, scores, carries):
        out = []
        for h in range(A_KV):
            v_ext = jnp.concatenate([vst_ref[0, kt, h * A_HD:(h + 1) * A_HD, :], ones_k], axis=0)
            for c in range(n_cg):
                out.append(_softmax_update_t(scores[h * n_cg + c], v_ext, carries[h * n_cg + c]))
        return tuple(out)

    def sel_pair(kp, carries):
        s_a, s_b = sel_scores(2 * kp, False), sel_scores(2 * kp + 1, False)
        return sel_update(2 * kp + 1, s_b, sel_update(2 * kp, s_a, carries))

    n_full = q0 // tk
    carries = lax.fori_loop(0, n_full // 2, sel_pair,
                            tuple(_softmax_init_t(A_HD + BF16_ROWS, cg) for _ in range(A_KV * n_cg)))
    carries = lax.fori_loop(n_full - n_full % 2, n_full,
                            lambda kt, c: sel_update(kt, sel_scores(kt, False), c), carries)

    s_diag = sel_scores(n_full, True)
    n_wt = WINDOW // tq + 1
    tiles = [jnp.maximum(i - wt, 0) for wt in range(n_wt)]
    win_scores = []
    for h in range(A_KV):
        for c in range(n_cg):
            per_tile = []
            for wt in range(n_wt):
                startc = pl.multiple_of(tiles[wt] * tq, tq)
                s = jnp.dot(kwb_ref[0, pl.ds(startc, tq), h * LANE:(h + 1) * LANE],
                            q0t[h][:, c * cg:(c + 1) * cg], preferred_element_type=F32)
                if wt == 0:
                    s = jnp.where(kk_q <= qq_q, s, MASK_NEG)
                else:
                    if wt == n_wt - 1:
                        s = jnp.where(kk_q > qq_q, s, MASK_NEG)
                    s = s + jnp.where(q0 - wt * tq >= 0, 0.0, MASK_NEG)
                per_tile.append(s)
            win_scores.append(per_tile)

    carries = sel_update(n_full, s_diag, carries)
    o_sel = [jnp.concatenate([acc[:A_HD] / acc[A_HD:A_HD + 1] for _, acc in carries[h * n_cg:(h + 1) * n_cg]], axis=1)
             for h in range(A_KV)]

    o_win = []
    for h in range(A_KV):
        v_ext = [jnp.concatenate([vwt_ref[0, tile, h * A_HD:(h + 1) * A_HD, :], ones_q], axis=0) for tile in tiles]
        outs = []
        for c in range(n_cg):
            per_tile = win_scores[h * n_cg + c]
            m = functools.reduce(jnp.maximum, [jnp.max(s, axis=0, keepdims=True) for s in per_tile])
            acc = functools.reduce(jnp.add, [jnp.dot(v, jnp.exp2(s - m).astype(BF16), preferred_element_type=F32)
                                             for v, s in zip(v_ext, per_tile)])
            outs.append(acc[:A_HD] / acc[A_HD:A_HD + 1])
        o_win.append(jnp.concatenate(outs, axis=1))

    mixes = []
    for h in range(A_KV):
        for g in range(A_G):
            head = A_G * h + g
            sl = (slice(None), slice(g * tq, (g + 1) * tq))
            mixes.append(gates_t[3 * head:3 * head + 1, :] * o_cmp[h][sl] + gates_t[3 * head + 1:3 * head + 2, :] * o_sel[h][sl]
                         + gates_t[3 * head + 2:3 * head + 3, :] * o_win[h][sl])
    o_ref[...] = jnp.concatenate(mixes, axis=0).T.astype(o_ref.dtype)


def _attn_prompt_call(qt, cm, ksa, vst, kwb, vwt, sm, *, batch, seq, tq, tk):
    nb = seq // CMP_BLOCK
    tiles = seq // tq
    assert nb <= A_HD and tq % LANE == 0 and tk % tq == 0 and seq % tk == 0 and WINDOW % tq == 0

    def full(width):
        return pl.BlockSpec((1, seq, width), lambda b, i: (b, 0, 0))

    def full_t(arr):
        return pl.BlockSpec((1,) + arr.shape[1:], lambda b, i: (b, 0, 0, 0))

    return pl.pallas_call(
        functools.partial(_attn_prompt_kernel, tq=tq, tk=tk, nb=nb),
        grid=(batch, tiles),
        in_specs=[pl.BlockSpec((A_W, tq), lambda b, i: (0, b * tiles + i)),
                  pl.BlockSpec((1, nb, KV_W), lambda b, i: (b, 0, 0)),
                  full(2 * LANE), full_t(vst), full(2 * LANE), full_t(vwt),
                  pl.BlockSpec((tq, LANE), lambda b, i: (b * tiles + i, 0))],
        out_specs=pl.BlockSpec((tq, A_W), lambda b, i: (b * tiles + i, 0)),
        out_shape=jax.ShapeDtypeStruct((batch * seq, A_W), BF16),
        compiler_params=_cparams(("arbitrary", "arbitrary")),
    )(qt, cm.reshape(batch, nb, KV_W), ksa.reshape(batch, seq, 2 * LANE), vst, kwb.reshape(batch, seq, 2 * LANE),
      vwt, sm)


def _page_spec(k, pages_per_step, layer):
    def index(b, j, pt):
        return (layer, pt[b, j * pages_per_step + k], 0, 0, 0, 0)
    return pl.BlockSpec((1, 1, 2, A_KV, A_HD, PAGE_SIZE), index)


def _page_kv(ref):
    n_pos = ref.shape[-1]
    return ref[0, 0, 0].reshape(A_KV * A_HD, n_pos), ref[0, 0, 1].reshape(A_KV * A_HD, n_pos)


def _cmp_means_kernel(pt_ref, *refs, n_pages):
    del pt_ref
    o_ref = refs[n_pages]
    per_page = PAGE_SIZE // CMP_BLOCK
    nbp = n_pages * per_page
    pos_blk = lax.broadcasted_iota(I32, (PAGE_SIZE, nbp), 0) // CMP_BLOCK
    blk = lax.broadcasted_iota(I32, (PAGE_SIZE, nbp), 1)
    acc = jnp.zeros((KV_W, nbp), F32)
    for k in range(n_pages):
        page = refs[k][0, 0].reshape(KV_W, PAGE_SIZE)
        place = jnp.where(blk == per_page * k + pos_blk, 1.0, 0.0).astype(BF16)
        for part in _split_bf16(page, 2):
            acc = acc + jnp.dot(part, place, preferred_element_type=F32)
    o_ref[0] = acc * (1.0 / CMP_BLOCK)


def _cmp_means_call(cache, page_table, layer):
    dec_b, n_pages = page_table.shape
    nbp = n_pages * (PAGE_SIZE // CMP_BLOCK)
    grid_spec = pltpu.PrefetchScalarGridSpec(
        num_scalar_prefetch=1,
        grid=(dec_b, 1),
        in_specs=[_page_spec(k, n_pages, layer) for k in range(n_pages)],
        out_specs=pl.BlockSpec((1, KV_W, nbp), lambda b, j, pt: (b, 0, 0)),
    )
    return pl.pallas_call(
        functools.partial(_cmp_means_kernel, n_pages=n_pages),
        grid_spec=grid_spec,
        out_shape=jax.ShapeDtypeStruct((dec_b, KV_W, nbp), F32),
        compiler_params=_cparams(("arbitrary", "arbitrary")),
    )(page_table, *([cache] * n_pages))


def _attn_sample_kernel(pt_ref, q_ref, cm_ref, kn_ref, win_ref, wn_ref, gl_ref, *refs,
                        pages_per_step, dec_t, past_len, nbp):
    del pt_ref
    page_refs = refs[:pages_per_step]
    o_ref = refs[pages_per_step]
    m_scr, l_scr, acc_scr, bias_scr, expand_scr, ocmp_scr = refs[pages_per_step + 1:]
    j = pl.program_id(1)
    rows = q_ref.shape[1]
    grp = A_KV * dec_t
    tk = pages_per_step * PAGE_SIZE
    blocks_per_step = tk // CMP_BLOCK
    qb = q_ref[0]
    t_row = lax.broadcasted_iota(I32, (rows, 1), 0) % dec_t

    @pl.when(j == 0)
    def _():
        cm_t = cm_ref[0]
        blk = lax.broadcasted_iota(I32, (rows, nbp), 1)
        ok = (blk + 1) * CMP_BLOCK <= past_len + t_row + 1
        p = _masked_softmax(jnp.dot(qb, cm_t[:LANE].astype(BF16), preferred_element_type=F32), ok, 1)
        ocmp_scr[...] = _nt(p.astype(BF16), cm_t[LANE:].astype(BF16))
        imp = p[0:grp]
        for g in range(1, A_G):
            imp = imp + p[g * grp:(g + 1) * grp]
        blk8 = lax.broadcasted_iota(I32, (grp, nbp), 1)
        cur = (past_len + lax.broadcasted_iota(I32, (grp, nbp), 0) % dec_t) // CMP_BLOCK
        score = jnp.where(blk8 == 0, FORCE, jnp.where(blk8 == cur, FORCE, jnp.where(blk8 <= cur, imp, -1.0)))
        cnt = jnp.where(FORCE > score, 1.0, 0.0)
        for ib in range(nbp):
            c_i = score[:, ib:ib + 1]
            cnt = cnt + jnp.where(blk8 > ib, jnp.where(c_i >= score, 1.0, 0.0), jnp.where(c_i > score, 1.0, 0.0))
        bias = jnp.where(cnt < TOPK, jnp.where(score >= 0, 0.0, SEL_OFF_BIAS), SEL_OFF_BIAS)
        bias = jnp.concatenate([bias] * A_G, axis=0)
        for jj in range(bias_scr.shape[0]):
            bias_scr[jj] = pltpu.roll(bias, (nbp - jj * blocks_per_step) % nbp, 1).astype(BF16)
        blk_e = lax.broadcasted_iota(I32, (nbp, tk), 0)
        key_blk = lax.broadcasted_iota(I32, (nbp, tk), 1) // CMP_BLOCK
        expand_scr[...] = jnp.where(blk_e == key_blk, 1.0, 0.0).astype(BF16)
        m_scr[...] = jnp.full(m_scr.shape, MASK_NEG, F32)
        l_scr[...] = jnp.zeros(l_scr.shape, F32)
        acc_scr[...] = jnp.zeros(acc_scr.shape, F32)

    pages = [_page_kv(r) for r in page_refs]
    keys_t = jnp.concatenate([k.astype(BF16) for k, _ in pages], axis=1)
    vals_t = jnp.concatenate([v.astype(BF16) for _, v in pages], axis=1)
    s = (jnp.dot(qb, keys_t, preferred_element_type=F32)
         + jnp.dot(bias_scr[j], expand_scr[...], preferred_element_type=F32))
    m, l, acc = _softmax_update(s, vals_t, (m_scr[...], l_scr[...], acc_scr[...]), v_feature_major=True)
    m_scr[...] = m
    l_scr[...] = l
    acc_scr[...] = acc

    @pl.when(j == pl.num_programs(1) - 1)
    def _():
        new_ok = lax.broadcasted_iota(I32, (rows, kn_ref.shape[1]), 1) <= t_row

        def new_rows(carry, ref):
            kv = ref[0]
            s_n = jnp.where(new_ok, _nt(qb, kv[:, :LANE].astype(BF16)), MASK_NEG)
            return _softmax_update(s_n, kv[:, LANE:].astype(BF16), carry)

        _, l_s, acc_s = new_rows((m_scr[...], l_scr[...], acc_scr[...]), kn_ref)
        o_sel = acc_s / l_s

        kw_t, vw_t = _page_kv(win_ref)
        wb = kw_t.shape[1]
        jw = lax.broadcasted_iota(I32, (rows, wb), 1)
        s_w = jnp.dot(qb, kw_t.astype(BF16), preferred_element_type=F32)
        s_w = jnp.where(jw > t_row + (wb - WINDOW), s_w, MASK_NEG)
        carry = _softmax_update(s_w, vw_t.astype(BF16), _softmax_init(rows, LANE), v_feature_major=True)
        _, l_w, acc_w = new_rows(carry, wn_ref)
        o_win = acc_w / l_w

        gate = jax.nn.sigmoid(gl_ref[0])
        o_ref[0] = gate[:, 0:1] * ocmp_scr[...] + gate[:, 1:2] * o_sel + gate[:, 2:3] * o_win


def _attn_sample_call(page_table, q, cm, kv_new, cache_sel, cache_win, win_new, gate_logits, layer, *,
                      pages_per_step, dec_t):
    dec_b, n_pages = page_table.shape
    rows = q.shape[1]
    nbp = cm.shape[2]
    wb = cache_win.shape[-1]
    past_len = n_pages * PAGE_SIZE
    assert nbp == LANE and past_len >= wb and wb >= WINDOW

    def per_b(shape):
        return pl.BlockSpec((1,) + shape, lambda b, j, pt: (b, 0, 0))

    grid_spec = pltpu.PrefetchScalarGridSpec(
        num_scalar_prefetch=1,
        grid=(dec_b, n_pages // pages_per_step),
        in_specs=[per_b((rows, LANE)), per_b((KV_W, nbp)), per_b(kv_new.shape[1:]),
                  pl.BlockSpec((1, 1, 2, A_KV, A_HD, wb), lambda b, j, pt: (layer, b, 0, 0, 0, 0)),
                  per_b(win_new.shape[1:]), per_b((rows, LANE))]
                 + [_page_spec(k, pages_per_step, layer) for k in range(pages_per_step)],
        out_specs=per_b((rows, LANE)),
        scratch_shapes=[pltpu.VMEM((rows, 1), F32), pltpu.VMEM((rows, 1), F32), pltpu.VMEM((rows, LANE), F32),
                        pltpu.VMEM((n_pages // pages_per_step, rows, nbp), BF16),
                        pltpu.VMEM((nbp, pages_per_step * PAGE_SIZE), BF16), pltpu.VMEM((rows, LANE), F32)],
    )
    return pl.pallas_call(
        functools.partial(_attn_sample_kernel, pages_per_step=pages_per_step, dec_t=dec_t,
                          past_len=past_len, nbp=nbp),
        grid_spec=grid_spec,
        out_shape=jax.ShapeDtypeStruct((dec_b, rows, LANE), F32),
        compiler_params=_cparams(("arbitrary", "arbitrary")),
    )(page_table, q, cm, kv_new, cache_win, win_new, gate_logits, *([cache_sel] * pages_per_step))


def _split_bf16(a, terms):
    parts = []
    for _ in range(terms):
        piece = a.astype(BF16)
        parts.append(piece)
        a = a - piece.astype(F32)
    return parts


def _log_sigmoid(x):
    return jnp.minimum(x, 0.0) - jnp.log1p(jnp.exp(-jnp.abs(x)))


def _mlstm_kernel(qkv_ref, sm_ref, smt_ref, mo_ref, bc_ref, br_ref, g_ref, c0_ref, n0_ref, m0_ref,
                  o_ref, c_ref, n_ref, m_ref, *, chunk, valid_len):
    @pl.when(pl.program_id(1) == 0)
    def _():
        c_ref[...] = c0_ref[...]
        n_ref[...] = n0_ref[...]
        m_ref[...] = m0_ref[...]

    ng = 2 * M_HEADS
    gate_c = sm_ref[:, GATE_COLS:GATE_COLS + ng] + bc_ref[...]
    gate_r = smt_ref[0] + br_ref[...]
    lf_c = _log_sigmoid(gate_c)
    lf_r = _log_sigmoid(gate_r)
    ig_c, ig_r = gate_c, gate_r
    if valid_len < chunk:
        tc = lax.broadcasted_iota(I32, (chunk, ng), 0)
        tr = lax.broadcasted_iota(I32, (ng, chunk), 1)
        lf_c = jnp.where(tc < valid_len, lf_c, 0.0)
        lf_r = jnp.where(tr < valid_len, lf_r, 0.0)
        ig_c = jnp.where(tc < valid_len, ig_c, MASK_NEG)
        ig_r = jnp.where(tr < valid_len, ig_r, MASK_NEG)

    t_i = lax.broadcasted_iota(I32, (chunk, chunk), 0)
    s_i = lax.broadcasted_iota(I32, (chunk, chunk), 1)
    causal = s_i <= t_i
    tri = jnp.where(causal, 1.0, 0.0).astype(BF16)
    tri_t = jnp.where(t_i <= s_i, 1.0, 0.0).astype(BF16)
    b_c = sum(jnp.dot(tri, part, preferred_element_type=F32) for part in _split_bf16(lf_c, 3))
    b_r = sum(jnp.dot(part, tri_t, preferred_element_type=F32) for part in _split_bf16(lf_r, 3))

    nqk = M_HEADS * M_DK
    for n in range(M_HEADS):
        q = qkv_ref[:, n * M_DK:(n + 1) * M_DK]
        k = qkv_ref[:, nqk + n * M_DK:nqk + (n + 1) * M_DK]
        v = qkv_ref[:, 2 * nqk + n * M_DV:2 * nqk + (n + 1) * M_DV]
        qb, kb, vb = q.astype(BF16), k.astype(BF16), v.astype(BF16)
        fcol = M_HEADS + n
        bc = b_c[:, fcol:fcol + 1]
        br = b_r[fcol:fcol + 1, :]
        igc = ig_c[:, n:n + 1]
        igr = ig_r[n:n + 1, :]
        b_end = bc[chunk - 1:chunk, :]
        m_prev = m_ref[0, :, n:n + 1]
        c_prev = c_ref[0, n]
        n_prev = n_ref[0, n:n + 1, :]

        a = bc + m_prev
        d = jnp.where(causal, bc + (igr - br), -jnp.inf)
        m_t = jnp.maximum(a, jnp.max(d, axis=1, keepdims=True))
        w = _nt(qb, kb) * jnp.exp(d - m_t)
        aw = jnp.exp(a - m_t)
        num = (jnp.dot(w.astype(BF16), vb, preferred_element_type=F32)
               + aw * jnp.dot(qb, c_prev.astype(BF16), preferred_element_type=F32))
        den = jnp.sum(w, axis=1, keepdims=True) + aw * jnp.sum(q * n_prev, axis=1, keepdims=True)
        hcell = num / jnp.maximum(jnp.abs(den), jnp.exp(-m_t))

        hn = hcell * lax.rsqrt(jnp.mean(hcell * hcell, axis=-1, keepdims=True) + EPS) * g_ref[:, n * M_DV:(n + 1) * M_DV]
        o_ref[:, n * M_DV:(n + 1) * M_DV] = (jax.nn.sigmoid(mo_ref[:, n * M_DV:(n + 1) * M_DV]) * hn).astype(o_ref.dtype)

        wl = b_end - bc + igc
        m_new = jnp.maximum(b_end + m_prev, jnp.max(wl, axis=0, keepdims=True))
        decay = jnp.exp(b_end + m_prev - m_new)
        kws = k * jnp.exp(wl - m_new)
        c_ref[0, n] = decay * c_prev + lax.dot_general(kws.astype(BF16), vb, (((0,), (0,)), ((), ())),
                                                       preferred_element_type=F32)
        n_ref[0, n:n + 1, :] = decay * n_prev + jnp.sum(kws, axis=0, keepdims=True)
        m_ref[0, :, n:n + 1] = m_new


def _mlstm_call(qkv, sm, smt, mo, b_if, norm_g, c0, n0, m0, *, batch, seq, chunk, valid_len):
    nc = seq // chunk
    ng = 2 * M_HEADS

    def rowspec(width):
        return pl.BlockSpec((chunk, width), lambda b, c: (b * nc + c, 0))

    def const(shape):
        return pl.BlockSpec(shape, lambda b, c: (0,) * len(shape))

    state_specs = [pl.BlockSpec((1, M_HEADS, M_DK, M_DV), lambda b, c: (b, 0, 0, 0)),
                   pl.BlockSpec((1, M_HEADS, M_DK), lambda b, c: (b, 0, 0)),
                   pl.BlockSpec((1, 1, M_HEADS), lambda b, c: (b, 0, 0))]
    return pl.pallas_call(
        functools.partial(_mlstm_kernel, chunk=chunk, valid_len=valid_len),
        grid=(batch, nc),
        in_specs=[rowspec(M_QKV_W), rowspec(LANE), pl.BlockSpec((1, ng, chunk), lambda b, c: (b, 0, c)),
                  rowspec(M_W), const((1, ng)), const((ng, 1)), const((1, M_W))] + state_specs,
        out_specs=[rowspec(M_W)] + state_specs,
        out_shape=[jax.ShapeDtypeStruct((batch * seq, M_W), BF16),
                   jax.ShapeDtypeStruct((batch, M_HEADS, M_DK, M_DV), F32),
                   jax.ShapeDtypeStruct((batch, M_HEADS, M_DK), F32),
                   jax.ShapeDtypeStruct((batch, 1, M_HEADS), F32)],
        compiler_params=_cparams(("arbitrary", "arbitrary")),
    )(qkv, sm, smt, mo, b_if.reshape(1, ng), b_if.reshape(ng, 1), norm_g.reshape(1, M_W), c0, n0, m0)


def _mlstm_t_kernel(qt_ref, k_ref, vt_ref, mot_ref, sm_ref, smt_ref, bc_ref, br_ref, g_ref, o_ref, ct_ref, n_ref, m_ref,
                    *, chunk):
    @pl.when(pl.program_id(1) == 0)
    def _():
        ct_ref[...] = jnp.zeros(ct_ref.shape, F32)
        n_ref[...] = jnp.zeros(n_ref.shape, F32)
        m_ref[...] = jnp.zeros(m_ref.shape, F32)

    ng = 2 * M_HEADS
    gate_c = sm_ref[:, GATE_COLS:GATE_COLS + ng] + bc_ref[...]
    gate_r = smt_ref[...] + br_ref[...]
    s_i = lax.broadcasted_iota(I32, (chunk, chunk), 0)
    t_i = lax.broadcasted_iota(I32, (chunk, chunk), 1)
    causal = s_i <= t_i
    tri_c = jnp.where(t_i <= s_i, 1.0, 0.0).astype(BF16)
    tri_r = jnp.where(causal, 1.0, 0.0).astype(BF16)
    b_c = sum(jnp.dot(tri_c, part, preferred_element_type=F32) for part in _split_bf16(_log_sigmoid(gate_c), 3))
    b_r = sum(jnp.dot(part, tri_r, preferred_element_type=F32) for part in _split_bf16(_log_sigmoid(gate_r), 3))
    r_c = gate_c[:, :M_HEADS] - b_c[:, M_HEADS:]
    g_full = jnp.concatenate([g_ref[...]] * (chunk // LANE), axis=1)

    outs = []
    for n in range(M_HEADS):
        qt = qt_ref[n * M_DK:(n + 1) * M_DK, :]
        kb = k_ref[:, n * M_DK:(n + 1) * M_DK]
        vt = vt_ref[n * M_DV:(n + 1) * M_DV, :]
        b_row = b_r[M_HEADS + n:M_HEADS + n + 1, :]
        ig_row = gate_r[n:n + 1, :]
        b_end = b_c[chunk - 1:chunk, M_HEADS + n:M_HEADS + n + 1]
        m_prev = m_ref[0, :, n:n + 1]
        ct_prev = ct_ref[0, n]
        n_prev = n_ref[0, n:n + 1, :]

        a = b_row + m_prev
        d_t = jnp.where(causal, r_c[:, n:n + 1] + b_row, -jnp.inf)
        m_t = jnp.maximum(a, jnp.max(d_t, axis=0, keepdims=True))
        w_t = jnp.dot(kb, qt, preferred_element_type=F32) * jnp.exp(d_t - m_t)
        aw = jnp.exp(a - m_t)
        num = (jnp.dot(vt.astype(BF16), w_t.astype(BF16), preferred_element_type=F32)
               + aw * jnp.dot(ct_prev.astype(BF16), qt, preferred_element_type=F32))
        den = (jnp.sum(w_t, axis=0, keepdims=True)
               + aw * jnp.dot(n_prev.astype(BF16), qt, preferred_element_type=F32))
        hcell = num / jnp.maximum(jnp.abs(den), jnp.exp(-m_t))
        hn = (hcell * lax.rsqrt(jnp.mean(hcell * hcell, axis=0, keepdims=True) + EPS)
              * g_full[n * M_DV:(n + 1) * M_DV, :])
        outs.append(jax.nn.sigmoid(mot_ref[n * M_DV:(n + 1) * M_DV, :]) * hn)

        wl = b_end - b_row + ig_row
        m_new = jnp.maximum(b_end + m_prev, jnp.max(wl, axis=1, keepdims=True))
        decay = jnp.exp(b_end + m_prev - m_new)
        ws = jnp.exp(wl - m_new)
        ct_ref[0, n] = decay * ct_prev + jnp.dot((vt * ws).astype(BF16), kb, preferred_element_type=F32)
        n_ref[0, n:n + 1, :] = decay * n_prev + jnp.dot(ws.astype(BF16), kb, preferred_element_type=F32)
        m_ref[0, :, n:n + 1] = m_new
    o_ref[...] = jnp.concatenate(outs, axis=0).T.astype(o_ref.dtype)


def _mlstm_t_call(qt, k, vt, mot, sm, smt, b_if, norm_g, *, batch, seq, chunk):
    nc = seq // chunk
    ng = 2 * M_HEADS
    nqk = M_HEADS * M_DK
    assert chunk % LANE == 0

    def colspec(n):
        return pl.BlockSpec((n, chunk), lambda b, c: (0, b * nc + c))

    def rowspec(width):
        return pl.BlockSpec((chunk, width), lambda b, c: (b * nc + c, 0))

    def const(shape):
        return pl.BlockSpec(shape, lambda b, c: (0,) * len(shape))

    state_specs = [pl.BlockSpec((1, M_HEADS, M_DV, M_DK), lambda b, c: (b, 0, 0, 0)),
                   pl.BlockSpec((1, M_HEADS, M_DK), lambda b, c: (b, 0, 0)),
                   pl.BlockSpec((1, 1, M_HEADS), lambda b, c: (b, 0, 0))]
    return pl.pallas_call(
        functools.partial(_mlstm_t_kernel, chunk=chunk),
        grid=(batch, nc),
        in_specs=[colspec(nqk), rowspec(nqk), colspec(M_W), colspec(M_W), rowspec(LANE), colspec(ng),
                  const((1, ng)), const((ng, 1)), const((M_W, LANE))],
        out_specs=[rowspec(M_W)] + state_specs,
        out_shape=[jax.ShapeDtypeStruct((batch * seq, M_W), BF16),
                   jax.ShapeDtypeStruct((batch, M_HEADS, M_DV, M_DK), F32),
                   jax.ShapeDtypeStruct((batch, M_HEADS, M_DK), F32),
                   jax.ShapeDtypeStruct((batch, 1, M_HEADS), F32)],
        compiler_params=_cparams(("arbitrary", "arbitrary")),
    )(qt, k, vt, mot, sm, smt, b_if.reshape(1, ng), b_if.reshape(ng, 1),
      jnp.broadcast_to(norm_g.reshape(M_W, 1), (M_W, LANE)))


def _ffn_kernel(x_ref, a_ref, mh_ref, wo_ref, g1_ref, sc_ref, sh_ref, g2_ref, n2_ref, wu_ref, wc_ref, bc_ref,
                wd_ref, hist_ref, *refs, tm, tiles_per_seq, shift, ff, ch, final):
    if final:
        fg_ref, xo_ref, cs_ref, y_ref, carry_scr, up_scr = refs
    else:
        xo_ref, cs_ref, carry_scr, up_scr = refs
    hist_rows = (CONV_W - 1) * shift
    hoff = up_scr.shape[0] - tm

    @pl.when(pl.program_id(0) % tiles_per_seq == 0)
    def _():
        carry_scr[...] = hist_ref[0]

    y = (jnp.dot(a_ref[...], wo_ref[:A_W, :], preferred_element_type=F32)
         + jnp.dot(mh_ref[...], wo_ref[A_W:, :], preferred_element_type=F32))
    x1 = x_ref[...] + g1_ref[0] * y
    xn = x1 * lax.rsqrt(jnp.mean(x1 * x1, axis=-1, keepdims=True) + EPS) * n2_ref[...]
    xb = (xn * (1.0 + sc_ref[0]) + sh_ref[0]).astype(BF16)

    acc = jnp.zeros(x1.shape, F32)
    for lo, hi in ch:
        w = hi - lo
        u = jnp.dot(xb, wu_ref[:, lo:hi], preferred_element_type=F32)
        gt = jnp.dot(xb, wu_ref[:, ff + lo:ff + hi], preferred_element_type=F32)
        up_scr[hoff - hist_rows:hoff, :w] = carry_scr[:, lo:hi]
        up_scr[hoff:, :w] = u
        tail = u[tm - hist_rows:, :]
        carry_scr[:, lo:hi] = tail
        cs_ref[0, :, lo:hi] = tail
        conv = bc_ref[:, lo:hi]
        for jj in range(CONV_W - 1):
            start = hoff - (CONV_W - 1 - jj) * shift
            conv = conv + wc_ref[jj:jj + 1, lo:hi] * up_scr[start:start + tm, :w]
        conv = conv + wc_ref[CONV_W - 1:CONV_W, lo:hi] * u
        hid = conv * jax.nn.sigmoid(conv) * gt
        acc = acc + jnp.dot(hid.astype(BF16), wd_ref[lo:hi, :], preferred_element_type=F32)

    x2 = x1 + g2_ref[0] * acc
    xo_ref[...] = x2
    if final:
        y_ref[...] = x2 * lax.rsqrt(jnp.mean(x2 * x2, axis=-1, keepdims=True) + EPS) * fg_ref[...]


def _ffn_call(x, a_out, m_out, wo, g1, sc, sh, g2, n2, wu, wc, bc, wd, hist, final_g, *,
              tm, tiles_per_seq, shift, per_row_mod, ch):
    rows, d = x.shape
    ff = wd.shape[0]
    groups, hist_rows, _ = hist.shape
    assert hist_rows == (CONV_W - 1) * shift and tm >= hist_rows
    assert ch[0][0] == 0 and ch[-1][1] == ff and all(a[1] == b[0] for a, b in zip(ch, ch[1:]))
    final = final_g is not None
    hoff = -(-hist_rows // SUBLANE) * SUBLANE
    mod_spec = _mod_specs(per_row_mod, tm, d, tiles_per_seq)

    def rowspec(width):
        return pl.BlockSpec((tm, width), lambda i: (i, 0))

    def const(arr):
        return _resident(arr.shape, lambda i: (0,) * arr.ndim)

    hist_spec = pl.BlockSpec((1, hist_rows, ff), lambda i: (i // tiles_per_seq, 0, 0))
    in_specs = [rowspec(d), rowspec(A_W), rowspec(M_W), const(wo), mod_spec, mod_spec, mod_spec, mod_spec,
                pl.BlockSpec((1, d), lambda i: (0, 0)), const(wu), const(wc), const(bc), const(wd), hist_spec]
    args = [x, a_out, m_out, wo, g1, sc, sh, g2, n2, wu, wc, bc, wd, hist]
    out_specs = [rowspec(d), hist_spec]
    out_shape = [jax.ShapeDtypeStruct((rows, d), F32), jax.ShapeDtypeStruct(hist.shape, F32)]
    if final:
        in_specs.append(pl.BlockSpec((1, d), lambda i: (0, 0)))
        args.append(final_g)
        out_specs.append(rowspec(d))
        out_shape.append(jax.ShapeDtypeStruct((rows, d), F32))
    return pl.pallas_call(
        functools.partial(_ffn_kernel, tm=tm, tiles_per_seq=tiles_per_seq, shift=shift, ff=ff, ch=ch, final=final),
        grid=(rows // tm,),
        in_specs=in_specs,
        out_specs=out_specs,
        out_shape=out_shape,
        scratch_shapes=[pltpu.VMEM((hist_rows, ff), F32),
                        pltpu.VMEM((hoff + tm, max(hi - lo for lo, hi in ch)), F32)],
        compiler_params=_cparams(("arbitrary",)),
    )(*args)


def _prep_w_in(w_in):
    sizes = (A_W, KV_W, KV_W, KV_W, GATE_COLS, M_HEADS * M_DK, M_HEADS * M_DK, M_W, M_W, 2 * M_HEADS)
    points = [int(s) for s in np.cumsum(sizes)[:-1]]
    a_q, a_kc, a_ks, a_kw, a_g, m_q, m_k, m_v, m_o, m_if = jnp.split(w_in, points, axis=-1)
    pad = jnp.zeros(w_in.shape[:-1] + (LANE - GATE_COLS - 2 * M_HEADS,), w_in.dtype)
    return jnp.concatenate([a_q, a_kc, a_ks, a_kw, m_q, m_k, m_v, m_o, a_g, m_if, pad], axis=-1).astype(BF16)


def _rope_tables(pos):
    half = A_HD // 2
    freq = ROPE_THETA ** (-2.0 * jnp.arange(half, dtype=F32) / A_HD)
    ang = pos.astype(F32)[:, None] * freq[None, :]
    cos, sin = jnp.cos(ang), jnp.sin(ang)
    return jnp.concatenate([cos, cos, cos, cos], axis=-1), jnp.concatenate([-sin, sin, -sin, sin], axis=-1)


def _kv_out(kv, lead):
    return kv.reshape(lead + (2, A_KV, A_HD))


def kernel(x_prompt, x_sample, cache_cmp_kv, cache_sel_kv, cache_win_kv, state_mlstm_C, state_mlstm_n,
           state_mlstm_m, state_ffn_conv, page_table, c_prompt, c_sample, norm1_g, norm2_g, w_mod, b_mod,
           w_in, b_if, mlstm_norm_g, w_out, w_up, w_conv, b_conv, w_down, final_g):
    batch, seq, d = x_prompt.shape
    dec_b, dec_t, _ = x_sample.shape
    depth = w_in.shape[0]
    ff = w_down.shape[1]
    n_pages = page_table.shape[1]
    past_len = n_pages * PAGE_SIZE
    wb = cache_win_kv.shape[2]
    rows_s = dec_t * dec_b
    assert A_KV * dec_t == SUBLANE and dec_t >= CONV_W - 1 and dec_b % SUBLANE == 0

    tm_p = min(512, seq)
    tq, tk = 512, min(512, seq)
    chunk_p = min(256, seq)
    ff_split = -(-ff // (2 * MXU_TILE)) * MXU_TILE
    ch = ((0, ff_split), (ff_split, ff))
    pages_per_step = n_pages
    t_pad = BF16_ROWS

    wi = _prep_w_in(w_in)
    wo, wu, wd = w_out.astype(BF16), w_up.astype(BF16), w_down.astype(BF16)
    fg = final_g.reshape(1, d)

    n_c = batch + dec_b
    c_all = jnp.concatenate([c_prompt, c_sample, jnp.zeros((-n_c % SUBLANE, d), F32)], axis=0)
    mod = _mod_call(c_all, w_mod, b_mod)

    cos_p, sin_p = _rope_tables(jnp.arange(seq))
    cos_s, sin_s = [jnp.repeat(t, dec_b, axis=0) for t in _rope_tables(past_len + jnp.arange(dec_t))]

    cache_cmp = cache_cmp_kv.transpose(0, 1, 3, 4, 5, 2)
    cache_sel = cache_sel_kv.transpose(0, 1, 3, 4, 5, 2)
    cache_win = cache_win_kv.transpose(0, 1, 3, 4, 5, 2)

    def kv_from_t(kv_t):
        return kv_t.reshape(kv_t.shape[:2] + (2, A_KV, A_HD, kv_t.shape[-1])).transpose(0, 1, 5, 2, 3, 4)

    def to_bt(a):
        a = a.reshape(dec_t, dec_b, a.shape[-1]).transpose(1, 0, 2)
        return jnp.pad(a, ((0, 0), (0, t_pad - dec_t), (0, 0)))

    xp = x_prompt.reshape(batch * seq, d)
    xs = x_sample.transpose(1, 0, 2).reshape(rows_s, d)
    outs_p = [[] for _ in range(4)]
    outs_s = [[] for _ in range(7)]
    kv_stacks = [jnp.zeros((depth, batch, KV_W, seq), F32) for _ in range(3)]
    y_p = y_s = None
    for l in range(depth):
        last = l == depth - 1
        mods = [mod[l, :, k * d:(k + 1) * d] for k in range(6)]
        sh1_p, sc1_p, g1_p, sh2_p, sc2_p, g2_p = [m[:batch].reshape(batch, 1, d) for m in mods]
        sh1_s, sc1_s, g1_s, sh2_s, sc2_s, g2_s = [jnp.tile(m[batch:n_c], (dec_t, 1)).reshape(1, rows_s, d)
                                                  for m in mods]
        n1, n2 = norm1_g[l].reshape(1, d), norm2_g[l].reshape(1, d)
        wc, bc = w_conv[l], b_conv[l].reshape(1, ff)

        tiles_p = seq // tm_p
        (qt, *kv_stacks, mqt, mk, mvt, mot, sm, smt, ksa, vst, kwb, vwt, cm) = _inproj_call(
            xp, sc1_p, sh1_p, n1, wi[l], cos_p, sin_p, tm=tm_p, tiles_per_seq=tiles_p, per_row_mod=False,
            attn_layouts=True, v_tiles=(tk, tq), kv_stacks=kv_stacks, layer=l)
        a_out = _attn_prompt_call(qt, cm, ksa, vst, kwb, vwt, sm, batch=batch, seq=seq, tq=tq, tk=tk)
        m_out, st_ct, st_n, st_m = _mlstm_t_call(mqt, mk, mvt, mot, sm, smt, b_if[l], mlstm_norm_g[l],
                                                 batch=batch, seq=seq, chunk=chunk_p)
        st_c = st_ct.transpose(0, 1, 3, 2)
        res = _ffn_call(xp, a_out, m_out, wo[l], g1_p, sc2_p, sh2_p, g2_p, n2, wu[l], wc, bc, wd[l],
                        jnp.zeros((batch, CONV_W - 1, ff), F32), fg if last else None,
                        tm=tm_p, tiles_per_seq=tiles_p, shift=1, per_row_mod=False, ch=ch)
        xp, conv_p = res[0], res[1]
        if last:
            y_p = res[2]
        for lst, arr in zip(outs_p, (st_c, st_n, st_m.reshape(batch, M_HEADS), conv_p)):
            lst.append(arr)

        (q, kvc, kvs, kvw, mqkv, mo, sm) = _inproj_call(
            xs, sc1_s, sh1_s, n1, wi[l], cos_s, sin_s, tm=rows_s, tiles_per_seq=1, per_row_mod=True,
            attn_layouts=False)
        cm_past = _cmp_means_call(cache_cmp, page_table, l)
        q5 = q.reshape(dec_t, dec_b, A_KV, A_G, LANE).transpose(1, 3, 2, 0, 4)
        q5 = jnp.concatenate([q5[:, :, :1], jnp.roll(q5[:, :, 1:], A_HD, axis=-1)], axis=2)
        gl = sm[:, :GATE_COLS].reshape(dec_t, dec_b, A_KV, A_G, 3).transpose(1, 3, 2, 0, 4)
        gl = jnp.pad(gl.reshape(dec_b, A_HEADS * dec_t, 3), ((0, 0), (0, 0), (0, LANE - 3)))
        o_s = _attn_sample_call(page_table, q5.reshape(dec_b, A_HEADS * dec_t, LANE), cm_past, to_bt(kvs),
                                cache_sel, cache_win, to_bt(kvw), gl, l,
                                pages_per_step=pages_per_step, dec_t=dec_t)
        o6 = o_s.reshape(dec_b, A_G, A_KV, dec_t, A_KV, A_HD)
        a_out = jnp.stack([o6[:, :, h, :, h] for h in range(A_KV)], axis=1)
        a_out = a_out.transpose(3, 0, 1, 2, 4).reshape(rows_s, A_W).astype(BF16)
        sm_bt = to_bt(sm)
        m_out, st_c, st_n, st_m = _mlstm_call(
            to_bt(mqkv).reshape(dec_b * t_pad, M_QKV_W), sm_bt.reshape(dec_b * t_pad, LANE),
            sm_bt[:, :, GATE_COLS:GATE_COLS + 2 * M_HEADS].transpose(0, 2, 1), to_bt(mo).reshape(dec_b * t_pad, M_W),
            b_if[l], mlstm_norm_g[l], state_mlstm_C[l], state_mlstm_n[l],
            state_mlstm_m[l].reshape(dec_b, 1, M_HEADS), batch=dec_b, seq=t_pad, chunk=t_pad, valid_len=dec_t)
        m_out = m_out.reshape(dec_b, t_pad, M_W)[:, :dec_t].transpose(1, 0, 2).reshape(rows_s, M_W)
        hist = state_ffn_conv[l].transpose(1, 0, 2).reshape(1, (CONV_W - 1) * dec_b, ff)
        res = _ffn_call(xs, a_out, m_out, wo[l], g1_s, sc2_s, sh2_s, g2_s, n2, wu[l], wc, bc, wd[l],
                        hist, fg if last else None,
                        tm=rows_s, tiles_per_seq=1, shift=dec_b, per_row_mod=True, ch=ch)
        xs, conv_s = res[0], res[1]
        if last:
            y_s = res[2]

        def s_kv(a):
            return _kv_out(a.reshape(dec_t, dec_b, KV_W).transpose(1, 0, 2), (dec_b, dec_t))

        kvw_new_t = kvw.reshape(dec_t, dec_b, 2, A_KV, A_HD).transpose(1, 2, 3, 4, 0)
        win_t = jnp.concatenate([cache_win[l], kvw_new_t], axis=-1)[..., -wb:]
        for lst, arr in zip(outs_s, (s_kv(kvc), s_kv(kvs), win_t.transpose(0, 4, 1, 2, 3), st_c, st_n,
                                     st_m.reshape(dec_b, M_HEADS),
                                     conv_s.reshape(CONV_W - 1, dec_b, ff).transpose(1, 0, 2))):
            lst.append(arr)

    y_prompt = y_p.reshape(batch, seq, d)
    y_sample = y_s.reshape(dec_t, dec_b, d).transpose(1, 0, 2)
    win_len = min(WINDOW, seq)
    kvc_all, kvs_all, kvw_all = kv_stacks
    return ((y_prompt, y_sample, kv_from_t(kvc_all), kv_from_t(kvs_all), kv_from_t(kvw_all[..., seq - win_len:]))
            + tuple(jnp.stack(a, axis=0) for a in outs_p) + tuple(jnp.stack(a, axis=0) for a in outs_s))
```

```python
import functools

import jax
import jax.numpy as jnp
from jax import lax
from jax.experimental import pallas as pl
from jax.experimental.pallas import tpu as pltpu

F32 = jnp.float32
BF16 = jnp.bfloat16
I32 = jnp.int32

A_HD = 64
A_HEADS = 8
A_KV = 2
A_G = A_HEADS // A_KV
CMP_BLOCK = 64
TOPK = 16
WINDOW = 512
ROPE_THETA = 10000.0
FORCE = float(A_G + 1)
M_HEADS = 4
M_DK = 64
M_DV = 128
CONV_W = 3
EPS = 1e-6
PAGE_SIZE = 128

LANE = 128
SUBLANE = 8
BF16_ROWS = 16
MXU_TILE = 256
VMEM_LIMIT = 56 * 1024 * 1024

A_W = A_HEADS * A_HD
KV_W = 2 * A_KV * A_HD
M_QKV_W = 2 * M_HEADS * M_DK + M_HEADS * M_DV
M_W = M_HEADS * M_DV
OFF_Q = 0
OFF_KC = OFF_Q + A_W
OFF_KS = OFF_KC + KV_W
OFF_KW = OFF_KS + KV_W
OFF_M = OFF_KW + KV_W
OFF_MO = OFF_M + M_QKV_W
OFF_SM = OFF_MO + M_W
IN_W_PAD = OFF_SM + LANE
GATE_COLS = 3 * A_HEADS

LOG2E = 1.4426950408889634
RANK_STEP = 16
SEL_OFF_BIAS = -(2.0 ** 30)
MASK_NEG = -1e30


def _cparams(sem):
    return pltpu.CompilerParams(dimension_semantics=sem, vmem_limit_bytes=VMEM_LIMIT)


def _resident(shape, index_map):
    return pl.BlockSpec(shape, index_map, pipeline_mode=pl.Buffered(1))


def _mod_kernel(c_ref, w_ref, b_ref, o_ref):
    c = c_ref[...]
    a = (c * jax.nn.sigmoid(c)).astype(BF16)
    o_ref[0] = jnp.dot(a, w_ref[0].astype(BF16), preferred_element_type=F32) + b_ref[0]


def _mod_call(c_all, w_mod, b_mod):
    depth, d, n = w_mod.shape
    rows = c_all.shape[0]
    tn = 1536
    return pl.pallas_call(
        _mod_kernel,
        grid=(depth, n // tn),
        in_specs=[pl.BlockSpec((rows, d), lambda l, j: (0, 0)),
                  pl.BlockSpec((1, d, tn), lambda l, j: (l, 0, j)),
                  pl.BlockSpec((1, 1, tn), lambda l, j: (l, 0, j))],
        out_specs=pl.BlockSpec((1, rows, tn), lambda l, j: (l, 0, j)),
        out_shape=jax.ShapeDtypeStruct((depth, rows, n), F32),
        compiler_params=_cparams(("arbitrary", "arbitrary")),
    )(c_all, w_mod, b_mod.reshape(depth, 1, n))


def _inproj_kernel(x_ref, sc_ref, sh_ref, g_ref, w_ref, cos_ref, sin_ref, *refs, tm, tiles_per_seq, attn_layouts):
    if attn_layouts:
        (q_ref, kvc_ref, kvs_ref, kvw_ref, mqt_ref, mk_ref, mvt_ref, mot_ref, sm_ref, smt_ref,
         ksa_ref, vsb_ref, kwb_ref, vwb_ref, cm_ref) = refs[3:]
    else:
        q_ref, kvc_ref, kvs_ref, kvw_ref, mqkv_ref, mo_ref, sm_ref = refs
    x = x_ref[...]
    xn = x * lax.rsqrt(jnp.mean(x * x, axis=-1, keepdims=True) + EPS) * g_ref[...]
    xn = xn * (1.0 + sc_ref[0]) + sh_ref[0]
    xb = xn.astype(BF16)
    cos = cos_ref[...]
    sin = sin_ref[...]
    lane = lax.broadcasted_iota(I32, (tm, LANE), 1)
    lo_half = (lane % A_HD) < (A_HD // 2)
    first_head = lane < A_HD

    def proj(lo, hi):
        return jnp.dot(xb, w_ref[:, lo:hi], preferred_element_type=F32)

    def rope(v):
        rot = jnp.where(lo_half, pltpu.roll(v, LANE - A_HD // 2, 1), pltpu.roll(v, A_HD // 2, 1))
        return v * cos + rot * sin

    qa = proj(OFF_Q, OFF_Q + A_W)
    q_scale = A_HD ** -0.5 * (LOG2E if attn_layouts else 1.0)
    for s in range(A_W // LANE):
        r = rope(qa[:, s * LANE:(s + 1) * LANE]) * q_scale
        if attn_layouts:
            q_ref[s * LANE:(s + 1) * LANE, :] = r.T.astype(BF16)
        else:
            q_ref[:, (2 * s) * LANE:(2 * s + 1) * LANE] = jnp.where(first_head, r, 0.0).astype(BF16)
            q_ref[:, (2 * s + 1) * LANE:(2 * s + 2) * LANE] = jnp.where(
                first_head, pltpu.roll(r, A_HD, 1), 0.0).astype(BF16)

    def kv(lo, out_ref):
        a = proj(lo, lo + KV_W)
        k, v = rope(a[:, :LANE]), a[:, LANE:]
        if attn_layouts:
            k_t, v_t = k.T, v.T
            out_ref[0, 0, :LANE, :] = k_t
            out_ref[0, 0, LANE:, :] = v_t
        else:
            k_t = v_t = None
            out_ref[:, :LANE] = k
            out_ref[:, LANE:] = v
        return k, v, v_t

    kc, vc, _ = kv(OFF_KC, kvc_ref)
    ks, vs, vs_t = kv(OFF_KS, kvs_ref)
    kw, vw, vw_t = kv(OFF_KW, kvw_ref)

    if attn_layouts:
        nblk = tm // CMP_BLOCK
        cm_ref[:, :LANE] = jnp.sum(kc.reshape(nblk, CMP_BLOCK, LANE), axis=1) * (1.0 / CMP_BLOCK)
        cm_ref[:, LANE:] = jnp.sum(vc.reshape(nblk, CMP_BLOCK, LANE), axis=1) * (1.0 / CMP_BLOCK)
        row = lax.broadcasted_iota(I32, (tm, LANE), 0)
        blk = (pl.program_id(0) % tiles_per_seq) * nblk + row // CMP_BLOCK
        onehot = jnp.where(lane - A_HD == blk, 1.0, 0.0)
        ksa_ref[:, :LANE] = jnp.where(first_head, ks, onehot).astype(BF16)
        ksa_ref[:, LANE:] = jnp.where(first_head, pltpu.roll(ks, A_HD, 1), onehot).astype(BF16)
        kwb_ref[:, :LANE] = kw.astype(BF16)
        kwb_ref[:, LANE:] = pltpu.roll(kw, A_HD, 1).astype(BF16)
        for ref, v_t in ((vsb_ref, vs_t), (vwb_ref, vw_t)):
            width = ref.shape[-1]
            for j in range(tm // width):
                ref[0, j] = v_t[:, j * width:(j + 1) * width].astype(BF16)

    m = proj(OFF_M, OFF_MO)
    mo = proj(OFF_MO, OFF_SM)
    nqk = M_HEADS * M_DK
    mk = m[:, nqk:2 * nqk] * (M_DK ** -0.5)
    if attn_layouts:
        for s in range(nqk // LANE):
            mqt_ref[s * LANE:(s + 1) * LANE, :] = m[:, s * LANE:(s + 1) * LANE].T.astype(BF16)
        mk_ref[...] = mk.astype(BF16)
        for s in range(M_W // LANE):
            mvt_ref[s * LANE:(s + 1) * LANE, :] = m[:, 2 * nqk + s * LANE:2 * nqk + (s + 1) * LANE].T
            mot_ref[s * LANE:(s + 1) * LANE, :] = mo[:, s * LANE:(s + 1) * LANE].T
    else:
        mqkv_ref[:, :nqk] = m[:, :nqk]
        mqkv_ref[:, nqk:2 * nqk] = mk
        mqkv_ref[:, 2 * nqk:] = m[:, 2 * nqk:]
        mo_ref[...] = mo
    sm = proj(OFF_SM, IN_W_PAD)
    sm_ref[...] = sm
    if attn_layouts:
        smt_ref[...] = sm.T[GATE_COLS:GATE_COLS + 2 * M_HEADS, :]


def _mod_specs(per_row_mod, tm, d, tiles_per_seq):
    if per_row_mod:
        return pl.BlockSpec((1, tm, d), lambda i: (0, i, 0))
    return pl.BlockSpec((1, 1, d), lambda i: (i // tiles_per_seq, 0, 0))


def _inproj_call(x, sc, sh, g, w, cos, sin, *, tm, tiles_per_seq, per_row_mod, attn_layouts, v_tiles=None,
                 kv_stacks=(), layer=0):
    rows, d = x.shape
    n_tiles = rows // tm
    n_seq = n_tiles // tiles_per_seq
    seq = tm * tiles_per_seq
    mod_spec = _mod_specs(per_row_mod, tm, d, tiles_per_seq)
    if per_row_mod:
        tab_spec = pl.BlockSpec((tm, LANE), lambda i: (i, 0))
    else:
        tab_spec = pl.BlockSpec((tm, LANE), lambda i: (i % tiles_per_seq, 0))

    def rowspec(width):
        return pl.BlockSpec((tm, width), lambda i: (i, 0))

    if attn_layouts:
        kv_t = jax.ShapeDtypeStruct(kv_stacks[0].shape, F32)
        kv_t_spec = pl.BlockSpec((1, 1, KV_W, tm), lambda i: (layer, i // tiles_per_seq, 0, i % tiles_per_seq))
        out_shapes = [jax.ShapeDtypeStruct((A_W, rows), BF16), kv_t, kv_t, kv_t]
        out_specs = [pl.BlockSpec((A_W, tm), lambda i: (0, i)), kv_t_spec, kv_t_spec, kv_t_spec]
    else:
        out_shapes = [jax.ShapeDtypeStruct((rows, 2 * A_W), BF16)] + [jax.ShapeDtypeStruct((rows, KV_W), F32)] * 3
        out_specs = [rowspec(2 * A_W), rowspec(KV_W), rowspec(KV_W), rowspec(KV_W)]
    if attn_layouts:
        def feat_major(n, dtype):
            return jax.ShapeDtypeStruct((n, rows), dtype), pl.BlockSpec((n, tm), lambda i: (0, i))
        nqk = M_HEADS * M_DK
        pairs = [feat_major(nqk, BF16),
                 (jax.ShapeDtypeStruct((rows, nqk), BF16), rowspec(nqk)),
                 feat_major(M_W, F32), feat_major(M_W, F32)]
    else:
        pairs = [(jax.ShapeDtypeStruct((rows, M_QKV_W), F32), rowspec(M_QKV_W)),
                 (jax.ShapeDtypeStruct((rows, M_W), F32), rowspec(M_W))]
    pairs.append((jax.ShapeDtypeStruct((rows, LANE), F32), rowspec(LANE)))
    if attn_layouts:
        pairs.append(feat_major(2 * M_HEADS, F32))
    out_shapes += [p[0] for p in pairs]
    out_specs += [p[1] for p in pairs]
    if attn_layouts:
        def v_tiled(width):
            per_tile = tm // width
            return (jax.ShapeDtypeStruct((n_seq, seq // width, LANE, width), BF16),
                    pl.BlockSpec((1, per_tile, LANE, width),
                                 lambda i: (i // tiles_per_seq, i % tiles_per_seq, 0, 0)))
        (vs_shape, vs_spec), (vw_shape, vw_spec) = v_tiled(v_tiles[0]), v_tiled(v_tiles[1])
        out_shapes += [
            jax.ShapeDtypeStruct((rows, 2 * LANE), BF16),
            vs_shape,
            jax.ShapeDtypeStruct((rows, 2 * LANE), BF16),
            vw_shape,
            jax.ShapeDtypeStruct((rows // CMP_BLOCK, KV_W), F32),
        ]
        out_specs += [rowspec(2 * LANE), vs_spec, rowspec(2 * LANE), vw_spec,
                      pl.BlockSpec((tm // CMP_BLOCK, KV_W), lambda i: (i, 0))]
    return pl.pallas_call(
        functools.partial(_inproj_kernel, tm=tm, tiles_per_seq=tiles_per_seq, attn_layouts=attn_layouts),
        grid=(n_tiles,),
        in_specs=[rowspec(d), mod_spec, mod_spec, pl.BlockSpec((1, d), lambda i: (0, 0)),
                  _resident(w.shape, lambda i: (0, 0)), tab_spec, tab_spec]
                 + [pl.BlockSpec(memory_space=pl.ANY)] * len(kv_stacks),
        out_specs=out_specs,
        out_shape=out_shapes,
        input_output_aliases={7 + k: 1 + k for k in range(len(kv_stacks))},
        compiler_params=_cparams(("arbitrary",)),
    )(x, sc, sh, g, w, cos, sin, *kv_stacks)


def _nt(a, b):
    return lax.dot_general(a, b, (((1,), (1,)), ((), ())), preferred_element_type=F32)


def _softmax_update(s, v, carry, v_feature_major=False):
    m, l, acc = carry
    m_new = jnp.maximum(m, jnp.max(s, axis=-1, keepdims=True))
    alpha = jnp.exp(m - m_new)
    p = jnp.exp(s - m_new)
    l = alpha * l + jnp.sum(p, axis=-1, keepdims=True)
    pb = p.astype(BF16)
    pv = _nt(pb, v) if v_feature_major else jnp.dot(pb, v, preferred_element_type=F32)
    return m_new, l, alpha * acc + pv


def _softmax_init(rows, width):
    return (jnp.full((rows, 1), MASK_NEG, F32), jnp.zeros((rows, 1), F32), jnp.zeros((rows, width), F32))


def _masked_softmax(s, ok, axis, base2=False):
    s = jnp.where(ok, s, -jnp.inf)
    smax = jnp.max(s, axis=axis, keepdims=True)
    e = (jnp.exp2 if base2 else jnp.exp)(s - jnp.where(jnp.isfinite(smax), smax, 0.0))
    den = jnp.sum(e, axis=axis, keepdims=True)
    return e / jnp.where(den > 0, den, 1.0)


def _stable_rank_sublanes(score, sub8):
    nb, n = score.shape
    groups = [score[SUBLANE * r:SUBLANE * (r + 1), :] for r in range(nb // SUBLANE)]
    cnts = [jnp.zeros((SUBLANE, n), F32) for _ in groups]
    for ib in range(nb):
        s_i = jnp.broadcast_to(score[ib:ib + 1, :], (SUBLANE, n))
        for r, grp in enumerate(groups):
            if SUBLANE * r > ib:
                one = jnp.where(s_i >= grp, 1.0, 0.0)
            elif SUBLANE * (r + 1) <= ib:
                one = jnp.where(s_i > grp, 1.0, 0.0)
            else:
                one = jnp.where(sub8 > ib - SUBLANE * r, jnp.where(s_i >= grp, 1.0, 0.0),
                                jnp.where(s_i > grp, 1.0, 0.0))
            cnts[r] = cnts[r] + one
    return jnp.concatenate(cnts, axis=0)


def _causal_rank(score, sub8, n_allowed):
    nb, n = score.shape

    def ranked(rows):
        def branch(sc):
            if rows <= TOPK:
                return jnp.zeros((nb, n), F32)
            cnt = _stable_rank_sublanes(sc[:rows], sub8)
            return cnt if rows == nb else jnp.concatenate([cnt, jnp.zeros((nb - rows, n), F32)], axis=0)
        return branch

    sizes = list(range(RANK_STEP, nb, RANK_STEP)) + [nb]
    which = jnp.minimum((n_allowed + RANK_STEP - 1) // RANK_STEP, len(sizes)) - 1
    return lax.switch(which, [ranked(r) for r in sizes], score)


def _softmax_update_t(s_t, v_ext, carry):
    m, acc = carry
    m_new = jnp.maximum(m, jnp.max(s_t, axis=0, keepdims=True))
    p = jnp.exp2(s_t - m_new).astype(BF16)
    acc = jnp.exp2(m - m_new) * acc + jnp.dot(v_ext, p, preferred_element_type=F32)
    return m_new, acc


def _softmax_init_t(n, cols):
    return (jnp.full((1, cols), MASK_NEG, F32), jnp.zeros((n, cols), F32))


def _attn_prompt_kernel(qt_ref, cm_ref, ksa_ref, vst_ref, kwb_ref, vwt_ref, sm_ref, o_ref, *, tq, tk, nb):
    i = pl.program_id(1)
    q0 = i * tq
    cols = A_G * tq

    gates_t = jax.nn.sigmoid(sm_ref[...].T[:GATE_COLS, :])
    cm = cm_ref[0]
    cm_pad = jnp.concatenate([cm[:, LANE:], jnp.zeros((LANE - nb, LANE), F32)], axis=0)
    cm_vt = cm_pad.T.astype(BF16)

    blk_t = lax.broadcasted_iota(I32, (nb, tq), 0)
    qpos_t = q0 + lax.broadcasted_iota(I32, (nb, tq), 1)
    cur_t = qpos_t // CMP_BLOCK
    forced_t = (blk_t == 0) | (blk_t == cur_t)
    allowed_t = blk_t <= cur_t
    blk_t4 = lax.broadcasted_iota(I32, (nb, cols), 0)
    qpos_t4 = q0 + (lax.broadcasted_iota(I32, (nb, cols), 1) & (tq - 1))
    cmp_ok_t4 = (blk_t4 + 1) * CMP_BLOCK <= qpos_t4 + 1
    cg = max(tq, cols // 2)
    n_cg = cols // cg
    kk_k = lax.broadcasted_iota(I32, (tk, cg), 0)
    qq_k = lax.broadcasted_iota(I32, (tk, cg), 1) & (tq - 1)
    ones_k = jnp.ones((BF16_ROWS, tk), BF16)
    ones_q = jnp.ones((BF16_ROWS, tq), BF16)
    kk_q = lax.broadcasted_iota(I32, (tq, cg), 0)
    qq_q = lax.broadcasted_iota(I32, (tq, cg), 1) & (tq - 1)
    sub8 = lax.broadcasted_iota(I32, (SUBLANE, tq), 0)
    zeros_hd = jnp.zeros((A_HD, tq), BF16)

    def stacked(q_heads, lower):
        return jnp.concatenate([jnp.concatenate([qg, lower], axis=0) for qg in q_heads], axis=1)

    q0t, q_aug, o_cmp = [], [], []
    for h in range(A_KV):
        q_heads = [qt_ref[(A_G * h + g) * A_HD:(A_G * h + g + 1) * A_HD, :] for g in range(A_G)]
        q0t.append(stacked(q_heads, zeros_hd))
        cm_k = cm[:, :LANE] if h == 0 else pltpu.roll(cm[:, :LANE], A_HD, 1)
        p_t = _masked_softmax(jnp.dot(cm_k.astype(BF16), q0t[h], preferred_element_type=F32), cmp_ok_t4, 0,
                              base2=True)
        p_pad = jnp.concatenate([p_t, jnp.zeros((LANE - nb, cols), F32)], axis=0).astype(BF16)
        o_cmp.append(jnp.dot(cm_vt[h * A_HD:(h + 1) * A_HD], p_pad, preferred_element_type=F32))

        imp = p_t[:, 0:tq]
        for g in range(1, A_G):
            imp = imp + p_t[:, g * tq:(g + 1) * tq]
        score = jnp.where(forced_t, FORCE, jnp.where(allowed_t, imp, -1.0))
        cnt = _causal_rank(score, sub8, (q0 + tq - 1) // CMP_BLOCK + 1)
        bias_t = jnp.where(cnt < TOPK, jnp.where(score >= 0, 0.0, SEL_OFF_BIAS), SEL_OFF_BIAS)
        if nb < A_HD:
            bias_t = jnp.concatenate([bias_t, jnp.zeros((A_HD - nb, tq), F32)], axis=0)
        q_aug.append(stacked(q_heads, bias_t.astype(BF16)))

    def sel_scores(kt, masked):
        k0 = pl.multiple_of(kt * tk, tk)
        scores = []
        for h in range(A_KV):
            for c in range(n_cg):
                s = jnp.dot(ksa_ref[0, pl.ds(k0, tk), h * LANE:(h + 1) * LANE],
                            q_aug[h][:, c * cg:(c + 1) * cg], preferred_element_type=F32)
                if masked:
                    s = jnp.where(k0 + kk_k <= q0 + qq_k, s, MASK_NEG)
                scores.append(s)
        return scores

    def sel_update(kt, scores, carries):
        out = []
        for h in range(A_KV):
            v_ext = jnp.concatenate([vst_ref[0, kt, h * A_HD:(h + 1) * A_HD, :], ones_k], axis=0)
            for c in range(n_cg):
                out.append(_softmax_update_t(scores[h * n_cg + c], v_ext, carries[h * n_cg + c]))
        return tuple(out)

    def sel_pair(kp, carries):
        s_a, s_b = sel_scores(2 * kp, False), sel_scores(2 * kp + 1, False)
        return sel_update(2 * kp + 1, s_b, sel_update(2 * kp, s_a, carries))

    n_full = q0 // tk
    carries = lax.fori_loop(0, n_full // 2, sel_pair,
                            tuple(_softmax_init_t(A_HD + BF16_ROWS, cg) for _ in range(A_KV * n_cg)))
    carries = lax.fori_loop(n_full - n_full % 2, n_full,
                            lambda kt, c: sel_update(kt, sel_scores(kt, False), c), carries)

    s_diag = sel_scores(n_full, True)
    n_wt = WINDOW // tq + 1
    tiles = [jnp.maximum(i - wt, 0) for wt in range(n_wt)]
    win_scores = []
    for h in range(A_KV):
        for c in range(n_cg):
            per_tile = []
            for wt in range(n_wt):
                startc = pl.multiple_of(tiles[wt] * tq, tq)
                s = jnp.dot(kwb_ref[0, pl.ds(startc, tq), h * LANE:(h + 1) * LANE],
                            q0t[h][:, c * cg:(c + 1) * cg], preferred_element_type=F32)
                if wt == 0:
                    s = jnp.where(kk_q <= qq_q, s, MASK_NEG)
                else:
                    if wt == n_wt - 1:
                        s = jnp.where(kk_q > qq_q, s, MASK_NEG)
                    s = s + jnp.where(q0 - wt * tq >= 0, 0.0, MASK_NEG)
                per_tile.append(s)
            win_scores.append(per_tile)

    carries = sel_update(n_full, s_diag, carries)
    o_sel = [jnp.concatenate([acc[:A_HD] / acc[A_HD:A_HD + 1] for _, acc in carries[h * n_cg:(h + 1) * n_cg]], axis=1)
             for h in range(A_KV)]

    o_win = []
    for h in range(A_KV):
        v_ext = [jnp.concatenate([vwt_ref[0, tile, h * A_HD:(h + 1) * A_HD, :], ones_q], axis=0) for tile in tiles]
        outs = []
        for c in range(n_cg):
            per_tile = win_scores[h * n_cg + c]
            m = functools.reduce(jnp.maximum, [jnp.max(s, axis=0, keepdims=True) for s in per_tile])
            acc = functools.reduce(jnp.add, [jnp.dot(v, jnp.exp2(s - m).astype(BF16), preferred_element_type=F32)
                                             for v, s in zip(v_ext, per_tile)])
            outs.append(acc[:A_HD] / acc[A_HD:A_HD + 1])
        o_win.append(jnp.concatenate(outs, axis=1))

    mixes = []
    for h in range(A_KV):
        for g in range(A_G):
            head = A_G * h + g
            sl = (slice(None), slice(g * tq, (g + 1) * tq))
            mixes.append(gates_t[3 * head:3 * head + 1, :] * o_cmp[h][sl] + gates_t[3 * head + 1:3 * head + 2, :] * o_sel[h][sl]
                         + gates_t[3 * head + 2:3 * head + 3, :] * o_win[h][sl])
    o_ref[...] = jnp.concatenate(mixes, axis=0).T.astype(o_ref.dtype)


def _attn_prompt_call(qt, cm, ksa, vst, kwb, vwt, sm, *, batch, seq, tq, tk):
    nb = seq // CMP_BLOCK
    tiles = seq // tq
    assert nb <= A_HD and tq % LANE == 0 and tk % tq == 0 and seq % tk == 0 and WINDOW % tq == 0

    def full(width):
        return pl.BlockSpec((1, seq, width), lambda b, i: (b, 0, 0))

    def full_t(arr):
        return pl.BlockSpec((1,) + arr.shape[1:], lambda b, i: (b, 0, 0, 0))

    return pl.pallas_call(
        functools.partial(_attn_prompt_kernel, tq=tq, tk=tk, nb=nb),
        grid=(batch, tiles),
        in_specs=[pl.BlockSpec((A_W, tq), lambda b, i: (0, b * tiles + i)),
                  pl.BlockSpec((1, nb, KV_W), lambda b, i: (b, 0, 0)),
                  full(2 * LANE), full_t(vst), full(2 * LANE), full_t(vwt),
                  pl.BlockSpec((tq, LANE), lambda b, i: (b * tiles + i, 0))],
        out_specs=pl.BlockSpec((tq, A_W), lambda b, i: (b * tiles + i, 0)),
        out_shape=jax.ShapeDtypeStruct((batch * seq, A_W), BF16),
        compiler_params=_cparams(("arbitrary", "arbitrary")),
    )(qt, cm.reshape(batch, nb, KV_W), ksa.reshape(batch, seq, 2 * LANE), vst, kwb.reshape(batch, seq, 2 * LANE),
      vwt, sm)


def _page_spec(k, pages_per_step, layer):
    def index(b, j, pt):
        return (layer, pt[b, j * pages_per_step + k], 0, 0, 0, 0)
    return pl.BlockSpec((1, 1, 2, A_KV, A_HD, PAGE_SIZE), index)


def _page_kv(ref):
    n_pos = ref.shape[-1]
    return ref[0, 0, 0].reshape(A_KV * A_HD, n_pos), ref[0, 0, 1].reshape(A_KV * A_HD, n_pos)


def _cmp_means_kernel(pt_ref, *refs, n_pages):
    del pt_ref
    o_ref = refs[n_pages]
    per_page = PAGE_SIZE // CMP_BLOCK
    nbp = n_pages * per_page
    pos_blk = lax.broadcasted_iota(I32, (PAGE_SIZE, nbp), 0) // CMP_BLOCK
    blk = lax.broadcasted_iota(I32, (PAGE_SIZE, nbp), 1)
    acc = jnp.zeros((KV_W, nbp), F32)
    for k in range(n_pages):
        page = refs[k][0, 0].reshape(KV_W, PAGE_SIZE)
        place = jnp.where(blk == per_page * k + pos_blk, 1.0, 0.0).astype(BF16)
        for part in _split_bf16(page, 2):
            acc = acc + jnp.dot(part, place, preferred_element_type=F32)
    o_ref[0] = acc * (1.0 / CMP_BLOCK)


def _cmp_means_call(cache, page_table, layer):
    dec_b, n_pages = page_table.shape
    nbp = n_pages * (PAGE_SIZE // CMP_BLOCK)
    grid_spec = pltpu.PrefetchScalarGridSpec(
        num_scalar_prefetch=1,
        grid=(dec_b, 1),
        in_specs=[_page_spec(k, n_pages, layer) for k in range(n_pages)],
        out_specs=pl.BlockSpec((1, KV_W, nbp), lambda b, j, pt: (b, 0, 0)),
    )
    return pl.pallas_call(
        functools.partial(_cmp_means_kernel, n_pages=n_pages),
        grid_spec=grid_spec,
        out_shape=jax.ShapeDtypeStruct((dec_b, KV_W, nbp), F32),
        compiler_params=_cparams(("arbitrary", "arbitrary")),
    )(page_table, *([cache] * n_pages))


def _attn_sample_kernel(pt_ref, q_ref, cm_ref, kn_ref, win_ref, wn_ref, gl_ref, *refs,
                        pages_per_step, dec_t, past_len, nbp):
    del pt_ref
    page_refs = refs[:pages_per_step]
    o_ref = refs[pages_per_step]
    m_scr, l_scr, acc_scr, bias_scr, expand_scr, ocmp_scr = refs[pages_per_step + 1:]
    j = pl.program_id(1)
    rows = q_ref.shape[1]
    grp = A_KV * dec_t
    tk = pages_per_step * PAGE_SIZE
    blocks_per_step = tk // CMP_BLOCK
    qb = q_ref[0]
    t_row = lax.broadcasted_iota(I32, (rows, 1), 0) % dec_t

    @pl.when(j == 0)
    def _():
        cm_t = cm_ref[0]
        blk = lax.broadcasted_iota(I32, (rows, nbp), 1)
        ok = (blk + 1) * CMP_BLOCK <= past_len + t_row + 1
        p = _masked_softmax(jnp.dot(qb, cm_t[:LANE].astype(BF16), preferred_element_type=F32), ok, 1)
        ocmp_scr[...] = _nt(p.astype(BF16), cm_t[LANE:].astype(BF16))
        imp = p[0:grp]
        for g in range(1, A_G):
            imp = imp + p[g * grp:(g + 1) * grp]
        blk8 = lax.broadcasted_iota(I32, (grp, nbp), 1)
        cur = (past_len + lax.broadcasted_iota(I32, (grp, nbp), 0) % dec_t) // CMP_BLOCK
        score = jnp.where(blk8 == 0, FORCE, jnp.where(blk8 == cur, FORCE, jnp.where(blk8 <= cur, imp, -1.0)))
        cnt = jnp.where(FORCE > score, 1.0, 0.0)
        for ib in range(nbp):
            c_i = score[:, ib:ib + 1]
            cnt = cnt + jnp.where(blk8 > ib, jnp.where(c_i >= score, 1.0, 0.0), jnp.where(c_i > score, 1.0, 0.0))
        bias = jnp.where(cnt < TOPK, jnp.where(score >= 0, 0.0, SEL_OFF_BIAS), SEL_OFF_BIAS)
        bias = jnp.concatenate([bias] * A_G, axis=0)
        for jj in range(bias_scr.shape[0]):
            bias_scr[jj] = pltpu.roll(bias, (nbp - jj * blocks_per_step) % nbp, 1).astype(BF16)
        blk_e = lax.broadcasted_iota(I32, (nbp, tk), 0)
        key_blk = lax.broadcasted_iota(I32, (nbp, tk), 1) // CMP_BLOCK
        expand_scr[...] = jnp.where(blk_e == key_blk, 1.0, 0.0).astype(BF16)
        m_scr[...] = jnp.full(m_scr.shape, MASK_NEG, F32)
        l_scr[...] = jnp.zeros(l_scr.shape, F32)
        acc_scr[...] = jnp.zeros(acc_scr.shape, F32)

    pages = [_page_kv(r) for r in page_refs]
    keys_t = jnp.concatenate([k.astype(BF16) for k, _ in pages], axis=1)
    vals_t = jnp.concatenate([v.astype(BF16) for _, v in pages], axis=1)
    s = (jnp.dot(qb, keys_t, preferred_element_type=F32)
         + jnp.dot(bias_scr[j], expand_scr[...], preferred_element_type=F32))
    m, l, acc = _softmax_update(s, vals_t, (m_scr[...], l_scr[...], acc_scr[...]), v_feature_major=True)
    m_scr[...] = m
    l_scr[...] = l
    acc_scr[...] = acc

    @pl.when(j == pl.num_programs(1) - 1)
    def _():
        new_ok = lax.broadcasted_iota(I32, (rows, kn_ref.shape[1]), 1) <= t_row

        def new_rows(carry, ref):
            kv = ref[0]
            s_n = jnp.where(new_ok, _nt(qb, kv[:, :LANE].astype(BF16)), MASK_NEG)
            return _softmax_update(s_n, kv[:, LANE:].astype(BF16), carry)

        _, l_s, acc_s = new_rows((m_scr[...], l_scr[...], acc_scr[...]), kn_ref)
        o_sel = acc_s / l_s

        kw_t, vw_t = _page_kv(win_ref)
        wb = kw_t.shape[1]
        jw = lax.broadcasted_iota(I32, (rows, wb), 1)
        s_w = jnp.dot(qb, kw_t.astype(BF16), preferred_element_type=F32)
        s_w = jnp.where(jw > t_row + (wb - WINDOW), s_w, MASK_NEG)
        carry = _softmax_update(s_w, vw_t.astype(BF16), _softmax_init(rows, LANE), v_feature_major=True)
        _, l_w, acc_w = new_rows(carry, wn_ref)
        o_win = acc_w / l_w

        gate = jax.nn.sigmoid(gl_ref[0])
        o_ref[0] = gate[:, 0:1] * ocmp_scr[...] + gate[:, 1:2] * o_sel + gate[:, 2:3] * o_win


def _attn_sample_call(page_table, q, cm, kv_new, cache_sel, cache_win, win_new, gate_logits, layer, *,
                      pages_per_step, dec_t):
    dec_b, n_pages = page_table.shape
    rows = q.shape[1]
    nbp = cm.shape[2]
    wb = cache_win.shape[-1]
    past_len = n_pages * PAGE_SIZE
    assert nbp == LANE and past_len >= wb and wb >= WINDOW

    def per_b(shape):
        return pl.BlockSpec((1,) + shape, lambda b, j, pt: (b, 0, 0))

    grid_spec = pltpu.PrefetchScalarGridSpec(
        num_scalar_prefetch=1,
        grid=(dec_b, n_pages // pages_per_step),
        in_specs=[per_b((rows, LANE)), per_b((KV_W, nbp)), per_b(kv_new.shape[1:]),
                  pl.BlockSpec((1, 1, 2, A_KV, A_HD, wb), lambda b, j, pt: (layer, b, 0, 0, 0, 0)),
                  per_b(win_new.shape[1:]), per_b((rows, LANE))]
                 + [_page_spec(k, pages_per_step, layer) for k in range(pages_per_step)],
        out_specs=per_b((rows, LANE)),
        scratch_shapes=[pltpu.VMEM((rows, 1), F32), pltpu.VMEM((rows, 1), F32), pltpu.VMEM((rows, LANE), F32),
                        pltpu.VMEM((n_pages // pages_per_step, rows, nbp), BF16),
                        pltpu.VMEM((nbp, pages_per_step * PAGE_SIZE), BF16), pltpu.VMEM((rows, LANE), F32)],
    )
    return pl.pallas_call(
        functools.partial(_attn_sample_kernel, pages_per_step=pages_per_step, dec_t=dec_t,
                          past_len=past_len, nbp=nbp),
        grid_spec=grid_spec,
        out_shape=jax.ShapeDtypeStruct((dec_b, rows, LANE), F32),
        compiler_params=_cparams(("arbitrary", "arbitrary")),
    )(page_table, q, cm, kv_new, cache_win, win_new, gate_logits, *([cache_sel] * pages_per_step))


def _split_bf16(a, terms):
    parts = []
    for _ in range(terms):
        piece = a.astype(BF16)
        parts.append(piece)
        a = a - piece.astype(F32)
    return parts


def _log_sigmoid(x):
    return jnp.minimum(x, 0.0) - jnp.log1p(jnp.exp(-jnp.abs(x)))


def _mlstm_kernel(qkv_ref, sm_ref, smt_ref, mo_ref, bc_ref, br_ref, g_ref, c0_ref, n0_ref, m0_ref,
                  o_ref, c_ref, n_ref, m_ref, *, chunk, valid_len, seqs):
    @pl.when(pl.program_id(1) == 0)
    def _():
        c_ref[...] = c0_ref[...]
        n_ref[...] = n0_ref[...]
        m_ref[...] = m0_ref[...]

    ng = 2 * M_HEADS
    t_i = lax.broadcasted_iota(I32, (chunk, chunk), 0)
    s_i = lax.broadcasted_iota(I32, (chunk, chunk), 1)
    causal = s_i <= t_i
    tri = jnp.where(causal, 1.0, 0.0).astype(BF16)
    tri_t = jnp.where(t_i <= s_i, 1.0, 0.0).astype(BF16)
    for sq in range(seqs):
        _mlstm_sequence(sq, slice(sq * chunk, (sq + 1) * chunk), qkv_ref, sm_ref, smt_ref, mo_ref, bc_ref, br_ref,
                        g_ref, o_ref, c_ref, n_ref, m_ref, causal, tri, tri_t, chunk, valid_len)


def _mlstm_sequence(sq, rows, qkv_ref, sm_ref, smt_ref, mo_ref, bc_ref, br_ref, g_ref, o_ref, c_ref, n_ref, m_ref,
                    causal, tri, tri_t, chunk, valid_len):
    ng = 2 * M_HEADS
    gate_c = sm_ref[rows, GATE_COLS:GATE_COLS + ng] + bc_ref[...]
    gate_r = smt_ref[sq] + br_ref[...]
    lf_c = _log_sigmoid(gate_c)
    lf_r = _log_sigmoid(gate_r)
    ig_c, ig_r = gate_c, gate_r
    if valid_len < chunk:
        tc = lax.broadcasted_iota(I32, (chunk, ng), 0)
        tr = lax.broadcasted_iota(I32, (ng, chunk), 1)
        lf_c = jnp.where(tc < valid_len, lf_c, 0.0)
        lf_r = jnp.where(tr < valid_len, lf_r, 0.0)
        ig_c = jnp.where(tc < valid_len, ig_c, MASK_NEG)
        ig_r = jnp.where(tr < valid_len, ig_r, MASK_NEG)
    b_c = sum(jnp.dot(tri, part, preferred_element_type=F32) for part in _split_bf16(lf_c, 3))
    b_r = sum(jnp.dot(part, tri_t, preferred_element_type=F32) for part in _split_bf16(lf_r, 3))

    nqk = M_HEADS * M_DK
    for n in range(M_HEADS):
        q = qkv_ref[rows, n * M_DK:(n + 1) * M_DK]
        k = qkv_ref[rows, nqk + n * M_DK:nqk + (n + 1) * M_DK]
        v = qkv_ref[rows, 2 * nqk + n * M_DV:2 * nqk + (n + 1) * M_DV]
        qb, kb, vb = q.astype(BF16), k.astype(BF16), v.astype(BF16)
        fcol = M_HEADS + n
        bc = b_c[:, fcol:fcol + 1]
        br = b_r[fcol:fcol + 1, :]
        igc = ig_c[:, n:n + 1]
        igr = ig_r[n:n + 1, :]
        b_end = bc[chunk - 1:chunk, :]
        m_prev = m_ref[sq, :, n:n + 1]
        c_prev = c_ref[sq, n]
        n_prev = n_ref[sq, n:n + 1, :]

        a = bc + m_prev
        d = jnp.where(causal, bc + (igr - br), -jnp.inf)
        m_t = jnp.maximum(a, jnp.max(d, axis=1, keepdims=True))
        w = _nt(qb, kb) * jnp.exp(d - m_t)
        aw = jnp.exp(a - m_t)
        num = (jnp.dot(w.astype(BF16), vb, preferred_element_type=F32)
               + aw * jnp.dot(qb, c_prev.astype(BF16), preferred_element_type=F32))
        den = jnp.sum(w, axis=1, keepdims=True) + aw * jnp.sum(q * n_prev, axis=1, keepdims=True)
        hcell = num / jnp.maximum(jnp.abs(den), jnp.exp(-m_t))

        hn = hcell * lax.rsqrt(jnp.mean(hcell * hcell, axis=-1, keepdims=True) + EPS) * g_ref[:, n * M_DV:(n + 1) * M_DV]
        o_ref[rows, n * M_DV:(n + 1) * M_DV] = (
            jax.nn.sigmoid(mo_ref[rows, n * M_DV:(n + 1) * M_DV]) * hn).astype(o_ref.dtype)

        wl = b_end - bc + igc
        m_new = jnp.maximum(b_end + m_prev, jnp.max(wl, axis=0, keepdims=True))
        decay = jnp.exp(b_end + m_prev - m_new)
        kws = k * jnp.exp(wl - m_new)
        c_ref[sq, n] = decay * c_prev + lax.dot_general(kws.astype(BF16), vb, (((0,), (0,)), ((), ())),
                                                        preferred_element_type=F32)
        n_ref[sq, n:n + 1, :] = decay * n_prev + jnp.sum(kws, axis=0, keepdims=True)
        m_ref[sq, :, n:n + 1] = m_new


def _mlstm_call(qkv, sm, smt, mo, b_if, norm_g, c0, n0, m0, *, batch, seq, chunk, valid_len, seqs=1):
    nc = seq // chunk
    ng = 2 * M_HEADS
    assert batch % seqs == 0 and (seqs == 1 or nc == 1)

    def rowspec(width):
        return pl.BlockSpec((seqs * chunk, width), lambda b, c: (b * nc + c, 0))

    def const(shape):
        return pl.BlockSpec(shape, lambda b, c: (0,) * len(shape))

    state_specs = [pl.BlockSpec((seqs, M_HEADS, M_DK, M_DV), lambda b, c: (b, 0, 0, 0)),
                   pl.BlockSpec((seqs, M_HEADS, M_DK), lambda b, c: (b, 0, 0)),
                   pl.BlockSpec((seqs, 1, M_HEADS), lambda b, c: (b, 0, 0))]
    return pl.pallas_call(
        functools.partial(_mlstm_kernel, chunk=chunk, valid_len=valid_len, seqs=seqs),
        grid=(batch // seqs, nc),
        in_specs=[rowspec(M_QKV_W), rowspec(LANE), pl.BlockSpec((seqs, ng, chunk), lambda b, c: (b, 0, c)),
                  rowspec(M_W), const((1, ng)), const((ng, 1)), const((1, M_W))] + state_specs,
        out_specs=[rowspec(M_W)] + state_specs,
        out_shape=[jax.ShapeDtypeStruct((batch * seq, M_W), BF16),
                   jax.ShapeDtypeStruct((batch, M_HEADS, M_DK, M_DV), F32),
                   jax.ShapeDtypeStruct((batch, M_HEADS, M_DK), F32),
                   jax.ShapeDtypeStruct((batch, 1, M_HEADS), F32)],
        compiler_params=_cparams(("arbitrary", "arbitrary")),
    )(qkv, sm, smt, mo, b_if.reshape(1, ng), b_if.reshape(ng, 1), norm_g.reshape(1, M_W), c0, n0, m0)


def _mlstm_t_kernel(qt_ref, k_ref, vt_ref, mot_ref, sm_ref, smt_ref, bc_ref, br_ref, g_ref, o_ref, ct_ref, n_ref, m_ref,
                    *, chunk):
    @pl.when(pl.program_id(1) == 0)
    def _():
        ct_ref[...] = jnp.zeros(ct_ref.shape, F32)
        n_ref[...] = jnp.zeros(n_ref.shape, F32)
        m_ref[...] = jnp.zeros(m_ref.shape, F32)

    ng = 2 * M_HEADS
    gate_c = sm_ref[:, GATE_COLS:GATE_COLS + ng] + bc_ref[...]
    gate_r = smt_ref[...] + br_ref[...]
    s_i = lax.broadcasted_iota(I32, (chunk, chunk), 0)
    t_i = lax.broadcasted_iota(I32, (chunk, chunk), 1)
    causal = s_i <= t_i
    tri_c = jnp.where(t_i <= s_i, 1.0, 0.0).astype(BF16)
    tri_r = jnp.where(causal, 1.0, 0.0).astype(BF16)
    b_c = sum(jnp.dot(tri_c, part, preferred_element_type=F32) for part in _split_bf16(_log_sigmoid(gate_c), 3))
    b_r = sum(jnp.dot(part, tri_r, preferred_element_type=F32) for part in _split_bf16(_log_sigmoid(gate_r), 3))
    r_c = gate_c[:, :M_HEADS] - b_c[:, M_HEADS:]
    g_full = jnp.concatenate([g_ref[...]] * (chunk // LANE), axis=1)

    outs = []
    for n in range(M_HEADS):
        qt = qt_ref[n * M_DK:(n + 1) * M_DK, :]
        kb = k_ref[:, n * M_DK:(n + 1) * M_DK]
        vt = vt_ref[n * M_DV:(n + 1) * M_DV, :]
        b_row = b_r[M_HEADS + n:M_HEADS + n + 1, :]
        ig_row = gate_r[n:n + 1, :]
        b_end = b_c[chunk - 1:chunk, M_HEADS + n:M_HEADS + n + 1]
        m_prev = m_ref[0, :, n:n + 1]
        ct_prev = ct_ref[0, n]
        n_prev = n_ref[0, n:n + 1, :]

        a = b_row + m_prev
        d_t = jnp.where(causal, r_c[:, n:n + 1] + b_row, -jnp.inf)
        m_t = jnp.maximum(a, jnp.max(d_t, axis=0, keepdims=True))
        w_t = jnp.dot(kb, qt, preferred_element_type=F32) * jnp.exp(d_t - m_t)
        aw = jnp.exp(a - m_t)
        num = (jnp.dot(vt.astype(BF16), w_t.astype(BF16), preferred_element_type=F32)
               + aw * jnp.dot(ct_prev.astype(BF16), qt, preferred_element_type=F32))
        den = (jnp.sum(w_t, axis=0, keepdims=True)
               + aw * jnp.dot(n_prev.astype(BF16), qt, preferred_element_type=F32))
        hcell = num / jnp.maximum(jnp.abs(den), jnp.exp(-m_t))
        hn = (hcell * lax.rsqrt(jnp.mean(hcell * hcell, axis=0, keepdims=True) + EPS)
              * g_full[n * M_DV:(n + 1) * M_DV, :])
        outs.append(jax.nn.sigmoid(mot_ref[n * M_DV:(n + 1) * M_DV, :]) * hn)

        wl = b_end - b_row + ig_row
        m_new = jnp.maximum(b_end + m_prev, jnp.max(wl, axis=1, keepdims=True))
        decay = jnp.exp(b_end + m_prev - m_new)
        ws = jnp.exp(wl - m_new)
        ct_ref[0, n] = decay * ct_prev + jnp.dot((vt * ws).astype(BF16), kb, preferred_element_type=F32)
        n_ref[0, n:n + 1, :] = decay * n_prev + jnp.dot(ws.astype(BF16), kb, preferred_element_type=F32)
        m_ref[0, :, n:n + 1] = m_new
    o_ref[...] = jnp.concatenate(outs, axis=0).T.astype(o_ref.dtype)


def _mlstm_t_call(qt, k, vt, mot, sm, smt, b_if, norm_g, *, batch, seq, chunk):
    nc = seq // chunk
    ng = 2 * M_HEADS
    nqk = M_HEADS * M_DK
    assert chunk % LANE == 0

    def colspec(n):
        return pl.BlockSpec((n, chunk), lambda b, c: (0, b * nc + c))

    def rowspec(width):
        return pl.BlockSpec((chunk, width), lambda b, c: (b * nc + c, 0))

    def const(shape):
        return pl.BlockSpec(shape, lambda b, c: (0,) * len(shape))

    state_specs = [pl.BlockSpec((1, M_HEADS, M_DV, M_DK), lambda b, c: (b, 0, 0, 0)),
                   pl.BlockSpec((1, M_HEADS, M_DK), lambda b, c: (b, 0, 0)),
                   pl.BlockSpec((1, 1, M_HEADS), lambda b, c: (b, 0, 0))]
    return pl.pallas_call(
        functools.partial(_mlstm_t_kernel, chunk=chunk),
        grid=(batch, nc),
        in_specs=[colspec(nqk), rowspec(nqk), colspec(M_W), colspec(M_W), rowspec(LANE), colspec(ng),
                  const((1, ng)), const((ng, 1)), const((M_W, LANE))],
        out_specs=[rowspec(M_W)] + state_specs,
        out_shape=[jax.ShapeDtypeStruct((batch * seq, M_W), BF16),
                   jax.ShapeDtypeStruct((batch, M_HEADS, M_DV, M_DK), F32),
                   jax.ShapeDtypeStruct((batch, M_HEADS, M_DK), F32),
                   jax.ShapeDtypeStruct((batch, 1, M_HEADS), F32)],
        compiler_params=_cparams(("arbitrary", "arbitrary")),
    )(qt, k, vt, mot, sm, smt, b_if.reshape(1, ng), b_if.reshape(ng, 1),
      jnp.broadcast_to(norm_g.reshape(M_W, 1), (M_W, LANE)))


def _ffn_kernel(x_ref, a_ref, mh_ref, wo_ref, g1_ref, sc_ref, sh_ref, g2_ref, n2_ref, wu_ref, wc_ref, bc_ref,
                wd_ref, hist_ref, *refs, tm, tiles_per_seq, shift, ff, ch, final):
    if final:
        fg_ref, xo_ref, cs_ref, y_ref, carry_scr, up_scr = refs
    else:
        xo_ref, cs_ref, carry_scr, up_scr = refs
    hist_rows = (CONV_W - 1) * shift
    hoff = up_scr.shape[0] - tm

    @pl.when(pl.program_id(0) % tiles_per_seq == 0)
    def _():
        carry_scr[...] = hist_ref[0]

    y = (jnp.dot(a_ref[...], wo_ref[:A_W, :], preferred_element_type=F32)
         + jnp.dot(mh_ref[...], wo_ref[A_W:, :], preferred_element_type=F32))
    x1 = x_ref[...] + g1_ref[0] * y
    xn = x1 * lax.rsqrt(jnp.mean(x1 * x1, axis=-1, keepdims=True) + EPS) * n2_ref[...]
    xb = (xn * (1.0 + sc_ref[0]) + sh_ref[0]).astype(BF16)

    acc = jnp.zeros(x1.shape, F32)
    for lo, hi in ch:
        w = hi - lo
        u = jnp.dot(xb, wu_ref[:, lo:hi], preferred_element_type=F32)
        gt = jnp.dot(xb, wu_ref[:, ff + lo:ff + hi], preferred_element_type=F32)
        up_scr[hoff - hist_rows:hoff, :w] = carry_scr[:, lo:hi]
        up_scr[hoff:, :w] = u
        tail = u[tm - hist_rows:, :]
        carry_scr[:, lo:hi] = tail
        cs_ref[0, :, lo:hi] = tail
        conv = bc_ref[:, lo:hi]
        for jj in range(CONV_W - 1):
            start = hoff - (CONV_W - 1 - jj) * shift
            conv = conv + wc_ref[jj:jj + 1, lo:hi] * up_scr[start:start + tm, :w]
        conv = conv + wc_ref[CONV_W - 1:CONV_W, lo:hi] * u
        hid = conv * jax.nn.sigmoid(conv) * gt
        acc = acc + jnp.dot(hid.astype(BF16), wd_ref[lo:hi, :], preferred_element_type=F32)

    x2 = x1 + g2_ref[0] * acc
    xo_ref[...] = x2
    if final:
        y_ref[...] = x2 * lax.rsqrt(jnp.mean(x2 * x2, axis=-1, keepdims=True) + EPS) * fg_ref[...]


def _ffn_call(x, a_out, m_out, wo, g1, sc, sh, g2, n2, wu, wc, bc, wd, hist, final_g, *,
              tm, tiles_per_seq, shift, per_row_mod, ch):
    rows, d = x.shape
    ff = wd.shape[0]
    groups, hist_rows, _ = hist.shape
    assert hist_rows == (CONV_W - 1) * shift and tm >= hist_rows
    assert ch[0][0] == 0 and ch[-1][1] == ff and all(a[1] == b[0] for a, b in zip(ch, ch[1:]))
    final = final_g is not None
    hoff = -(-hist_rows // SUBLANE) * SUBLANE
    mod_spec = _mod_specs(per_row_mod, tm, d, tiles_per_seq)

    def rowspec(width):
        return pl.BlockSpec((tm, width), lambda i: (i, 0))

    def const(arr):
        return _resident(arr.shape, lambda i: (0,) * arr.ndim)

    hist_spec = pl.BlockSpec((1, hist_rows, ff), lambda i: (i // tiles_per_seq, 0, 0))
    in_specs = [rowspec(d), rowspec(A_W), rowspec(M_W), const(wo), mod_spec, mod_spec, mod_spec, mod_spec,
                pl.BlockSpec((1, d), lambda i: (0, 0)), const(wu), const(wc), const(bc), const(wd), hist_spec]
    args = [x, a_out, m_out, wo, g1, sc, sh, g2, n2, wu, wc, bc, wd, hist]
    out_specs = [rowspec(d), hist_spec]
    out_shape = [jax.ShapeDtypeStruct((rows, d), F32), jax.ShapeDtypeStruct(hist.shape, F32)]
    if final:
        in_specs.append(pl.BlockSpec((1, d), lambda i: (0, 0)))
        args.append(final_g)
        out_specs.append(rowspec(d))
        out_shape.append(jax.ShapeDtypeStruct((rows, d), F32))
    return pl.pallas_call(
        functools.partial(_ffn_kernel, tm=tm, tiles_per_seq=tiles_per_seq, shift=shift, ff=ff, ch=ch, final=final),
        grid=(rows // tm,),
        in_specs=in_specs,
        out_specs=out_specs,
        out_shape=out_shape,
        scratch_shapes=[pltpu.VMEM((hist_rows, ff), F32),
                        pltpu.VMEM((hoff + tm, max(hi - lo for lo, hi in ch)), F32)],
        compiler_params=_cparams(("arbitrary",)),
    )(*args)


def _prep_w_in(w_in):
    attn_end = OFF_M
    mlstm_lo = attn_end + GATE_COLS
    mlstm_hi = mlstm_lo + M_QKV_W + M_W
    assert mlstm_hi + 2 * M_HEADS == w_in.shape[-1]
    w = w_in.astype(BF16)
    pad = jnp.zeros(w.shape[:-1] + (LANE - GATE_COLS - 2 * M_HEADS,), BF16)
    return jnp.concatenate([w[..., :attn_end], w[..., mlstm_lo:mlstm_hi], w[..., attn_end:mlstm_lo],
                            w[..., mlstm_hi:], pad], axis=-1)


def _rope_tables(pos):
    half = A_HD // 2
    freq = ROPE_THETA ** (-2.0 * jnp.arange(half, dtype=F32) / A_HD)
    ang = pos.astype(F32)[:, None] * freq[None, :]
    cos, sin = jnp.cos(ang), jnp.sin(ang)
    return jnp.concatenate([cos, cos, cos, cos], axis=-1), jnp.concatenate([-sin, sin, -sin, sin], axis=-1)


def _kv_out(kv, lead):
    return kv.reshape(lead + (2, A_KV, A_HD))


def kernel(x_prompt, x_sample, cache_cmp_kv, cache_sel_kv, cache_win_kv, state_mlstm_C, state_mlstm_n,
           state_mlstm_m, state_ffn_conv, page_table, c_prompt, c_sample, norm1_g, norm2_g, w_mod, b_mod,
           w_in, b_if, mlstm_norm_g, w_out, w_up, w_conv, b_conv, w_down, final_g):
    batch, seq, d = x_prompt.shape
    dec_b, dec_t, _ = x_sample.shape
    depth = w_in.shape[0]
    ff = w_down.shape[1]
    n_pages = page_table.shape[1]
    past_len = n_pages * PAGE_SIZE
    wb = cache_win_kv.shape[2]
    rows_s = dec_t * dec_b
    assert A_KV * dec_t == SUBLANE and dec_t >= CONV_W - 1 and dec_b % SUBLANE == 0

    tm_p = min(512, seq)
    tq, tk = 512, min(512, seq)
    chunk_p = min(256, seq)
    ff_split = -(-ff // (2 * MXU_TILE)) * MXU_TILE
    ch = ((0, ff_split), (ff_split, ff))
    pages_per_step = n_pages
    t_pad = BF16_ROWS

    wi = _prep_w_in(w_in)
    wo, wu, wd = w_out.astype(BF16), w_up.astype(BF16), w_down.astype(BF16)
    fg = final_g.reshape(1, d)

    n_c = batch + dec_b
    c_all = jnp.concatenate([c_prompt, c_sample, jnp.zeros((-n_c % SUBLANE, d), F32)], axis=0)
    mod = _mod_call(c_all, w_mod, b_mod)

    cos_p, sin_p = _rope_tables(jnp.arange(seq))
    cos_s, sin_s = [jnp.repeat(t, dec_b, axis=0) for t in _rope_tables(past_len + jnp.arange(dec_t))]

    cache_cmp = cache_cmp_kv.transpose(0, 1, 3, 4, 5, 2)
    cache_sel = cache_sel_kv.transpose(0, 1, 3, 4, 5, 2)
    cache_win = cache_win_kv.transpose(0, 1, 3, 4, 5, 2)

    def kv_from_t(kv_t):
        return kv_t.reshape(kv_t.shape[:2] + (2, A_KV, A_HD, kv_t.shape[-1])).transpose(0, 1, 5, 2, 3, 4)

    def to_bt(a):
        a = a.reshape(dec_t, dec_b, a.shape[-1]).transpose(1, 0, 2)
        return jnp.pad(a, ((0, 0), (0, t_pad - dec_t), (0, 0)))

    xp = x_prompt.reshape(batch * seq, d)
    xs = x_sample.transpose(1, 0, 2).reshape(rows_s, d)
    outs_p = [[] for _ in range(4)]
    outs_s = [[] for _ in range(7)]
    kv_stacks = [jnp.zeros((depth, batch, KV_W, seq), F32) for _ in range(3)]
    y_p = y_s = None
    for l in range(depth):
        last = l == depth - 1
        mods = [mod[l, :, k * d:(k + 1) * d] for k in range(6)]
        sh1_p, sc1_p, g1_p, sh2_p, sc2_p, g2_p = [m[:batch].reshape(batch, 1, d) for m in mods]
        sh1_s, sc1_s, g1_s, sh2_s, sc2_s, g2_s = [jnp.tile(m[batch:n_c], (dec_t, 1)).reshape(1, rows_s, d)
                                                  for m in mods]
        n1, n2 = norm1_g[l].reshape(1, d), norm2_g[l].reshape(1, d)
        wc, bc = w_conv[l], b_conv[l].reshape(1, ff)

        tiles_p = seq // tm_p
        (qt, *kv_stacks, mqt, mk, mvt, mot, sm, smt, ksa, vst, kwb, vwt, cm) = _inproj_call(
            xp, sc1_p, sh1_p, n1, wi[l], cos_p, sin_p, tm=tm_p, tiles_per_seq=tiles_p, per_row_mod=False,
            attn_layouts=True, v_tiles=(tk, tq), kv_stacks=kv_stacks, layer=l)
        a_out = _attn_prompt_call(qt, cm, ksa, vst, kwb, vwt, sm, batch=batch, seq=seq, tq=tq, tk=tk)
        m_out, st_ct, st_n, st_m = _mlstm_t_call(mqt, mk, mvt, mot, sm, smt, b_if[l], mlstm_norm_g[l],
                                                 batch=batch, seq=seq, chunk=chunk_p)
        st_c = st_ct.transpose(0, 1, 3, 2)
        res = _ffn_call(xp, a_out, m_out, wo[l], g1_p, sc2_p, sh2_p, g2_p, n2, wu[l], wc, bc, wd[l],
                        jnp.zeros((batch, CONV_W - 1, ff), F32), fg if last else None,
                        tm=tm_p, tiles_per_seq=tiles_p, shift=1, per_row_mod=False, ch=ch)
        xp, conv_p = res[0], res[1]
        if last:
            y_p = res[2]
        for lst, arr in zip(outs_p, (st_c, st_n, st_m.reshape(batch, M_HEADS), conv_p)):
            lst.append(arr)

        (q, kvc, kvs, kvw, mqkv, mo, sm) = _inproj_call(
            xs, sc1_s, sh1_s, n1, wi[l], cos_s, sin_s, tm=rows_s, tiles_per_seq=1, per_row_mod=True,
            attn_layouts=False)
        cm_past = _cmp_means_call(cache_cmp, page_table, l)
        q5 = q.reshape(dec_t, dec_b, A_KV, A_G, LANE).transpose(1, 3, 2, 0, 4)
        q5 = jnp.concatenate([q5[:, :, :1], jnp.roll(q5[:, :, 1:], A_HD, axis=-1)], axis=2)
        gl = sm[:, :GATE_COLS].reshape(dec_t, dec_b, A_KV, A_G, 3).transpose(1, 3, 2, 0, 4)
        gl = jnp.pad(gl.reshape(dec_b, A_HEADS * dec_t, 3), ((0, 0), (0, 0), (0, LANE - 3)))
        o_s = _attn_sample_call(page_table, q5.reshape(dec_b, A_HEADS * dec_t, LANE), cm_past, to_bt(kvs),
                                cache_sel, cache_win, to_bt(kvw), gl, l,
                                pages_per_step=pages_per_step, dec_t=dec_t)
        o6 = o_s.reshape(dec_b, A_G, A_KV, dec_t, A_KV, A_HD)
        a_out = jnp.stack([o6[:, :, h, :, h] for h in range(A_KV)], axis=1)
        a_out = a_out.transpose(3, 0, 1, 2, 4).reshape(rows_s, A_W).astype(BF16)
        sm_bt = to_bt(sm)
        m_out, st_c, st_n, st_m = _mlstm_call(
            to_bt(mqkv).reshape(dec_b * t_pad, M_QKV_W), sm_bt.reshape(dec_b * t_pad, LANE),
            sm_bt[:, :, GATE_COLS:GATE_COLS + 2 * M_HEADS].transpose(0, 2, 1), to_bt(mo).reshape(dec_b * t_pad, M_W),
            b_if[l], mlstm_norm_g[l], state_mlstm_C[l], state_mlstm_n[l],
            state_mlstm_m[l].reshape(dec_b, 1, M_HEADS), batch=dec_b, seq=t_pad, chunk=t_pad, valid_len=dec_t,
            seqs=SUBLANE)
        m_out = m_out.reshape(dec_b, t_pad, M_W)[:, :dec_t].transpose(1, 0, 2).reshape(rows_s, M_W)
        hist = state_ffn_conv[l].transpose(1, 0, 2).reshape(1, (CONV_W - 1) * dec_b, ff)
        res = _ffn_call(xs, a_out, m_out, wo[l], g1_s, sc2_s, sh2_s, g2_s, n2, wu[l], wc, bc, wd[l],
                        hist, fg if last else None,
                        tm=rows_s, tiles_per_seq=1, shift=dec_b, per_row_mod=True, ch=ch)
        xs, conv_s = res[0], res[1]
        if last:
            y_s = res[2]

        def s_kv(a):
            return _kv_out(a.reshape(dec_t, dec_b, KV_W).transpose(1, 0, 2), (dec_b, dec_t))

        for lst, arr in zip(outs_s, (s_kv(kvc), s_kv(kvs),
                                     kvw.reshape(dec_t, dec_b, 2, A_KV, A_HD).transpose(1, 2, 3, 4, 0),
                                     st_c, st_n, st_m.reshape(dec_b, M_HEADS),
                                     conv_s.reshape(CONV_W - 1, dec_b, ff).transpose(1, 0, 2))):
            lst.append(arr)

    y_prompt = y_p.reshape(batch, seq, d)
    y_sample = y_s.reshape(dec_t, dec_b, d).transpose(1, 0, 2)
    win_len = min(WINDOW, seq)
    kvc_all, kvs_all, kvw_all = kv_stacks
    s_cmp, s_sel, s_win_new, *s_rest = [jnp.stack(a, axis=0) for a in outs_s]
    s_win = jnp.concatenate([cache_win, s_win_new], axis=-1)[..., -wb:].transpose(0, 1, 5, 2, 3, 4)
    return ((y_prompt, y_sample, kv_from_t(kvc_all), kv_from_t(kvs_all), kv_from_t(kvw_all[..., seq - win_len:]))
            + tuple(jnp.stack(a, axis=0) for a in outs_p) + (s_cmp, s_sel, s_win) + tuple(s_rest))
```

```python
import functools

import jax
import jax.numpy as jnp
from jax import lax
from jax.experimental import pallas as pl
from jax.experimental.pallas import tpu as pltpu

F32 = jnp.float32
BF16 = jnp.bfloat16
I32 = jnp.int32

A_HD = 64
A_HEADS = 8
A_KV = 2
A_G = A_HEADS // A_KV
CMP_BLOCK = 64
TOPK = 16
WINDOW = 512
ROPE_THETA = 10000.0
FORCE = float(A_G + 1)
M_HEADS = 4
M_DK = 64
M_DV = 128
CONV_W = 3
EPS = 1e-6
PAGE_SIZE = 128

LANE = 128
SUBLANE = 8
BF16_ROWS = 16
MXU_TILE = 256
VMEM_LIMIT = 56 * 1024 * 1024

A_W = A_HEADS * A_HD
KV_W = 2 * A_KV * A_HD
M_QKV_W = 2 * M_HEADS * M_DK + M_HEADS * M_DV
M_W = M_HEADS * M_DV
OFF_Q = 0
OFF_KC = OFF_Q + A_W
OFF_KS = OFF_KC + KV_W
OFF_KW = OFF_KS + KV_W
OFF_M = OFF_KW + KV_W
OFF_MO = OFF_M + M_QKV_W
OFF_SM = OFF_MO + M_W
IN_W_PAD = OFF_SM + LANE
GATE_COLS = 3 * A_HEADS

LOG2E = 1.4426950408889634
RANK_STEP = 16
SEL_OFF_BIAS = -(2.0 ** 30)
MASK_NEG = -1e30


def _cparams(sem):
    return pltpu.CompilerParams(dimension_semantics=sem, vmem_limit_bytes=VMEM_LIMIT)


def _resident(shape, index_map):
    return pl.BlockSpec(shape, index_map, pipeline_mode=pl.Buffered(1))


def _layer_block(stacked, layer):
    return _resident((1,) + stacked.shape[1:], lambda i: (layer,) + (0,) * (stacked.ndim - 1))


def _mod_kernel(c_ref, w_ref, b_ref, o_ref):
    c = c_ref[...]
    a = (c * jax.nn.sigmoid(c)).astype(BF16)
    o_ref[0] = jnp.dot(a, w_ref[0].astype(BF16), preferred_element_type=F32) + b_ref[0]


def _mod_call(c_all, w_mod, b_mod):
    depth, d, n = w_mod.shape
    rows = c_all.shape[0]
    tn = 1536
    return pl.pallas_call(
        _mod_kernel,
        grid=(depth, n // tn),
        in_specs=[pl.BlockSpec((rows, d), lambda l, j: (0, 0)),
                  pl.BlockSpec((1, d, tn), lambda l, j: (l, 0, j)),
                  pl.BlockSpec((1, 1, tn), lambda l, j: (l, 0, j))],
        out_specs=pl.BlockSpec((1, rows, tn), lambda l, j: (l, 0, j)),
        out_shape=jax.ShapeDtypeStruct((depth, rows, n), F32),
        compiler_params=_cparams(("arbitrary", "arbitrary")),
    )(c_all, w_mod, b_mod.reshape(depth, 1, n))


def _inproj_kernel(x_ref, sc_ref, sh_ref, g_ref, w_ref, cos_ref, sin_ref, *refs, tm, tiles_per_seq, attn_layouts):
    if attn_layouts:
        (q_ref, kvc_ref, kvs_ref, kvw_ref, mqt_ref, mk_ref, mvt_ref, mot_ref, sm_ref, smt_ref,
         ksa_ref, vsb_ref, kwb_ref, vwb_ref, cm_ref) = refs[3:]
    else:
        q_ref, kvc_ref, kvs_ref, kvw_ref, mqkv_ref, mo_ref, sm_ref = refs
    x = x_ref[...]
    xn = x * lax.rsqrt(jnp.mean(x * x, axis=-1, keepdims=True) + EPS) * g_ref[...]
    xn = xn * (1.0 + sc_ref[0]) + sh_ref[0]
    xb = xn.astype(BF16)
    cos = cos_ref[...]
    sin = sin_ref[...]
    lane = lax.broadcasted_iota(I32, (tm, LANE), 1)
    lo_half = (lane % A_HD) < (A_HD // 2)
    first_head = lane < A_HD

    def proj(lo, hi):
        return jnp.dot(xb, w_ref[0, :, lo:hi], preferred_element_type=F32)

    def rope(v):
        rot = jnp.where(lo_half, pltpu.roll(v, LANE - A_HD // 2, 1), pltpu.roll(v, A_HD // 2, 1))
        return v * cos + rot * sin

    qa = proj(OFF_Q, OFF_Q + A_W)
    q_scale = A_HD ** -0.5 * (LOG2E if attn_layouts else 1.0)
    for s in range(A_W // LANE):
        r = rope(qa[:, s * LANE:(s + 1) * LANE]) * q_scale
        if attn_layouts:
            q_ref[s * LANE:(s + 1) * LANE, :] = r.T.astype(BF16)
        else:
            q_ref[:, (2 * s) * LANE:(2 * s + 1) * LANE] = jnp.where(first_head, r, 0.0).astype(BF16)
            q_ref[:, (2 * s + 1) * LANE:(2 * s + 2) * LANE] = jnp.where(
                first_head, pltpu.roll(r, A_HD, 1), 0.0).astype(BF16)

    def kv(lo, out_ref):
        a = proj(lo, lo + KV_W)
        k, v = rope(a[:, :LANE]), a[:, LANE:]
        if attn_layouts:
            k_t, v_t = k.T, v.T
            out_ref[0, 0, :LANE, :] = k_t
            out_ref[0, 0, LANE:, :] = v_t
        else:
            k_t = v_t = None
            out_ref[:, :LANE] = k
            out_ref[:, LANE:] = v
        return k, v, v_t

    kc, vc, _ = kv(OFF_KC, kvc_ref)
    ks, vs, vs_t = kv(OFF_KS, kvs_ref)
    kw, vw, vw_t = kv(OFF_KW, kvw_ref)

    if attn_layouts:
        nblk = tm // CMP_BLOCK
        cm_ref[:, :LANE] = jnp.sum(kc.reshape(nblk, CMP_BLOCK, LANE), axis=1) * (1.0 / CMP_BLOCK)
        cm_ref[:, LANE:] = jnp.sum(vc.reshape(nblk, CMP_BLOCK, LANE), axis=1) * (1.0 / CMP_BLOCK)
        row = lax.broadcasted_iota(I32, (tm, LANE), 0)
        blk = (pl.program_id(0) % tiles_per_seq) * nblk + row // CMP_BLOCK
        onehot = jnp.where(lane - A_HD == blk, 1.0, 0.0)
        ksa_ref[:, :LANE] = jnp.where(first_head, ks, onehot).astype(BF16)
        ksa_ref[:, LANE:] = jnp.where(first_head, pltpu.roll(ks, A_HD, 1), onehot).astype(BF16)
        kwb_ref[:, :LANE] = kw.astype(BF16)
        kwb_ref[:, LANE:] = pltpu.roll(kw, A_HD, 1).astype(BF16)
        for ref, v_t in ((vsb_ref, vs_t), (vwb_ref, vw_t)):
            width = ref.shape[-1]
            for j in range(tm // width):
                ref[0, j] = v_t[:, j * width:(j + 1) * width].astype(BF16)

    m = proj(OFF_M, OFF_MO)
    mo = proj(OFF_MO, OFF_SM)
    nqk = M_HEADS * M_DK
    mk = m[:, nqk:2 * nqk] * (M_DK ** -0.5)
    if attn_layouts:
        for s in range(nqk // LANE):
            mqt_ref[s * LANE:(s + 1) * LANE, :] = m[:, s * LANE:(s + 1) * LANE].T.astype(BF16)
        mk_ref[...] = mk.astype(BF16)
        for s in range(M_W // LANE):
            mvt_ref[s * LANE:(s + 1) * LANE, :] = m[:, 2 * nqk + s * LANE:2 * nqk + (s + 1) * LANE].T
            mot_ref[s * LANE:(s + 1) * LANE, :] = mo[:, s * LANE:(s + 1) * LANE].T
    else:
        mqkv_ref[:, :nqk] = m[:, :nqk]
        mqkv_ref[:, nqk:2 * nqk] = mk
        mqkv_ref[:, 2 * nqk:] = m[:, 2 * nqk:]
        mo_ref[...] = mo
    sm = proj(OFF_SM, IN_W_PAD)
    sm_ref[...] = sm
    if attn_layouts:
        smt_ref[...] = sm.T[GATE_COLS:GATE_COLS + 2 * M_HEADS, :]


def _mod_specs(per_row_mod, tm, d, tiles_per_seq):
    if per_row_mod:
        return pl.BlockSpec((1, tm, d), lambda i: (0, i, 0))
    return pl.BlockSpec((1, 1, d), lambda i: (i // tiles_per_seq, 0, 0))


def _inproj_call(x, sc, sh, g, w, cos, sin, *, tm, tiles_per_seq, per_row_mod, attn_layouts, v_tiles=None,
                 kv_stacks=(), layer=0):
    rows, d = x.shape
    n_tiles = rows // tm
    n_seq = n_tiles // tiles_per_seq
    seq = tm * tiles_per_seq
    mod_spec = _mod_specs(per_row_mod, tm, d, tiles_per_seq)
    if per_row_mod:
        tab_spec = pl.BlockSpec((tm, LANE), lambda i: (i, 0))
    else:
        tab_spec = pl.BlockSpec((tm, LANE), lambda i: (i % tiles_per_seq, 0))

    def rowspec(width):
        return pl.BlockSpec((tm, width), lambda i: (i, 0))

    if attn_layouts:
        kv_t = jax.ShapeDtypeStruct(kv_stacks[0].shape, F32)
        kv_t_spec = pl.BlockSpec((1, 1, KV_W, tm), lambda i: (layer, i // tiles_per_seq, 0, i % tiles_per_seq))
        out_shapes = [jax.ShapeDtypeStruct((A_W, rows), BF16), kv_t, kv_t, kv_t]
        out_specs = [pl.BlockSpec((A_W, tm), lambda i: (0, i)), kv_t_spec, kv_t_spec, kv_t_spec]
    else:
        out_shapes = [jax.ShapeDtypeStruct((rows, 2 * A_W), BF16)] + [jax.ShapeDtypeStruct((rows, KV_W), F32)] * 3
        out_specs = [rowspec(2 * A_W), rowspec(KV_W), rowspec(KV_W), rowspec(KV_W)]
    if attn_layouts:
        def feat_major(n, dtype):
            return jax.ShapeDtypeStruct((n, rows), dtype), pl.BlockSpec((n, tm), lambda i: (0, i))
        nqk = M_HEADS * M_DK
        pairs = [feat_major(nqk, BF16),
                 (jax.ShapeDtypeStruct((rows, nqk), BF16), rowspec(nqk)),
                 feat_major(M_W, F32), feat_major(M_W, F32)]
    else:
        pairs = [(jax.ShapeDtypeStruct((rows, M_QKV_W), F32), rowspec(M_QKV_W)),
                 (jax.ShapeDtypeStruct((rows, M_W), F32), rowspec(M_W))]
    pairs.append((jax.ShapeDtypeStruct((rows, LANE), F32), rowspec(LANE)))
    if attn_layouts:
        pairs.append(feat_major(2 * M_HEADS, F32))
    out_shapes += [p[0] for p in pairs]
    out_specs += [p[1] for p in pairs]
    if attn_layouts:
        def v_tiled(width):
            per_tile = tm // width
            return (jax.ShapeDtypeStruct((n_seq, seq // width, LANE, width), BF16),
                    pl.BlockSpec((1, per_tile, LANE, width),
                                 lambda i: (i // tiles_per_seq, i % tiles_per_seq, 0, 0)))
        (vs_shape, vs_spec), (vw_shape, vw_spec) = v_tiled(v_tiles[0]), v_tiled(v_tiles[1])
        out_shapes += [
            jax.ShapeDtypeStruct((rows, 2 * LANE), BF16),
            vs_shape,
            jax.ShapeDtypeStruct((rows, 2 * LANE), BF16),
            vw_shape,
            jax.ShapeDtypeStruct((rows // CMP_BLOCK, KV_W), F32),
        ]
        out_specs += [rowspec(2 * LANE), vs_spec, rowspec(2 * LANE), vw_spec,
                      pl.BlockSpec((tm // CMP_BLOCK, KV_W), lambda i: (i, 0))]
    return pl.pallas_call(
        functools.partial(_inproj_kernel, tm=tm, tiles_per_seq=tiles_per_seq, attn_layouts=attn_layouts),
        grid=(n_tiles,),
        in_specs=[rowspec(d), mod_spec, mod_spec, pl.BlockSpec((1, d), lambda i: (0, 0)),
                  _layer_block(w, layer), tab_spec, tab_spec]
                 + [pl.BlockSpec(memory_space=pl.ANY)] * len(kv_stacks),
        out_specs=out_specs,
        out_shape=out_shapes,
        input_output_aliases={7 + k: 1 + k for k in range(len(kv_stacks))},
        compiler_params=_cparams(("arbitrary",)),
    )(x, sc, sh, g, w, cos, sin, *kv_stacks)


def _nt(a, b):
    return lax.dot_general(a, b, (((1,), (1,)), ((), ())), preferred_element_type=F32)


def _softmax_update(s, v, carry, v_feature_major=False):
    m, l, acc = carry
    m_new = jnp.maximum(m, jnp.max(s, axis=-1, keepdims=True))
    alpha = jnp.exp(m - m_new)
    p = jnp.exp(s - m_new)
    l = alpha * l + jnp.sum(p, axis=-1, keepdims=True)
    pb = p.astype(BF16)
    pv = _nt(pb, v) if v_feature_major else jnp.dot(pb, v, preferred_element_type=F32)
    return m_new, l, alpha * acc + pv


def _softmax_init(rows, width):
    return (jnp.full((rows, 1), MASK_NEG, F32), jnp.zeros((rows, 1), F32), jnp.zeros((rows, width), F32))


def _masked_softmax(s, ok, axis, base2=False):
    s = jnp.where(ok, s, -jnp.inf)
    smax = jnp.max(s, axis=axis, keepdims=True)
    e = (jnp.exp2 if base2 else jnp.exp)(s - jnp.where(jnp.isfinite(smax), smax, 0.0))
    den = jnp.sum(e, axis=axis, keepdims=True)
    return e / jnp.where(den > 0, den, 1.0)


def _stable_rank_sublanes(score, sub8):
    nb, n = score.shape
    groups = [score[SUBLANE * r:SUBLANE * (r + 1), :] for r in range(nb // SUBLANE)]
    cnts = [jnp.zeros((SUBLANE, n), F32) for _ in groups]
    for ib in range(nb):
        s_i = jnp.broadcast_to(score[ib:ib + 1, :], (SUBLANE, n))
        for r, grp in enumerate(groups):
            if SUBLANE * r > ib:
                one = jnp.where(s_i >= grp, 1.0, 0.0)
            elif SUBLANE * (r + 1) <= ib:
                one = jnp.where(s_i > grp, 1.0, 0.0)
            else:
                one = jnp.where(sub8 > ib - SUBLANE * r, jnp.where(s_i >= grp, 1.0, 0.0),
                                jnp.where(s_i > grp, 1.0, 0.0))
            cnts[r] = cnts[r] + one
    return jnp.concatenate(cnts, axis=0)


def _causal_rank(score, sub8, n_allowed):
    nb, n = score.shape

    def ranked(rows):
        def branch(sc):
            if rows <= TOPK:
                return jnp.zeros((nb, n), F32)
            cnt = _stable_rank_sublanes(sc[:rows], sub8)
            return cnt if rows == nb else jnp.concatenate([cnt, jnp.zeros((nb - rows, n), F32)], axis=0)
        return branch

    sizes = list(range(RANK_STEP, nb, RANK_STEP)) + [nb]
    which = jnp.minimum((n_allowed + RANK_STEP - 1) // RANK_STEP, len(sizes)) - 1
    return lax.switch(which, [ranked(r) for r in sizes], score)


def _softmax_update_t(s_t, v_ext, carry):
    m, acc = carry
    m_new = jnp.maximum(m, jnp.max(s_t, axis=0, keepdims=True))
    p = jnp.exp2(s_t - m_new).astype(BF16)
    acc = jnp.exp2(m - m_new) * acc + jnp.dot(v_ext, p, preferred_element_type=F32)
    return m_new, acc


def _softmax_init_t(n, cols):
    return (jnp.full((1, cols), MASK_NEG, F32), jnp.zeros((n, cols), F32))


def _attn_prompt_kernel(qt_ref, cm_ref, ksa_ref, vst_ref, kwb_ref, vwt_ref, sm_ref, o_ref, *, tq, tk, nb):
    i = pl.program_id(1)
    q0 = i * tq
    cols = A_G * tq

    gates_t = jax.nn.sigmoid(sm_ref[...].T[:GATE_COLS, :])
    cm = cm_ref[0]
    cm_pad = jnp.concatenate([cm[:, LANE:], jnp.zeros((LANE - nb, LANE), F32)], axis=0)
    cm_vt = cm_pad.T.astype(BF16)

    blk_t = lax.broadcasted_iota(I32, (nb, tq), 0)
    qpos_t = q0 + lax.broadcasted_iota(I32, (nb, tq), 1)
    cur_t = qpos_t // CMP_BLOCK
    forced_t = (blk_t == 0) | (blk_t == cur_t)
    allowed_t = blk_t <= cur_t
    blk_t4 = lax.broadcasted_iota(I32, (nb, cols), 0)
    qpos_t4 = q0 + (lax.broadcasted_iota(I32, (nb, cols), 1) & (tq - 1))
    cmp_ok_t4 = (blk_t4 + 1) * CMP_BLOCK <= qpos_t4 + 1
    cg = max(tq, cols // 2)
    n_cg = cols // cg
    kk_k = lax.broadcasted_iota(I32, (tk, cg), 0)
    qq_k = lax.broadcasted_iota(I32, (tk, cg), 1) & (tq - 1)
    ones_k = jnp.ones((BF16_ROWS, tk), BF16)
    ones_q = jnp.ones((BF16_ROWS, tq), BF16)
    kk_q = lax.broadcasted_iota(I32, (tq, cg), 0)
    qq_q = lax.broadcasted_iota(I32, (tq, cg), 1) & (tq - 1)
    sub8 = lax.broadcasted_iota(I32, (SUBLANE, tq), 0)
    zeros_hd = jnp.zeros((A_HD, tq), BF16)

    def stacked(q_heads, lower):
        return jnp.concatenate([jnp.concatenate([qg, lower], axis=0) for qg in q_heads], axis=1)

    q0t, q_aug, o_cmp = [], [], []
    for h in range(A_KV):
        q_heads = [qt_ref[(A_G * h + g) * A_HD:(A_G * h + g + 1) * A_HD, :] for g in range(A_G)]
        q0t.append(stacked(q_heads, zeros_hd))
        cm_k = cm[:, :LANE] if h == 0 else pltpu.roll(cm[:, :LANE], A_HD, 1)
        p_t = _masked_softmax(jnp.dot(cm_k.astype(BF16), q0t[h], preferred_element_type=F32), cmp_ok_t4, 0,
                              base2=True)
        p_pad = jnp.concatenate([p_t, jnp.zeros((LANE - nb, cols), F32)], axis=0).astype(BF16)
        o_cmp.append(jnp.dot(cm_vt[h * A_HD:(h + 1) * A_HD], p_pad, preferred_element_type=F32))

        imp = p_t[:, 0:tq]
        for g in range(1, A_G):
            imp = imp + p_t[:, g * tq:(g + 1) * tq]
        score = jnp.where(forced_t, FORCE, jnp.where(allowed_t, imp, -1.0))
        cnt = _causal_rank(score, sub8, (q0 + tq - 1) // CMP_BLOCK + 1)
        bias_t = jnp.where(cnt < TOPK, jnp.where(score >= 0, 0.0, SEL_OFF_BIAS), SEL_OFF_BIAS)
        if nb < A_HD:
            bias_t = jnp.concatenate([bias_t, jnp.zeros((A_HD - nb, tq), F32)], axis=0)
        q_aug.append(stacked(q_heads, bias_t.astype(BF16)))

    def sel_scores(kt, masked):
        k0 = pl.multiple_of(kt * tk, tk)
        scores = []
        for h in range(A_KV):
            for c in range(n_cg):
                s = jnp.dot(ksa_ref[0, pl.ds(k0, tk), h * LANE:(h + 1) * LANE],
                            q_aug[h][:, c * cg:(c + 1) * cg], preferred_element_type=F32)
                if masked:
                    s = jnp.where(k0 + kk_k <= q0 + qq_k, s, MASK_NEG)
                scores.append(s)
        return scores

    def sel_update(kt, scores, carries):
        out = []
        for h in range(A_KV):
            v_ext = jnp.concatenate([vst_ref[0, kt, h * A_HD:(h + 1) * A_HD, :], ones_k], axis=0)
            for c in range(n_cg):
                out.append(_softmax_update_t(scores[h * n_cg + c], v_ext, carries[h * n_cg + c]))
        return tuple(out)

    def sel_pair(kp, carries):
        s_a, s_b = sel_scores(2 * kp, False), sel_scores(2 * kp + 1, False)
        return sel_update(2 * kp + 1, s_b, sel_update(2 * kp, s_a, carries))

    n_full = q0 // tk
    carries = lax.fori_loop(0, n_full // 2, sel_pair,
                            tuple(_softmax_init_t(A_HD + BF16_ROWS, cg) for _ in range(A_KV * n_cg)))
    carries = lax.fori_loop(n_full - n_full % 2, n_full,
                            lambda kt, c: sel_update(kt, sel_scores(kt, False), c), carries)

    s_diag = sel_scores(n_full, True)
    n_wt = WINDOW // tq + 1
    tiles = [jnp.maximum(i - wt, 0) for wt in range(n_wt)]
    win_scores = []
    for h in range(A_KV):
        for c in range(n_cg):
            per_tile = []
            for wt in range(n_wt):
                startc = pl.multiple_of(tiles[wt] * tq, tq)
                s = jnp.dot(kwb_ref[0, pl.ds(startc, tq), h * LANE:(h + 1) * LANE],
                            q0t[h][:, c * cg:(c + 1) * cg], preferred_element_type=F32)
                if wt == 0:
                    s = jnp.where(kk_q <= qq_q, s, MASK_NEG)
                else:
                    if wt == n_wt - 1:
                        s = jnp.where(kk_q > qq_q, s, MASK_NEG)
                    s = s + jnp.where(q0 - wt * tq >= 0, 0.0, MASK_NEG)
                per_tile.append(s)
            win_scores.append(per_tile)

    carries = sel_update(n_full, s_diag, carries)
    o_sel = [jnp.concatenate([acc[:A_HD] / acc[A_HD:A_HD + 1] for _, acc in carries[h * n_cg:(h + 1) * n_cg]], axis=1)
             for h in range(A_KV)]

    o_win = []
    for h in range(A_KV):
        v_ext = [jnp.concatenate([vwt_ref[0, tile, h * A_HD:(h + 1) * A_HD, :], ones_q], axis=0) for tile in tiles]
        outs = []
        for c in range(n_cg):
            per_tile = win_scores[h * n_cg + c]
            m = functools.reduce(jnp.maximum, [jnp.max(s, axis=0, keepdims=True) for s in per_tile])
            acc = functools.reduce(jnp.add, [jnp.dot(v, jnp.exp2(s - m).astype(BF16), preferred_element_type=F32)
                                             for v, s in zip(v_ext, per_tile)])
            outs.append(acc[:A_HD] / acc[A_HD:A_HD + 1])
        o_win.append(jnp.concatenate(outs, axis=1))

    mixes = []
    for h in range(A_KV):
        for g in range(A_G):
            head = A_G * h + g
            sl = (slice(None), slice(g * tq, (g + 1) * tq))
            mixes.append(gates_t[3 * head:3 * head + 1, :] * o_cmp[h][sl] + gates_t[3 * head + 1:3 * head + 2, :] * o_sel[h][sl]
                         + gates_t[3 * head + 2:3 * head + 3, :] * o_win[h][sl])
    o_ref[...] = jnp.concatenate(mixes, axis=0).T.astype(o_ref.dtype)


def _attn_prompt_call(qt, cm, ksa, vst, kwb, vwt, sm, *, batch, seq, tq, tk):
    nb = seq // CMP_BLOCK
    tiles = seq // tq
    assert nb <= A_HD and tq % LANE == 0 and tk % tq == 0 and seq % tk == 0 and WINDOW % tq == 0

    def full(width):
        return pl.BlockSpec((1, seq, width), lambda b, i: (b, 0, 0))

    def full_t(arr):
        return pl.BlockSpec((1,) + arr.shape[1:], lambda b, i: (b, 0, 0, 0))

    return pl.pallas_call(
        functools.partial(_attn_prompt_kernel, tq=tq, tk=tk, nb=nb),
        grid=(batch, tiles),
        in_specs=[pl.BlockSpec((A_W, tq), lambda b, i: (0, b * tiles + i)),
                  pl.BlockSpec((1, nb, KV_W), lambda b, i: (b, 0, 0)),
                  full(2 * LANE), full_t(vst), full(2 * LANE), full_t(vwt),
                  pl.BlockSpec((tq, LANE), lambda b, i: (b * tiles + i, 0))],
        out_specs=pl.BlockSpec((tq, A_W), lambda b, i: (b * tiles + i, 0)),
        out_shape=jax.ShapeDtypeStruct((batch * seq, A_W), BF16),
        compiler_params=_cparams(("arbitrary", "arbitrary")),
    )(qt, cm.reshape(batch, nb, KV_W), ksa.reshape(batch, seq, 2 * LANE), vst, kwb.reshape(batch, seq, 2 * LANE),
      vwt, sm)


def _page_spec(k, pages_per_step, layer):
    def index(b, j, pt):
        return (layer, pt[b, j * pages_per_step + k], 0, 0, 0, 0)
    return pl.BlockSpec((1, 1, 2, A_KV, A_HD, PAGE_SIZE), index)


def _page_kv(ref):
    n_pos = ref.shape[-1]
    return ref[0, 0, 0].reshape(A_KV * A_HD, n_pos), ref[0, 0, 1].reshape(A_KV * A_HD, n_pos)


def _cmp_means_kernel(pt_ref, *refs, n_pages):
    del pt_ref
    o_ref = refs[n_pages]
    per_page = PAGE_SIZE // CMP_BLOCK
    nbp = n_pages * per_page
    pos_blk = lax.broadcasted_iota(I32, (PAGE_SIZE, nbp), 0) // CMP_BLOCK
    blk = lax.broadcasted_iota(I32, (PAGE_SIZE, nbp), 1)
    acc = jnp.zeros((KV_W, nbp), F32)
    for k in range(n_pages):
        page = refs[k][0, 0].reshape(KV_W, PAGE_SIZE)
        place = jnp.where(blk == per_page * k + pos_blk, 1.0, 0.0).astype(BF16)
        for part in _split_bf16(page, 2):
            acc = acc + jnp.dot(part, place, preferred_element_type=F32)
    o_ref[0] = acc * (1.0 / CMP_BLOCK)


def _cmp_means_call(cache, page_table, layer):
    dec_b, n_pages = page_table.shape
    nbp = n_pages * (PAGE_SIZE // CMP_BLOCK)
    grid_spec = pltpu.PrefetchScalarGridSpec(
        num_scalar_prefetch=1,
        grid=(dec_b, 1),
        in_specs=[_page_spec(k, n_pages, layer) for k in range(n_pages)],
        out_specs=pl.BlockSpec((1, KV_W, nbp), lambda b, j, pt: (b, 0, 0)),
    )
    return pl.pallas_call(
        functools.partial(_cmp_means_kernel, n_pages=n_pages),
        grid_spec=grid_spec,
        out_shape=jax.ShapeDtypeStruct((dec_b, KV_W, nbp), F32),
        compiler_params=_cparams(("arbitrary", "arbitrary")),
    )(page_table, *([cache] * n_pages))


def _attn_sample_kernel(pt_ref, q_ref, cm_ref, kn_ref, win_ref, wn_ref, gl_ref, *refs,
                        pages_per_step, dec_t, past_len, nbp):
    del pt_ref
    page_refs = refs[:pages_per_step]
    o_ref = refs[pages_per_step]
    m_scr, l_scr, acc_scr, bias_scr, expand_scr, ocmp_scr = refs[pages_per_step + 1:]
    j = pl.program_id(1)
    rows = q_ref.shape[1]
    grp = A_KV * dec_t
    tk = pages_per_step * PAGE_SIZE
    blocks_per_step = tk // CMP_BLOCK
    qb = q_ref[0]
    t_row = lax.broadcasted_iota(I32, (rows, 1), 0) % dec_t

    @pl.when(j == 0)
    def _():
        cm_t = cm_ref[0]
        blk = lax.broadcasted_iota(I32, (rows, nbp), 1)
        ok = (blk + 1) * CMP_BLOCK <= past_len + t_row + 1
        p = _masked_softmax(jnp.dot(qb, cm_t[:LANE].astype(BF16), preferred_element_type=F32), ok, 1)
        ocmp_scr[...] = _nt(p.astype(BF16), cm_t[LANE:].astype(BF16))
        imp = p[0:grp]
        for g in range(1, A_G):
            imp = imp + p[g * grp:(g + 1) * grp]
        blk8 = lax.broadcasted_iota(I32, (grp, nbp), 1)
        cur = (past_len + lax.broadcasted_iota(I32, (grp, nbp), 0) % dec_t) // CMP_BLOCK
        score = jnp.where(blk8 == 0, FORCE, jnp.where(blk8 == cur, FORCE, jnp.where(blk8 <= cur, imp, -1.0)))
        cnt = jnp.where(FORCE > score, 1.0, 0.0)
        for ib in range(nbp):
            c_i = score[:, ib:ib + 1]
            cnt = cnt + jnp.where(blk8 > ib, jnp.where(c_i >= score, 1.0, 0.0), jnp.where(c_i > score, 1.0, 0.0))
        bias = jnp.where(cnt < TOPK, jnp.where(score >= 0, 0.0, SEL_OFF_BIAS), SEL_OFF_BIAS)
        bias = jnp.concatenate([bias] * A_G, axis=0)
        for jj in range(bias_scr.shape[0]):
            bias_scr[jj] = pltpu.roll(bias, (nbp - jj * blocks_per_step) % nbp, 1).astype(BF16)
        blk_e = lax.broadcasted_iota(I32, (nbp, tk), 0)
        key_blk = lax.broadcasted_iota(I32, (nbp, tk), 1) // CMP_BLOCK
        expand_scr[...] = jnp.where(blk_e == key_blk, 1.0, 0.0).astype(BF16)
        m_scr[...] = jnp.full(m_scr.shape, MASK_NEG, F32)
        l_scr[...] = jnp.zeros(l_scr.shape, F32)
        acc_scr[...] = jnp.zeros(acc_scr.shape, F32)

    pages = [_page_kv(r) for r in page_refs]
    keys_t = jnp.concatenate([k.astype(BF16) for k, _ in pages], axis=1)
    vals_t = jnp.concatenate([v.astype(BF16) for _, v in pages], axis=1)
    s = (jnp.dot(qb, keys_t, preferred_element_type=F32)
         + jnp.dot(bias_scr[j], expand_scr[...], preferred_element_type=F32))
    m, l, acc = _softmax_update(s, vals_t, (m_scr[...], l_scr[...], acc_scr[...]), v_feature_major=True)
    m_scr[...] = m
    l_scr[...] = l
    acc_scr[...] = acc

    @pl.when(j == pl.num_programs(1) - 1)
    def _():
        new_ok = lax.broadcasted_iota(I32, (rows, kn_ref.shape[1]), 1) <= t_row

        def new_rows(carry, ref):
            kv = ref[0]
            s_n = jnp.where(new_ok, _nt(qb, kv[:, :LANE].astype(BF16)), MASK_NEG)
            return _softmax_update(s_n, kv[:, LANE:].astype(BF16), carry)

        _, l_s, acc_s = new_rows((m_scr[...], l_scr[...], acc_scr[...]), kn_ref)
        o_sel = acc_s / l_s

        kw_t, vw_t = _page_kv(win_ref)
        wb = kw_t.shape[1]
        jw = lax.broadcasted_iota(I32, (rows, wb), 1)
        s_w = jnp.dot(qb, kw_t.astype(BF16), preferred_element_type=F32)
        s_w = jnp.where(jw > t_row + (wb - WINDOW), s_w, MASK_NEG)
        carry = _softmax_update(s_w, vw_t.astype(BF16), _softmax_init(rows, LANE), v_feature_major=True)
        _, l_w, acc_w = new_rows(carry, wn_ref)
        o_win = acc_w / l_w

        gate = jax.nn.sigmoid(gl_ref[0])
        o_ref[0] = gate[:, 0:1] * ocmp_scr[...] + gate[:, 1:2] * o_sel + gate[:, 2:3] * o_win


def _attn_sample_call(page_table, q, cm, kv_new, cache_sel, cache_win, win_new, gate_logits, layer, *,
                      pages_per_step, dec_t):
    dec_b, n_pages = page_table.shape
    rows = q.shape[1]
    nbp = cm.shape[2]
    wb = cache_win.shape[-1]
    past_len = n_pages * PAGE_SIZE
    assert nbp == LANE and past_len >= wb and wb >= WINDOW

    def per_b(shape):
        return pl.BlockSpec((1,) + shape, lambda b, j, pt: (b, 0, 0))

    grid_spec = pltpu.PrefetchScalarGridSpec(
        num_scalar_prefetch=1,
        grid=(dec_b, n_pages // pages_per_step),
        in_specs=[per_b((rows, LANE)), per_b((KV_W, nbp)), per_b(kv_new.shape[1:]),
                  pl.BlockSpec((1, 1, 2, A_KV, A_HD, wb), lambda b, j, pt: (layer, b, 0, 0, 0, 0)),
                  per_b(win_new.shape[1:]), per_b((rows, LANE))]
                 + [_page_spec(k, pages_per_step, layer) for k in range(pages_per_step)],
        out_specs=per_b((rows, LANE)),
        scratch_shapes=[pltpu.VMEM((rows, 1), F32), pltpu.VMEM((rows, 1), F32), pltpu.VMEM((rows, LANE), F32),
                        pltpu.VMEM((n_pages // pages_per_step, rows, nbp), BF16),
                        pltpu.VMEM((nbp, pages_per_step * PAGE_SIZE), BF16), pltpu.VMEM((rows, LANE), F32)],
    )
    return pl.pallas_call(
        functools.partial(_attn_sample_kernel, pages_per_step=pages_per_step, dec_t=dec_t,
                          past_len=past_len, nbp=nbp),
        grid_spec=grid_spec,
        out_shape=jax.ShapeDtypeStruct((dec_b, rows, LANE), F32),
        compiler_params=_cparams(("arbitrary", "arbitrary")),
    )(page_table, q, cm, kv_new, cache_win, win_new, gate_logits, *([cache_sel] * pages_per_step))


def _split_bf16(a, terms):
    parts = []
    for _ in range(terms):
        piece = a.astype(BF16)
        parts.append(piece)
        a = a - piece.astype(F32)
    return parts


def _log_sigmoid(x):
    return jnp.minimum(x, 0.0) - jnp.log1p(jnp.exp(-jnp.abs(x)))


def _mlstm_kernel(qkv_ref, sm_ref, smt_ref, mo_ref, bc_ref, br_ref, g_ref, c0_ref, n0_ref, m0_ref,
                  o_ref, c_ref, n_ref, m_ref, *, chunk, valid_len, seqs):
    @pl.when(pl.program_id(1) == 0)
    def _():
        c_ref[...] = c0_ref[...]
        n_ref[...] = n0_ref[...]
        m_ref[...] = m0_ref[...]

    ng = 2 * M_HEADS
    t_i = lax.broadcasted_iota(I32, (chunk, chunk), 0)
    s_i = lax.broadcasted_iota(I32, (chunk, chunk), 1)
    causal = s_i <= t_i
    tri = jnp.where(causal, 1.0, 0.0).astype(BF16)
    tri_t = jnp.where(t_i <= s_i, 1.0, 0.0).astype(BF16)
    for sq in range(seqs):
        _mlstm_sequence(sq, slice(sq * chunk, (sq + 1) * chunk), qkv_ref, sm_ref, smt_ref, mo_ref, bc_ref, br_ref,
                        g_ref, o_ref, c_ref, n_ref, m_ref, causal, tri, tri_t, chunk, valid_len)


def _mlstm_sequence(sq, rows, qkv_ref, sm_ref, smt_ref, mo_ref, bc_ref, br_ref, g_ref, o_ref, c_ref, n_ref, m_ref,
                    causal, tri, tri_t, chunk, valid_len):
    ng = 2 * M_HEADS
    gate_c = sm_ref[rows, GATE_COLS:GATE_COLS + ng] + bc_ref[...]
    gate_r = smt_ref[sq] + br_ref[...]
    lf_c = _log_sigmoid(gate_c)
    lf_r = _log_sigmoid(gate_r)
    ig_c, ig_r = gate_c, gate_r
    if valid_len < chunk:
        tc = lax.broadcasted_iota(I32, (chunk, ng), 0)
        tr = lax.broadcasted_iota(I32, (ng, chunk), 1)
        lf_c = jnp.where(tc < valid_len, lf_c, 0.0)
        lf_r = jnp.where(tr < valid_len, lf_r, 0.0)
        ig_c = jnp.where(tc < valid_len, ig_c, MASK_NEG)
        ig_r = jnp.where(tr < valid_len, ig_r, MASK_NEG)
    b_c = sum(jnp.dot(tri, part, preferred_element_type=F32) for part in _split_bf16(lf_c, 3))
    b_r = sum(jnp.dot(part, tri_t, preferred_element_type=F32) for part in _split_bf16(lf_r, 3))

    nqk = M_HEADS * M_DK
    for n in range(M_HEADS):
        q = qkv_ref[rows, n * M_DK:(n + 1) * M_DK]
        k = qkv_ref[rows, nqk + n * M_DK:nqk + (n + 1) * M_DK]
        v = qkv_ref[rows, 2 * nqk + n * M_DV:2 * nqk + (n + 1) * M_DV]
        qb, kb, vb = q.astype(BF16), k.astype(BF16), v.astype(BF16)
        fcol = M_HEADS + n
        bc = b_c[:, fcol:fcol + 1]
        br = b_r[fcol:fcol + 1, :]
        igc = ig_c[:, n:n + 1]
        igr = ig_r[n:n + 1, :]
        b_end = bc[chunk - 1:chunk, :]
        m_prev = m_ref[sq, :, n:n + 1]
        c_prev = c_ref[sq, n]
        n_prev = n_ref[sq, n:n + 1, :]

        a = bc + m_prev
        d = jnp.where(causal, bc + (igr - br), -jnp.inf)
        m_t = jnp.maximum(a, jnp.max(d, axis=1, keepdims=True))
        w = _nt(qb, kb) * jnp.exp(d - m_t)
        aw = jnp.exp(a - m_t)
        num = (jnp.dot(w.astype(BF16), vb, preferred_element_type=F32)
               + aw * jnp.dot(qb, c_prev.astype(BF16), preferred_element_type=F32))
        den = jnp.sum(w, axis=1, keepdims=True) + aw * jnp.sum(q * n_prev, axis=1, keepdims=True)
        hcell = num / jnp.maximum(jnp.abs(den), jnp.exp(-m_t))

        hn = hcell * lax.rsqrt(jnp.mean(hcell * hcell, axis=-1, keepdims=True) + EPS) * g_ref[:, n * M_DV:(n + 1) * M_DV]
        o_ref[rows, n * M_DV:(n + 1) * M_DV] = (
            jax.nn.sigmoid(mo_ref[rows, n * M_DV:(n + 1) * M_DV]) * hn).astype(o_ref.dtype)

        wl = b_end - bc + igc
        m_new = jnp.maximum(b_end + m_prev, jnp.max(wl, axis=0, keepdims=True))
        decay = jnp.exp(b_end + m_prev - m_new)
        kws = k * jnp.exp(wl - m_new)
        c_ref[sq, n] = decay * c_prev + lax.dot_general(kws.astype(BF16), vb, (((0,), (0,)), ((), ())),
                                                        preferred_element_type=F32)
        n_ref[sq, n:n + 1, :] = decay * n_prev + jnp.sum(kws, axis=0, keepdims=True)
        m_ref[sq, :, n:n + 1] = m_new


def _mlstm_call(qkv, sm, smt, mo, b_if, norm_g, c0, n0, m0, *, batch, seq, chunk, valid_len, seqs=1):
    nc = seq // chunk
    ng = 2 * M_HEADS
    assert batch % seqs == 0 and (seqs == 1 or nc == 1)

    def rowspec(width):
        return pl.BlockSpec((seqs * chunk, width), lambda b, c: (b * nc + c, 0))

    def const(shape):
        return pl.BlockSpec(shape, lambda b, c: (0,) * len(shape))

    state_specs = [pl.BlockSpec((seqs, M_HEADS, M_DK, M_DV), lambda b, c: (b, 0, 0, 0)),
                   pl.BlockSpec((seqs, M_HEADS, M_DK), lambda b, c: (b, 0, 0)),
                   pl.BlockSpec((seqs, 1, M_HEADS), lambda b, c: (b, 0, 0))]
    return pl.pallas_call(
        functools.partial(_mlstm_kernel, chunk=chunk, valid_len=valid_len, seqs=seqs),
        grid=(batch // seqs, nc),
        in_specs=[rowspec(M_QKV_W), rowspec(LANE), pl.BlockSpec((seqs, ng, chunk), lambda b, c: (b, 0, c)),
                  rowspec(M_W), const((1, ng)), const((ng, 1)), const((1, M_W))] + state_specs,
        out_specs=[rowspec(M_W)] + state_specs,
        out_shape=[jax.ShapeDtypeStruct((batch * seq, M_W), BF16),
                   jax.ShapeDtypeStruct((batch, M_HEADS, M_DK, M_DV), F32),
                   jax.ShapeDtypeStruct((batch, M_HEADS, M_DK), F32),
                   jax.ShapeDtypeStruct((batch, 1, M_HEADS), F32)],
        compiler_params=_cparams(("arbitrary", "arbitrary")),
    )(qkv, sm, smt, mo, b_if.reshape(1, ng), b_if.reshape(ng, 1), norm_g.reshape(1, M_W), c0, n0, m0)


def _mlstm_t_kernel(qt_ref, k_ref, vt_ref, mot_ref, sm_ref, smt_ref, bc_ref, br_ref, g_ref, o_ref, ct_ref, n_ref, m_ref,
                    *, chunk):
    @pl.when(pl.program_id(1) == 0)
    def _():
        ct_ref[...] = jnp.zeros(ct_ref.shape, F32)
        n_ref[...] = jnp.zeros(n_ref.shape, F32)
        m_ref[...] = jnp.zeros(m_ref.shape, F32)

    ng = 2 * M_HEADS
    gate_c = sm_ref[:, GATE_COLS:GATE_COLS + ng] + bc_ref[...]
    gate_r = smt_ref[...] + br_ref[...]
    s_i = lax.broadcasted_iota(I32, (chunk, chunk), 0)
    t_i = lax.broadcasted_iota(I32, (chunk, chunk), 1)
    causal = s_i <= t_i
    tri_c = jnp.where(t_i <= s_i, 1.0, 0.0).astype(BF16)
    tri_r = jnp.where(causal, 1.0, 0.0).astype(BF16)
    b_c = sum(jnp.dot(tri_c, part, preferred_element_type=F32) for part in _split_bf16(_log_sigmoid(gate_c), 3))
    b_r = sum(jnp.dot(part, tri_r, preferred_element_type=F32) for part in _split_bf16(_log_sigmoid(gate_r), 3))
    r_c = gate_c[:, :M_HEADS] - b_c[:, M_HEADS:]
    g_full = jnp.concatenate([g_ref[...]] * (chunk // LANE), axis=1)

    outs = []
    for n in range(M_HEADS):
        qt = qt_ref[n * M_DK:(n + 1) * M_DK, :]
        kb = k_ref[:, n * M_DK:(n + 1) * M_DK]
        vt = vt_ref[n * M_DV:(n + 1) * M_DV, :]
        b_row = b_r[M_HEADS + n:M_HEADS + n + 1, :]
        ig_row = gate_r[n:n + 1, :]
        b_end = b_c[chunk - 1:chunk, M_HEADS + n:M_HEADS + n + 1]
        m_prev = m_ref[0, :, n:n + 1]
        ct_prev = ct_ref[0, n]
        n_prev = n_ref[0, n:n + 1, :]

        a = b_row + m_prev
        d_t = jnp.where(causal, r_c[:, n:n + 1] + b_row, -jnp.inf)
        m_t = jnp.maximum(a, jnp.max(d_t, axis=0, keepdims=True))
        w_t = jnp.dot(kb, qt, preferred_element_type=F32) * jnp.exp(d_t - m_t)
        aw = jnp.exp(a - m_t)
        num = (jnp.dot(vt.astype(BF16), w_t.astype(BF16), preferred_element_type=F32)
               + aw * jnp.dot(ct_prev.astype(BF16), qt, preferred_element_type=F32))
        den = (jnp.sum(w_t, axis=0, keepdims=True)
               + aw * jnp.dot(n_prev.astype(BF16), qt, preferred_element_type=F32))
        hcell = num / jnp.maximum(jnp.abs(den), jnp.exp(-m_t))
        hn = (hcell * lax.rsqrt(jnp.mean(hcell * hcell, axis=0, keepdims=True) + EPS)
              * g_full[n * M_DV:(n + 1) * M_DV, :])
        outs.append(jax.nn.sigmoid(mot_ref[n * M_DV:(n + 1) * M_DV, :]) * hn)

        wl = b_end - b_row + ig_row
        m_new = jnp.maximum(b_end + m_prev, jnp.max(wl, axis=1, keepdims=True))
        decay = jnp.exp(b_end + m_prev - m_new)
        ws = jnp.exp(wl - m_new)
        ct_ref[0, n] = decay * ct_prev + jnp.dot((vt * ws).astype(BF16), kb, preferred_element_type=F32)
        n_ref[0, n:n + 1, :] = decay * n_prev + jnp.dot(ws.astype(BF16), kb, preferred_element_type=F32)
        m_ref[0, :, n:n + 1] = m_new
    o_ref[...] = jnp.concatenate(outs, axis=0).T.astype(o_ref.dtype)


def _mlstm_t_call(qt, k, vt, mot, sm, smt, b_if, norm_g, *, batch, seq, chunk):
    nc = seq // chunk
    ng = 2 * M_HEADS
    nqk = M_HEADS * M_DK
    assert chunk % LANE == 0

    def colspec(n):
        return pl.BlockSpec((n, chunk), lambda b, c: (0, b * nc + c))

    def rowspec(width):
        return pl.BlockSpec((chunk, width), lambda b, c: (b * nc + c, 0))

    def const(shape):
        return pl.BlockSpec(shape, lambda b, c: (0,) * len(shape))

    state_specs = [pl.BlockSpec((1, M_HEADS, M_DV, M_DK), lambda b, c: (b, 0, 0, 0)),
                   pl.BlockSpec((1, M_HEADS, M_DK), lambda b, c: (b, 0, 0)),
                   pl.BlockSpec((1, 1, M_HEADS), lambda b, c: (b, 0, 0))]
    return pl.pallas_call(
        functools.partial(_mlstm_t_kernel, chunk=chunk),
        grid=(batch, nc),
        in_specs=[colspec(nqk), rowspec(nqk), colspec(M_W), colspec(M_W), rowspec(LANE), colspec(ng),
                  const((1, ng)), const((ng, 1)), const((M_W, LANE))],
        out_specs=[rowspec(M_W)] + state_specs,
        out_shape=[jax.ShapeDtypeStruct((batch * seq, M_W), BF16),
                   jax.ShapeDtypeStruct((batch, M_HEADS, M_DV, M_DK), F32),
                   jax.ShapeDtypeStruct((batch, M_HEADS, M_DK), F32),
                   jax.ShapeDtypeStruct((batch, 1, M_HEADS), F32)],
        compiler_params=_cparams(("arbitrary", "arbitrary")),
    )(qt, k, vt, mot, sm, smt, b_if.reshape(1, ng), b_if.reshape(ng, 1),
      jnp.broadcast_to(norm_g.reshape(M_W, 1), (M_W, LANE)))


def _ffn_kernel(x_ref, a_ref, mh_ref, wo_ref, g1_ref, sc_ref, sh_ref, g2_ref, n2_ref, wu_ref, wc_ref, bc_ref,
                wd_ref, hist_ref, *refs, tm, tiles_per_seq, shift, ff, ch, final):
    if final:
        fg_ref, xo_ref, cs_ref, y_ref, carry_scr, up_scr = refs
    else:
        xo_ref, cs_ref, carry_scr, up_scr = refs
    hist_rows = (CONV_W - 1) * shift
    hoff = up_scr.shape[0] - tm

    @pl.when(pl.program_id(0) % tiles_per_seq == 0)
    def _():
        carry_scr[...] = hist_ref[0]

    y = (jnp.dot(a_ref[...], wo_ref[0, :A_W, :], preferred_element_type=F32)
         + jnp.dot(mh_ref[...], wo_ref[0, A_W:, :], preferred_element_type=F32))
    x1 = x_ref[...] + g1_ref[0] * y
    xn = x1 * lax.rsqrt(jnp.mean(x1 * x1, axis=-1, keepdims=True) + EPS) * n2_ref[...]
    xb = (xn * (1.0 + sc_ref[0]) + sh_ref[0]).astype(BF16)

    acc = jnp.zeros(x1.shape, F32)
    for lo, hi in ch:
        w = hi - lo
        u = jnp.dot(xb, wu_ref[0, :, lo:hi], preferred_element_type=F32)
        gt = jnp.dot(xb, wu_ref[0, :, ff + lo:ff + hi], preferred_element_type=F32)
        up_scr[hoff - hist_rows:hoff, :w] = carry_scr[:, lo:hi]
        up_scr[hoff:, :w] = u
        tail = u[tm - hist_rows:, :]
        carry_scr[:, lo:hi] = tail
        cs_ref[0, :, lo:hi] = tail
        conv = bc_ref[:, lo:hi]
        for jj in range(CONV_W - 1):
            start = hoff - (CONV_W - 1 - jj) * shift
            conv = conv + wc_ref[jj:jj + 1, lo:hi] * up_scr[start:start + tm, :w]
        conv = conv + wc_ref[CONV_W - 1:CONV_W, lo:hi] * u
        hid = conv * jax.nn.sigmoid(conv) * gt
        acc = acc + jnp.dot(hid.astype(BF16), wd_ref[0, lo:hi, :], preferred_element_type=F32)

    x2 = x1 + g2_ref[0] * acc
    xo_ref[...] = x2
    if final:
        y_ref[...] = x2 * lax.rsqrt(jnp.mean(x2 * x2, axis=-1, keepdims=True) + EPS) * fg_ref[...]


def _ffn_call(x, a_out, m_out, wo, g1, sc, sh, g2, n2, wu, wc, bc, wd, hist, final_g, *,
              tm, tiles_per_seq, shift, per_row_mod, ch, layer):
    rows, d = x.shape
    ff = wd.shape[1]
    groups, hist_rows, _ = hist.shape
    assert hist_rows == (CONV_W - 1) * shift and tm >= hist_rows
    assert ch[0][0] == 0 and ch[-1][1] == ff and all(a[1] == b[0] for a, b in zip(ch, ch[1:]))
    final = final_g is not None
    hoff = -(-hist_rows // SUBLANE) * SUBLANE
    mod_spec = _mod_specs(per_row_mod, tm, d, tiles_per_seq)

    def rowspec(width):
        return pl.BlockSpec((tm, width), lambda i: (i, 0))

    def const(arr):
        return _resident(arr.shape, lambda i: (0,) * arr.ndim)

    hist_spec = pl.BlockSpec((1, hist_rows, ff), lambda i: (i // tiles_per_seq, 0, 0))
    in_specs = [rowspec(d), rowspec(A_W), rowspec(M_W), _layer_block(wo, layer), mod_spec, mod_spec, mod_spec, mod_spec,
                pl.BlockSpec((1, d), lambda i: (0, 0)), _layer_block(wu, layer), const(wc), const(bc),
                _layer_block(wd, layer), hist_spec]
    args = [x, a_out, m_out, wo, g1, sc, sh, g2, n2, wu, wc, bc, wd, hist]
    out_specs = [rowspec(d), hist_spec]
    out_shape = [jax.ShapeDtypeStruct((rows, d), F32), jax.ShapeDtypeStruct(hist.shape, F32)]
    if final:
        in_specs.append(pl.BlockSpec((1, d), lambda i: (0, 0)))
        args.append(final_g)
        out_specs.append(rowspec(d))
        out_shape.append(jax.ShapeDtypeStruct((rows, d), F32))
    return pl.pallas_call(
        functools.partial(_ffn_kernel, tm=tm, tiles_per_seq=tiles_per_seq, shift=shift, ff=ff, ch=ch, final=final),
        grid=(rows // tm,),
        in_specs=in_specs,
        out_specs=out_specs,
        out_shape=out_shape,
        scratch_shapes=[pltpu.VMEM((hist_rows, ff), F32),
                        pltpu.VMEM((hoff + tm, max(hi - lo for lo, hi in ch)), F32)],
        compiler_params=_cparams(("arbitrary",)),
    )(*args)


def _prep_w_in(w_in):
    attn_end = OFF_M
    mlstm_lo = attn_end + GATE_COLS
    mlstm_hi = mlstm_lo + M_QKV_W + M_W
    assert mlstm_hi + 2 * M_HEADS == w_in.shape[-1]
    w = w_in.astype(BF16)
    pad = jnp.zeros(w.shape[:-1] + (LANE - GATE_COLS - 2 * M_HEADS,), BF16)
    return jnp.concatenate([w[..., :attn_end], w[..., mlstm_lo:mlstm_hi], w[..., attn_end:mlstm_lo],
                            w[..., mlstm_hi:], pad], axis=-1)


def _rope_tables(pos):
    half = A_HD // 2
    freq = ROPE_THETA ** (-2.0 * jnp.arange(half, dtype=F32) / A_HD)
    ang = pos.astype(F32)[:, None] * freq[None, :]
    cos, sin = jnp.cos(ang), jnp.sin(ang)
    return jnp.concatenate([cos, cos, cos, cos], axis=-1), jnp.concatenate([-sin, sin, -sin, sin], axis=-1)


def _kv_out(kv, lead):
    return kv.reshape(lead + (2, A_KV, A_HD))


def kernel(x_prompt, x_sample, cache_cmp_kv, cache_sel_kv, cache_win_kv, state_mlstm_C, state_mlstm_n,
           state_mlstm_m, state_ffn_conv, page_table, c_prompt, c_sample, norm1_g, norm2_g, w_mod, b_mod,
           w_in, b_if, mlstm_norm_g, w_out, w_up, w_conv, b_conv, w_down, final_g):
    batch, seq, d = x_prompt.shape
    dec_b, dec_t, _ = x_sample.shape
    depth = w_in.shape[0]
    ff = w_down.shape[1]
    n_pages = page_table.shape[1]
    past_len = n_pages * PAGE_SIZE
    wb = cache_win_kv.shape[2]
    rows_s = dec_t * dec_b
    assert A_KV * dec_t == SUBLANE and dec_t >= CONV_W - 1 and dec_b % SUBLANE == 0

    tm_p = min(512, seq)
    tq, tk = 512, min(512, seq)
    chunk_p = min(256, seq)
    ff_split = -(-ff // (2 * MXU_TILE)) * MXU_TILE
    ch = ((0, ff_split), (ff_split, ff))
    pages_per_step = n_pages
    t_pad = BF16_ROWS

    wi = _prep_w_in(w_in)
    wo, wu, wd = w_out.astype(BF16), w_up.astype(BF16), w_down.astype(BF16)
    fg = final_g.reshape(1, d)

    n_c = batch + dec_b
    c_all = jnp.concatenate([c_prompt, c_sample, jnp.zeros((-n_c % SUBLANE, d), F32)], axis=0)
    mod = _mod_call(c_all, w_mod, b_mod)

    cos_p, sin_p = _rope_tables(jnp.arange(seq))
    cos_s, sin_s = [jnp.repeat(t, dec_b, axis=0) for t in _rope_tables(past_len + jnp.arange(dec_t))]

    cache_cmp = cache_cmp_kv.transpose(0, 1, 3, 4, 5, 2)
    cache_sel = cache_sel_kv.transpose(0, 1, 3, 4, 5, 2)
    cache_win = cache_win_kv.transpose(0, 1, 3, 4, 5, 2)

    def kv_from_t(kv_t):
        return kv_t.reshape(kv_t.shape[:2] + (2, A_KV, A_HD, kv_t.shape[-1])).transpose(0, 1, 5, 2, 3, 4)

    def to_bt(a):
        a = a.reshape(dec_t, dec_b, a.shape[-1]).transpose(1, 0, 2)
        return jnp.pad(a, ((0, 0), (0, t_pad - dec_t), (0, 0)))

    xp = x_prompt.reshape(batch * seq, d)
    xs = x_sample.transpose(1, 0, 2).reshape(rows_s, d)
    outs_p = [[] for _ in range(4)]
    outs_s = [[] for _ in range(7)]
    kv_stacks = [jnp.zeros((depth, batch, KV_W, seq), F32) for _ in range(3)]
    y_p = y_s = None
    for l in range(depth):
        last = l == depth - 1
        mods = [mod[l, :, k * d:(k + 1) * d] for k in range(6)]
        sh1_p, sc1_p, g1_p, sh2_p, sc2_p, g2_p = [m[:batch].reshape(batch, 1, d) for m in mods]
        sh1_s, sc1_s, g1_s, sh2_s, sc2_s, g2_s = [jnp.tile(m[batch:n_c], (dec_t, 1)).reshape(1, rows_s, d)
                                                  for m in mods]
        n1, n2 = norm1_g[l].reshape(1, d), norm2_g[l].reshape(1, d)
        wc, bc = w_conv[l], b_conv[l].reshape(1, ff)

        tiles_p = seq // tm_p
        (qt, *kv_stacks, mqt, mk, mvt, mot, sm, smt, ksa, vst, kwb, vwt, cm) = _inproj_call(
            xp, sc1_p, sh1_p, n1, wi, cos_p, sin_p, tm=tm_p, tiles_per_seq=tiles_p, per_row_mod=False,
            attn_layouts=True, v_tiles=(tk, tq), kv_stacks=kv_stacks, layer=l)
        a_out = _attn_prompt_call(qt, cm, ksa, vst, kwb, vwt, sm, batch=batch, seq=seq, tq=tq, tk=tk)
        m_out, st_ct, st_n, st_m = _mlstm_t_call(mqt, mk, mvt, mot, sm, smt, b_if[l], mlstm_norm_g[l],
                                                 batch=batch, seq=seq, chunk=chunk_p)
        st_c = st_ct.transpose(0, 1, 3, 2)
        res = _ffn_call(xp, a_out, m_out, wo, g1_p, sc2_p, sh2_p, g2_p, n2, wu, wc, bc, wd,
                        jnp.zeros((batch, CONV_W - 1, ff), F32), fg if last else None,
                        tm=tm_p, tiles_per_seq=tiles_p, shift=1, per_row_mod=False, ch=ch, layer=l)
        xp, conv_p = res[0], res[1]
        if last:
            y_p = res[2]
        for lst, arr in zip(outs_p, (st_c, st_n, st_m.reshape(batch, M_HEADS), conv_p)):
            lst.append(arr)

        (q, kvc, kvs, kvw, mqkv, mo, sm) = _inproj_call(
            xs, sc1_s, sh1_s, n1, wi, cos_s, sin_s, tm=rows_s, tiles_per_seq=1, per_row_mod=True,
            attn_layouts=False, layer=l)
        cm_past = _cmp_means_call(cache_cmp, page_table, l)
        q5 = q.reshape(dec_t, dec_b, A_KV, A_G, LANE).transpose(1, 3, 2, 0, 4)
        q5 = jnp.concatenate([q5[:, :, :1], jnp.roll(q5[:, :, 1:], A_HD, axis=-1)], axis=2)
        gl = sm[:, :GATE_COLS].reshape(dec_t, dec_b, A_KV, A_G, 3).transpose(1, 3, 2, 0, 4)
        gl = jnp.pad(gl.reshape(dec_b, A_HEADS * dec_t, 3), ((0, 0), (0, 0), (0, LANE - 3)))
        o_s = _attn_sample_call(page_table, q5.reshape(dec_b, A_HEADS * dec_t, LANE), cm_past, to_bt(kvs),
                                cache_sel, cache_win, to_bt(kvw), gl, l,
                                pages_per_step=pages_per_step, dec_t=dec_t)
        o6 = o_s.reshape(dec_b, A_G, A_KV, dec_t, A_KV, A_HD)
        a_out = jnp.stack([o6[:, :, h, :, h] for h in range(A_KV)], axis=1)
        a_out = a_out.transpose(3, 0, 1, 2, 4).reshape(rows_s, A_W).astype(BF16)
        sm_bt = to_bt(sm)
        m_out, st_c, st_n, st_m = _mlstm_call(
            to_bt(mqkv).reshape(dec_b * t_pad, M_QKV_W), sm_bt.reshape(dec_b * t_pad, LANE),
            sm_bt[:, :, GATE_COLS:GATE_COLS + 2 * M_HEADS].transpose(0, 2, 1), to_bt(mo).reshape(dec_b * t_pad, M_W),
            b_if[l], mlstm_norm_g[l], state_mlstm_C[l], state_mlstm_n[l],
            state_mlstm_m[l].reshape(dec_b, 1, M_HEADS), batch=dec_b, seq=t_pad, chunk=t_pad, valid_len=dec_t,
            seqs=SUBLANE)
        m_out = m_out.reshape(dec_b, t_pad, M_W)[:, :dec_t].transpose(1, 0, 2).reshape(rows_s, M_W)
        hist = state_ffn_conv[l].transpose(1, 0, 2).reshape(1, (CONV_W - 1) * dec_b, ff)
        res = _ffn_call(xs, a_out, m_out, wo, g1_s, sc2_s, sh2_s, g2_s, n2, wu, wc, bc, wd,
                        hist, fg if last else None,
                        tm=rows_s, tiles_per_seq=1, shift=dec_b, per_row_mod=True, ch=ch, layer=l)
        xs, conv_s = res[0], res[1]
        if last:
            y_s = res[2]

        def s_kv(a):
            return _kv_out(a.reshape(dec_t, dec_b, KV_W).transpose(1, 0, 2), (dec_b, dec_t))

        for lst, arr in zip(outs_s, (s_kv(kvc), s_kv(kvs),
                                     kvw.reshape(dec_t, dec_b, 2, A_KV, A_HD).transpose(1, 2, 3, 4, 0),
                                     st_c, st_n, st_m.reshape(dec_b, M_HEADS),
                                     conv_s.reshape(CONV_W - 1, dec_b, ff).transpose(1, 0, 2))):
            lst.append(arr)

    y_prompt = y_p.reshape(batch, seq, d)
    y_sample = y_s.reshape(dec_t, dec_b, d).transpose(1, 0, 2)
    win_len = min(WINDOW, seq)
    kvc_all, kvs_all, kvw_all = kv_stacks
    s_cmp, s_sel, s_win_new, *s_rest = [jnp.stack(a, axis=0) for a in outs_s]
    s_win = jnp.concatenate([cache_win, s_win_new], axis=-1)[..., -wb:].transpose(0, 1, 5, 2, 3, 4)
    return ((y_prompt, y_sample, kv_from_t(kvc_all), kv_from_t(kvs_all), kv_from_t(kvw_all[..., seq - win_len:]))
            + tuple(jnp.stack(a, axis=0) for a in outs_p) + (s_cmp, s_sel, s_win) + tuple(s_rest))
```

```python
import functools

import jax
import jax.numpy as jnp
from jax import lax
from jax.experimental import pallas as pl
from jax.experimental.pallas import tpu as pltpu

F32 = jnp.float32
BF16 = jnp.bfloat16
I32 = jnp.int32

A_HD = 64
A_HEADS = 8
A_KV = 2
A_G = A_HEADS // A_KV
CMP_BLOCK = 64
TOPK = 16
WINDOW = 512
ROPE_THETA = 10000.0
FORCE = float(A_G + 1)
M_HEADS = 4
M_DK = 64
M_DV = 128
CONV_W = 3
EPS = 1e-6
PAGE_SIZE = 128

LANE = 128
SUBLANE = 8
BF16_ROWS = 16
MXU_TILE = 256
VMEM_LIMIT = 56 * 1024 * 1024

A_W = A_HEADS * A_HD
KV_W = 2 * A_KV * A_HD
M_QKV_W = 2 * M_HEADS * M_DK + M_HEADS * M_DV
M_W = M_HEADS * M_DV
OFF_Q = 0
OFF_KC = OFF_Q + A_W
OFF_KS = OFF_KC + KV_W
OFF_KW = OFF_KS + KV_W
OFF_M = OFF_KW + KV_W
OFF_MO = OFF_M + M_QKV_W
OFF_SM = OFF_MO + M_W
IN_W_PAD = OFF_SM + LANE
GATE_COLS = 3 * A_HEADS

LOG2E = 1.4426950408889634
RANK_STEP = 16
SEL_OFF_BIAS = -(2.0 ** 30)
MASK_NEG = -1e30


def _cparams(sem):
    return pltpu.CompilerParams(dimension_semantics=sem, vmem_limit_bytes=VMEM_LIMIT)


def _resident(shape, index_map):
    return pl.BlockSpec(shape, index_map, pipeline_mode=pl.Buffered(1))


def _layer_block(stacked, layer):
    return _resident((1,) + stacked.shape[1:], lambda i: (layer,) + (0,) * (stacked.ndim - 1))


def _mod_kernel(c_ref, w_ref, b_ref, o_ref):
    c = c_ref[...]
    a = (c * jax.nn.sigmoid(c)).astype(BF16)
    o_ref[0] = jnp.dot(a, w_ref[0].astype(BF16), preferred_element_type=F32) + b_ref[0]


def _mod_call(c_all, w_mod, b_mod):
    depth, d, n = w_mod.shape
    rows = c_all.shape[0]
    tn = 1536
    return pl.pallas_call(
        _mod_kernel,
        grid=(depth, n // tn),
        in_specs=[pl.BlockSpec((rows, d), lambda l, j: (0, 0)),
                  pl.BlockSpec((1, d, tn), lambda l, j: (l, 0, j)),
                  pl.BlockSpec((1, 1, tn), lambda l, j: (l, 0, j))],
        out_specs=pl.BlockSpec((1, rows, tn), lambda l, j: (l, 0, j)),
        out_shape=jax.ShapeDtypeStruct((depth, rows, n), F32),
        compiler_params=_cparams(("arbitrary", "arbitrary")),
    )(c_all, w_mod, b_mod.reshape(depth, 1, n))


def _inproj_kernel(x_ref, sc_ref, sh_ref, g_ref, w_ref, cos_ref, sin_ref, *refs, tm, tiles_per_seq, attn_layouts):
    if attn_layouts:
        (q_ref, kvc_ref, kvs_ref, kvw_ref, mqt_ref, mk_ref, mvt_ref, mot_ref, sm_ref, smt_ref,
         ksa_ref, vsb_ref, kwb_ref, vwb_ref, cm_ref) = refs[3:]
    else:
        q_ref, kvc_ref, kvs_ref, kvw_ref, mqkv_ref, mo_ref, sm_ref = refs
    x = x_ref[...]
    xn = x * lax.rsqrt(jnp.mean(x * x, axis=-1, keepdims=True) + EPS) * g_ref[...]
    xn = xn * (1.0 + sc_ref[0]) + sh_ref[0]
    xb = xn.astype(BF16)
    cos = cos_ref[...]
    sin = sin_ref[...]
    lane = lax.broadcasted_iota(I32, (tm, LANE), 1)
    lo_half = (lane % A_HD) < (A_HD // 2)
    first_head = lane < A_HD

    def proj(lo, hi):
        return jnp.dot(xb, w_ref[0, :, lo:hi], preferred_element_type=F32)

    def rope(v):
        rot = jnp.where(lo_half, pltpu.roll(v, LANE - A_HD // 2, 1), pltpu.roll(v, A_HD // 2, 1))
        return v * cos + rot * sin

    qa = proj(OFF_Q, OFF_Q + A_W)
    q_scale = A_HD ** -0.5 * (LOG2E if attn_layouts else 1.0)
    for s in range(A_W // LANE):
        r = rope(qa[:, s * LANE:(s + 1) * LANE]) * q_scale
        if attn_layouts:
            q_ref[s * LANE:(s + 1) * LANE, :] = r.T.astype(BF16)
        else:
            q_ref[:, (2 * s) * LANE:(2 * s + 1) * LANE] = jnp.where(first_head, r, 0.0).astype(BF16)
            q_ref[:, (2 * s + 1) * LANE:(2 * s + 2) * LANE] = jnp.where(
                first_head, pltpu.roll(r, A_HD, 1), 0.0).astype(BF16)

    def kv(lo, out_ref):
        a = proj(lo, lo + KV_W)
        k, v = rope(a[:, :LANE]), a[:, LANE:]
        if attn_layouts:
            k_t, v_t = k.T, v.T
            out_ref[0, 0, :LANE, :] = k_t
            out_ref[0, 0, LANE:, :] = v_t
        else:
            k_t = v_t = None
            out_ref[:, :LANE] = k
            out_ref[:, LANE:] = v
        return k, v, v_t

    kc, vc, _ = kv(OFF_KC, kvc_ref)
    ks, vs, vs_t = kv(OFF_KS, kvs_ref)
    kw, vw, vw_t = kv(OFF_KW, kvw_ref)

    if attn_layouts:
        nblk = tm // CMP_BLOCK
        cm_ref[:, :LANE] = jnp.sum(kc.reshape(nblk, CMP_BLOCK, LANE), axis=1) * (1.0 / CMP_BLOCK)
        cm_ref[:, LANE:] = jnp.sum(vc.reshape(nblk, CMP_BLOCK, LANE), axis=1) * (1.0 / CMP_BLOCK)
        row = lax.broadcasted_iota(I32, (tm, LANE), 0)
        blk = (pl.program_id(0) % tiles_per_seq) * nblk + row // CMP_BLOCK
        onehot = jnp.where(lane - A_HD == blk, 1.0, 0.0)
        ksa_ref[:, :LANE] = jnp.where(first_head, ks, onehot).astype(BF16)
        ksa_ref[:, LANE:] = jnp.where(first_head, pltpu.roll(ks, A_HD, 1), onehot).astype(BF16)
        kwb_ref[:, :LANE] = kw.astype(BF16)
        kwb_ref[:, LANE:] = pltpu.roll(kw, A_HD, 1).astype(BF16)
        for ref, v_t in ((vsb_ref, vs_t), (vwb_ref, vw_t)):
            width = ref.shape[-1]
            for j in range(tm // width):
                ref[0, j] = v_t[:, j * width:(j + 1) * width].astype(BF16)

    m = proj(OFF_M, OFF_MO)
    mo = proj(OFF_MO, OFF_SM)
    nqk = M_HEADS * M_DK
    mk = m[:, nqk:2 * nqk] * (M_DK ** -0.5)
    if attn_layouts:
        for s in range(nqk // LANE):
            mqt_ref[s * LANE:(s + 1) * LANE, :] = m[:, s * LANE:(s + 1) * LANE].T.astype(BF16)
        mk_ref[...] = mk.astype(BF16)
        for s in range(M_W // LANE):
            mvt_ref[s * LANE:(s + 1) * LANE, :] = m[:, 2 * nqk + s * LANE:2 * nqk + (s + 1) * LANE].T
            mot_ref[s * LANE:(s + 1) * LANE, :] = mo[:, s * LANE:(s + 1) * LANE].T
    else:
        mqkv_ref[:, :nqk] = m[:, :nqk]
        mqkv_ref[:, nqk:2 * nqk] = mk
        mqkv_ref[:, 2 * nqk:] = m[:, 2 * nqk:]
        mo_ref[...] = mo
    sm = proj(OFF_SM, IN_W_PAD)
    sm_ref[...] = sm
    if attn_layouts:
        smt_ref[...] = sm.T[GATE_COLS:GATE_COLS + 2 * M_HEADS, :]


def _mod_specs(per_row_mod, tm, d, tiles_per_seq):
    if per_row_mod:
        return pl.BlockSpec((1, tm, d), lambda i: (0, i, 0))
    return pl.BlockSpec((1, 1, d), lambda i: (i // tiles_per_seq, 0, 0))


def _inproj_call(x, sc, sh, g, w, cos, sin, *, tm, tiles_per_seq, per_row_mod, attn_layouts, v_tiles=None,
                 kv_stacks=(), layer=0):
    rows, d = x.shape
    n_tiles = rows // tm
    n_seq = n_tiles // tiles_per_seq
    seq = tm * tiles_per_seq
    mod_spec = _mod_specs(per_row_mod, tm, d, tiles_per_seq)
    if per_row_mod:
        tab_spec = pl.BlockSpec((tm, LANE), lambda i: (i, 0))
    else:
        tab_spec = pl.BlockSpec((tm, LANE), lambda i: (i % tiles_per_seq, 0))

    def rowspec(width):
        return pl.BlockSpec((tm, width), lambda i: (i, 0))

    if attn_layouts:
        kv_t = jax.ShapeDtypeStruct(kv_stacks[0].shape, F32)
        kv_t_spec = pl.BlockSpec((1, 1, KV_W, tm), lambda i: (layer, i // tiles_per_seq, 0, i % tiles_per_seq))
        assert kv_stacks[2].shape[-1] == tm
        kw_t = jax.ShapeDtypeStruct(kv_stacks[2].shape, F32)
        kw_t_spec = pl.BlockSpec((1, 1, KV_W, tm), lambda i: (layer, i // tiles_per_seq, 0, 0))
        out_shapes = [jax.ShapeDtypeStruct((A_W, rows), BF16), kv_t, kv_t, kw_t]
        out_specs = [pl.BlockSpec((A_W, tm), lambda i: (0, i)), kv_t_spec, kv_t_spec, kw_t_spec]
    else:
        out_shapes = [jax.ShapeDtypeStruct((rows, 2 * A_W), BF16)] + [jax.ShapeDtypeStruct((rows, KV_W), F32)] * 3
        out_specs = [rowspec(2 * A_W), rowspec(KV_W), rowspec(KV_W), rowspec(KV_W)]
    if attn_layouts:
        def feat_major(n, dtype):
            return jax.ShapeDtypeStruct((n, rows), dtype), pl.BlockSpec((n, tm), lambda i: (0, i))
        nqk = M_HEADS * M_DK
        pairs = [feat_major(nqk, BF16),
                 (jax.ShapeDtypeStruct((rows, nqk), BF16), rowspec(nqk)),
                 feat_major(M_W, F32), feat_major(M_W, F32)]
    else:
        pairs = [(jax.ShapeDtypeStruct((rows, M_QKV_W), F32), rowspec(M_QKV_W)),
                 (jax.ShapeDtypeStruct((rows, M_W), F32), rowspec(M_W))]
    pairs.append((jax.ShapeDtypeStruct((rows, LANE), F32), rowspec(LANE)))
    if attn_layouts:
        pairs.append(feat_major(2 * M_HEADS, F32))
    out_shapes += [p[0] for p in pairs]
    out_specs += [p[1] for p in pairs]
    if attn_layouts:
        def v_tiled(width):
            per_tile = tm // width
            return (jax.ShapeDtypeStruct((n_seq, seq // width, LANE, width), BF16),
                    pl.BlockSpec((1, per_tile, LANE, width),
                                 lambda i: (i // tiles_per_seq, i % tiles_per_seq, 0, 0)))
        (vs_shape, vs_spec), (vw_shape, vw_spec) = v_tiled(v_tiles[0]), v_tiled(v_tiles[1])
        out_shapes += [
            jax.ShapeDtypeStruct((rows, 2 * LANE), BF16),
            vs_shape,
            jax.ShapeDtypeStruct((rows, 2 * LANE), BF16),
            vw_shape,
            jax.ShapeDtypeStruct((rows // CMP_BLOCK, KV_W), F32),
        ]
        out_specs += [rowspec(2 * LANE), vs_spec, rowspec(2 * LANE), vw_spec,
                      pl.BlockSpec((tm // CMP_BLOCK, KV_W), lambda i: (i, 0))]
    return pl.pallas_call(
        functools.partial(_inproj_kernel, tm=tm, tiles_per_seq=tiles_per_seq, attn_layouts=attn_layouts),
        grid=(n_tiles,),
        in_specs=[rowspec(d), mod_spec, mod_spec, pl.BlockSpec((1, d), lambda i: (0, 0)),
                  _layer_block(w, layer), tab_spec, tab_spec]
                 + [pl.BlockSpec(memory_space=pl.ANY)] * len(kv_stacks),
        out_specs=out_specs,
        out_shape=out_shapes,
        input_output_aliases={7 + k: 1 + k for k in range(len(kv_stacks))},
        compiler_params=_cparams(("arbitrary",)),
    )(x, sc, sh, g, w, cos, sin, *kv_stacks)


def _nt(a, b):
    return lax.dot_general(a, b, (((1,), (1,)), ((), ())), preferred_element_type=F32)


def _softmax_update(s, v, carry, v_feature_major=False):
    m, l, acc = carry
    m_new = jnp.maximum(m, jnp.max(s, axis=-1, keepdims=True))
    alpha = jnp.exp(m - m_new)
    p = jnp.exp(s - m_new)
    l = alpha * l + jnp.sum(p, axis=-1, keepdims=True)
    pb = p.astype(BF16)
    pv = _nt(pb, v) if v_feature_major else jnp.dot(pb, v, preferred_element_type=F32)
    return m_new, l, alpha * acc + pv


def _softmax_init(rows, width):
    return (jnp.full((rows, 1), MASK_NEG, F32), jnp.zeros((rows, 1), F32), jnp.zeros((rows, width), F32))


def _masked_softmax(s, ok, axis, base2=False):
    s = jnp.where(ok, s, -jnp.inf)
    smax = jnp.max(s, axis=axis, keepdims=True)
    e = (jnp.exp2 if base2 else jnp.exp)(s - jnp.where(jnp.isfinite(smax), smax, 0.0))
    den = jnp.sum(e, axis=axis, keepdims=True)
    return e / jnp.where(den > 0, den, 1.0)


def _stable_rank_sublanes(score, sub8):
    nb, n = score.shape
    groups = [score[SUBLANE * r:SUBLANE * (r + 1), :] for r in range(nb // SUBLANE)]
    cnts = [jnp.zeros((SUBLANE, n), F32) for _ in groups]
    for ib in range(nb):
        s_i = jnp.broadcast_to(score[ib:ib + 1, :], (SUBLANE, n))
        for r, grp in enumerate(groups):
            if SUBLANE * r > ib:
                one = jnp.where(s_i >= grp, 1.0, 0.0)
            elif SUBLANE * (r + 1) <= ib:
                one = jnp.where(s_i > grp, 1.0, 0.0)
            else:
                one = jnp.where(sub8 > ib - SUBLANE * r, jnp.where(s_i >= grp, 1.0, 0.0),
                                jnp.where(s_i > grp, 1.0, 0.0))
            cnts[r] = cnts[r] + one
    return jnp.concatenate(cnts, axis=0)


def _causal_rank(score, sub8, n_allowed):
    nb, n = score.shape

    def ranked(rows):
        def branch(sc):
            if rows <= TOPK:
                return jnp.zeros((nb, n), F32)
            cnt = _stable_rank_sublanes(sc[:rows], sub8)
            return cnt if rows == nb else jnp.concatenate([cnt, jnp.zeros((nb - rows, n), F32)], axis=0)
        return branch

    sizes = list(range(RANK_STEP, nb, RANK_STEP)) + [nb]
    which = jnp.minimum((n_allowed + RANK_STEP - 1) // RANK_STEP, len(sizes)) - 1
    return lax.switch(which, [ranked(r) for r in sizes], score)


def _softmax_update_t(s_t, v_ext, carry):
    m, acc = carry
    m_new = jnp.maximum(m, jnp.max(s_t, axis=0, keepdims=True))
    p = jnp.exp2(s_t - m_new).astype(BF16)
    acc = jnp.exp2(m - m_new) * acc + jnp.dot(v_ext, p, preferred_element_type=F32)
    return m_new, acc


def _softmax_init_t(n, cols):
    return (jnp.full((1, cols), MASK_NEG, F32), jnp.zeros((n, cols), F32))


def _attn_prompt_kernel(qt_ref, cm_ref, ksa_ref, vst_ref, kwb_ref, vwt_ref, sm_ref, o_ref, *, tq, tk, nb):
    i = pl.program_id(1)
    q0 = i * tq
    cols = A_G * tq

    gates_t = jax.nn.sigmoid(sm_ref[...].T[:GATE_COLS, :])
    cm = cm_ref[0]
    cm_pad = jnp.concatenate([cm[:, LANE:], jnp.zeros((LANE - nb, LANE), F32)], axis=0)
    cm_vt = cm_pad.T.astype(BF16)

    blk_t = lax.broadcasted_iota(I32, (nb, tq), 0)
    qpos_t = q0 + lax.broadcasted_iota(I32, (nb, tq), 1)
    cur_t = qpos_t // CMP_BLOCK
    forced_t = (blk_t == 0) | (blk_t == cur_t)
    allowed_t = blk_t <= cur_t
    blk_t4 = lax.broadcasted_iota(I32, (nb, cols), 0)
    qpos_t4 = q0 + (lax.broadcasted_iota(I32, (nb, cols), 1) & (tq - 1))
    cmp_ok_t4 = (blk_t4 + 1) * CMP_BLOCK <= qpos_t4 + 1
    cg = max(tq, cols // 2)
    n_cg = cols // cg
    kk_k = lax.broadcasted_iota(I32, (tk, cg), 0)
    qq_k = lax.broadcasted_iota(I32, (tk, cg), 1) & (tq - 1)
    ones_k = jnp.ones((BF16_ROWS, tk), BF16)
    ones_q = jnp.ones((BF16_ROWS, tq), BF16)
    kk_q = lax.broadcasted_iota(I32, (tq, cg), 0)
    qq_q = lax.broadcasted_iota(I32, (tq, cg), 1) & (tq - 1)
    sub8 = lax.broadcasted_iota(I32, (SUBLANE, tq), 0)
    zeros_hd = jnp.zeros((A_HD, tq), BF16)

    def stacked(q_heads, lower):
        return jnp.concatenate([jnp.concatenate([qg, lower], axis=0) for qg in q_heads], axis=1)

    q0t, q_aug, o_cmp = [], [], []
    for h in range(A_KV):
        q_heads = [qt_ref[(A_G * h + g) * A_HD:(A_G * h + g + 1) * A_HD, :] for g in range(A_G)]
        q0t.append(stacked(q_heads, zeros_hd))
        cm_k = cm[:, :LANE] if h == 0 else pltpu.roll(cm[:, :LANE], A_HD, 1)
        p_t = _masked_softmax(jnp.dot(cm_k.astype(BF16), q0t[h], preferred_element_type=F32), cmp_ok_t4, 0,
                              base2=True)
        p_pad = jnp.concatenate([p_t, jnp.zeros((LANE - nb, cols), F32)], axis=0).astype(BF16)
        o_cmp.append(jnp.dot(cm_vt[h * A_HD:(h + 1) * A_HD], p_pad, preferred_element_type=F32))

        imp = p_t[:, 0:tq]
        for g in range(1, A_G):
            imp = imp + p_t[:, g * tq:(g + 1) * tq]
        score = jnp.where(forced_t, FORCE, jnp.where(allowed_t, imp, -1.0))
        cnt = _causal_rank(score, sub8, (q0 + tq - 1) // CMP_BLOCK + 1)
        bias_t = jnp.where(cnt < TOPK, jnp.where(score >= 0, 0.0, SEL_OFF_BIAS), SEL_OFF_BIAS)
        if nb < A_HD:
            bias_t = jnp.concatenate([bias_t, jnp.zeros((A_HD - nb, tq), F32)], axis=0)
        q_aug.append(stacked(q_heads, bias_t.astype(BF16)))

    def sel_scores(kt, masked):
        k0 = pl.multiple_of(kt * tk, tk)
        scores = []
        for h in range(A_KV):
            for c in range(n_cg):
                s = jnp.dot(ksa_ref[0, pl.ds(k0, tk), h * LANE:(h + 1) * LANE],
                            q_aug[h][:, c * cg:(c + 1) * cg], preferred_element_type=F32)
                if masked:
                    s = jnp.where(k0 + kk_k <= q0 + qq_k, s, MASK_NEG)
                scores.append(s)
        return scores

    def sel_update(kt, scores, carries):
        out = []
        for h in range(A_KV):
            v_ext = jnp.concatenate([vst_ref[0, kt, h * A_HD:(h + 1) * A_HD, :], ones_k], axis=0)
            for c in range(n_cg):
                out.append(_softmax_update_t(scores[h * n_cg + c], v_ext, carries[h * n_cg + c]))
        return tuple(out)

    def sel_pair(kp, carries):
        s_a, s_b = sel_scores(2 * kp, False), sel_scores(2 * kp + 1, False)
        return sel_update(2 * kp + 1, s_b, sel_update(2 * kp, s_a, carries))

    n_full = q0 // tk
    carries = lax.fori_loop(0, n_full // 2, sel_pair,
                            tuple(_softmax_init_t(A_HD + BF16_ROWS, cg) for _ in range(A_KV * n_cg)))
    carries = lax.fori_loop(n_full - n_full % 2, n_full,
                            lambda kt, c: sel_update(kt, sel_scores(kt, False), c), carries)

    s_diag = sel_scores(n_full, True)
    n_wt = WINDOW // tq + 1
    tiles = [jnp.maximum(i - wt, 0) for wt in range(n_wt)]
    win_scores = []
    for h in range(A_KV):
        for c in range(n_cg):
            per_tile = []
            for wt in range(n_wt):
                startc = pl.multiple_of(tiles[wt] * tq, tq)
                s = jnp.dot(kwb_ref[0, pl.ds(startc, tq), h * LANE:(h + 1) * LANE],
                            q0t[h][:, c * cg:(c + 1) * cg], preferred_element_type=F32)
                if wt == 0:
                    s = jnp.where(kk_q <= qq_q, s, MASK_NEG)
                else:
                    if wt == n_wt - 1:
                        s = jnp.where(kk_q > qq_q, s, MASK_NEG)
                    s = s + jnp.where(q0 - wt * tq >= 0, 0.0, MASK_NEG)
                per_tile.append(s)
            win_scores.append(per_tile)

    carries = sel_update(n_full, s_diag, carries)
    o_sel = [jnp.concatenate([acc[:A_HD] / acc[A_HD:A_HD + 1] for _, acc in carries[h * n_cg:(h + 1) * n_cg]], axis=1)
             for h in range(A_KV)]

    o_win = []
    for h in range(A_KV):
        v_ext = [jnp.concatenate([vwt_ref[0, tile, h * A_HD:(h + 1) * A_HD, :], ones_q], axis=0) for tile in tiles]
        outs = []
        for c in range(n_cg):
            per_tile = win_scores[h * n_cg + c]
            m = functools.reduce(jnp.maximum, [jnp.max(s, axis=0, keepdims=True) for s in per_tile])
            acc = functools.reduce(jnp.add, [jnp.dot(v, jnp.exp2(s - m).astype(BF16), preferred_element_type=F32)
                                             for v, s in zip(v_ext, per_tile)])
            outs.append(acc[:A_HD] / acc[A_HD:A_HD + 1])
        o_win.append(jnp.concatenate(outs, axis=1))

    mixes = []
    for h in range(A_KV):
        for g in range(A_G):
            head = A_G * h + g
            sl = (slice(None), slice(g * tq, (g + 1) * tq))
            mixes.append(gates_t[3 * head:3 * head + 1, :] * o_cmp[h][sl] + gates_t[3 * head + 1:3 * head + 2, :] * o_sel[h][sl]
                         + gates_t[3 * head + 2:3 * head + 3, :] * o_win[h][sl])
    o_ref[...] = jnp.concatenate(mixes, axis=0).T.astype(o_ref.dtype)


def _attn_prompt_call(qt, cm, ksa, vst, kwb, vwt, sm, *, batch, seq, tq, tk):
    nb = seq // CMP_BLOCK
    tiles = seq // tq
    assert nb <= A_HD and tq % LANE == 0 and tk % tq == 0 and seq % tk == 0 and WINDOW % tq == 0

    def full(width):
        return pl.BlockSpec((1, seq, width), lambda b, i: (b, 0, 0))

    def full_t(arr):
        return pl.BlockSpec((1,) + arr.shape[1:], lambda b, i: (b, 0, 0, 0))

    return pl.pallas_call(
        functools.partial(_attn_prompt_kernel, tq=tq, tk=tk, nb=nb),
        grid=(batch, tiles),
        in_specs=[pl.BlockSpec((A_W, tq), lambda b, i: (0, b * tiles + i)),
                  pl.BlockSpec((1, nb, KV_W), lambda b, i: (b, 0, 0)),
                  full(2 * LANE), full_t(vst), full(2 * LANE), full_t(vwt),
                  pl.BlockSpec((tq, LANE), lambda b, i: (b * tiles + i, 0))],
        out_specs=pl.BlockSpec((tq, A_W), lambda b, i: (b * tiles + i, 0)),
        out_shape=jax.ShapeDtypeStruct((batch * seq, A_W), BF16),
        compiler_params=_cparams(("arbitrary", "arbitrary")),
    )(qt, cm.reshape(batch, nb, KV_W), ksa.reshape(batch, seq, 2 * LANE), vst, kwb.reshape(batch, seq, 2 * LANE),
      vwt, sm)


def _page_spec(k, pages_per_step, layer):
    def index(b, j, pt):
        return (layer, pt[b, j * pages_per_step + k], 0, 0, 0, 0)
    return pl.BlockSpec((1, 1, 2, A_KV, A_HD, PAGE_SIZE), index)


def _page_kv(ref):
    n_pos = ref.shape[-1]
    return ref[0, 0, 0].reshape(A_KV * A_HD, n_pos), ref[0, 0, 1].reshape(A_KV * A_HD, n_pos)


def _cmp_means_kernel(pt_ref, *refs, n_pages):
    del pt_ref
    o_ref = refs[n_pages]
    per_page = PAGE_SIZE // CMP_BLOCK
    nbp = n_pages * per_page
    pos_blk = lax.broadcasted_iota(I32, (PAGE_SIZE, nbp), 0) // CMP_BLOCK
    blk = lax.broadcasted_iota(I32, (PAGE_SIZE, nbp), 1)
    acc = jnp.zeros((KV_W, nbp), F32)
    for k in range(n_pages):
        page = refs[k][0, 0].reshape(KV_W, PAGE_SIZE)
        place = jnp.where(blk == per_page * k + pos_blk, 1.0, 0.0).astype(BF16)
        for part in _split_bf16(page, 2):
            acc = acc + jnp.dot(part, place, preferred_element_type=F32)
    o_ref[0] = acc * (1.0 / CMP_BLOCK)


def _cmp_means_call(cache, page_table, layer):
    dec_b, n_pages = page_table.shape
    nbp = n_pages * (PAGE_SIZE // CMP_BLOCK)
    grid_spec = pltpu.PrefetchScalarGridSpec(
        num_scalar_prefetch=1,
        grid=(dec_b, 1),
        in_specs=[_page_spec(k, n_pages, layer) for k in range(n_pages)],
        out_specs=pl.BlockSpec((1, KV_W, nbp), lambda b, j, pt: (b, 0, 0)),
    )
    return pl.pallas_call(
        functools.partial(_cmp_means_kernel, n_pages=n_pages),
        grid_spec=grid_spec,
        out_shape=jax.ShapeDtypeStruct((dec_b, KV_W, nbp), F32),
        compiler_params=_cparams(("arbitrary", "arbitrary")),
    )(page_table, *([cache] * n_pages))


def _attn_sample_kernel(pt_ref, q_ref, cm_ref, kn_ref, win_ref, wn_ref, gl_ref, *refs,
                        pages_per_step, dec_t, past_len, nbp):
    del pt_ref
    page_refs = refs[:pages_per_step]
    o_ref = refs[pages_per_step]
    m_scr, l_scr, acc_scr, bias_scr, expand_scr, ocmp_scr = refs[pages_per_step + 1:]
    j = pl.program_id(1)
    rows = q_ref.shape[1]
    grp = A_KV * dec_t
    tk = pages_per_step * PAGE_SIZE
    blocks_per_step = tk // CMP_BLOCK
    qb = q_ref[0]
    t_row = lax.broadcasted_iota(I32, (rows, 1), 0) % dec_t

    @pl.when(j == 0)
    def _():
        cm_t = cm_ref[0]
        blk = lax.broadcasted_iota(I32, (rows, nbp), 1)
        ok = (blk + 1) * CMP_BLOCK <= past_len + t_row + 1
        p = _masked_softmax(jnp.dot(qb, cm_t[:LANE].astype(BF16), preferred_element_type=F32), ok, 1)
        ocmp_scr[...] = _nt(p.astype(BF16), cm_t[LANE:].astype(BF16))
        imp = p[0:grp]
        for g in range(1, A_G):
            imp = imp + p[g * grp:(g + 1) * grp]
        blk8 = lax.broadcasted_iota(I32, (grp, nbp), 1)
        cur = (past_len + lax.broadcasted_iota(I32, (grp, nbp), 0) % dec_t) // CMP_BLOCK
        score = jnp.where(blk8 == 0, FORCE, jnp.where(blk8 == cur, FORCE, jnp.where(blk8 <= cur, imp, -1.0)))
        cnt = jnp.where(FORCE > score, 1.0, 0.0)
        for ib in range(nbp):
            c_i = score[:, ib:ib + 1]
            cnt = cnt + jnp.where(blk8 > ib, jnp.where(c_i >= score, 1.0, 0.0), jnp.where(c_i > score, 1.0, 0.0))
        bias = jnp.where(cnt < TOPK, jnp.where(score >= 0, 0.0, SEL_OFF_BIAS), SEL_OFF_BIAS)
        bias = jnp.concatenate([bias] * A_G, axis=0)
        for jj in range(bias_scr.shape[0]):
            bias_scr[jj] = pltpu.roll(bias, (nbp - jj * blocks_per_step) % nbp, 1).astype(BF16)
        blk_e = lax.broadcasted_iota(I32, (nbp, tk), 0)
        key_blk = lax.broadcasted_iota(I32, (nbp, tk), 1) // CMP_BLOCK
        expand_scr[...] = jnp.where(blk_e == key_blk, 1.0, 0.0).astype(BF16)
        m_scr[...] = jnp.full(m_scr.shape, MASK_NEG, F32)
        l_scr[...] = jnp.zeros(l_scr.shape, F32)
        acc_scr[...] = jnp.zeros(acc_scr.shape, F32)

    pages = [_page_kv(r) for r in page_refs]
    keys_t = jnp.concatenate([k.astype(BF16) for k, _ in pages], axis=1)
    vals_t = jnp.concatenate([v.astype(BF16) for _, v in pages], axis=1)
    s = (jnp.dot(qb, keys_t, preferred_element_type=F32)
         + jnp.dot(bias_scr[j], expand_scr[...], preferred_element_type=F32))
    m, l, acc = _softmax_update(s, vals_t, (m_scr[...], l_scr[...], acc_scr[...]), v_feature_major=True)
    m_scr[...] = m
    l_scr[...] = l
    acc_scr[...] = acc

    @pl.when(j == pl.num_programs(1) - 1)
    def _():
        new_ok = lax.broadcasted_iota(I32, (rows, kn_ref.shape[1]), 1) <= t_row

        def new_rows(carry, ref):
            kv = ref[0]
            s_n = jnp.where(new_ok, _nt(qb, kv[:, :LANE].astype(BF16)), MASK_NEG)
            return _softmax_update(s_n, kv[:, LANE:].astype(BF16), carry)

        _, l_s, acc_s = new_rows((m_scr[...], l_scr[...], acc_scr[...]), kn_ref)
        o_sel = acc_s / l_s

        kw_t, vw_t = _page_kv(win_ref)
        wb = kw_t.shape[1]
        jw = lax.broadcasted_iota(I32, (rows, wb), 1)
        s_w = jnp.dot(qb, kw_t.astype(BF16), preferred_element_type=F32)
        s_w = jnp.where(jw > t_row + (wb - WINDOW), s_w, MASK_NEG)
        carry = _softmax_update(s_w, vw_t.astype(BF16), _softmax_init(rows, LANE), v_feature_major=True)
        _, l_w, acc_w = new_rows(carry, wn_ref)
        o_win = acc_w / l_w

        gate = jax.nn.sigmoid(gl_ref[0])
        o_ref[0] = gate[:, 0:1] * ocmp_scr[...] + gate[:, 1:2] * o_sel + gate[:, 2:3] * o_win


def _attn_sample_call(page_table, q, cm, kv_new, cache_sel, cache_win, win_new, gate_logits, layer, *,
                      pages_per_step, dec_t):
    dec_b, n_pages = page_table.shape
    rows = q.shape[1]
    nbp = cm.shape[2]
    wb = cache_win.shape[-1]
    past_len = n_pages * PAGE_SIZE
    assert nbp == LANE and past_len >= wb and wb >= WINDOW

    def per_b(shape):
        return pl.BlockSpec((1,) + shape, lambda b, j, pt: (b, 0, 0))

    grid_spec = pltpu.PrefetchScalarGridSpec(
        num_scalar_prefetch=1,
        grid=(dec_b, n_pages // pages_per_step),
        in_specs=[per_b((rows, LANE)), per_b((KV_W, nbp)), per_b(kv_new.shape[1:]),
                  pl.BlockSpec((1, 1, 2, A_KV, A_HD, wb), lambda b, j, pt: (layer, b, 0, 0, 0, 0)),
                  per_b(win_new.shape[1:]), per_b((rows, LANE))]
                 + [_page_spec(k, pages_per_step, layer) for k in range(pages_per_step)],
        out_specs=per_b((rows, LANE)),
        scratch_shapes=[pltpu.VMEM((rows, 1), F32), pltpu.VMEM((rows, 1), F32), pltpu.VMEM((rows, LANE), F32),
                        pltpu.VMEM((n_pages // pages_per_step, rows, nbp), BF16),
                        pltpu.VMEM((nbp, pages_per_step * PAGE_SIZE), BF16), pltpu.VMEM((rows, LANE), F32)],
    )
    return pl.pallas_call(
        functools.partial(_attn_sample_kernel, pages_per_step=pages_per_step, dec_t=dec_t,
                          past_len=past_len, nbp=nbp),
        grid_spec=grid_spec,
        out_shape=jax.ShapeDtypeStruct((dec_b, rows, LANE), F32),
        compiler_params=_cparams(("arbitrary", "arbitrary")),
    )(page_table, q, cm, kv_new, cache_win, win_new, gate_logits, *([cache_sel] * pages_per_step))


def _split_bf16(a, terms):
    parts = []
    for _ in range(terms):
        piece = a.astype(BF16)
        parts.append(piece)
        a = a - piece.astype(F32)
    return parts


def _log_sigmoid(x):
    return jnp.minimum(x, 0.0) - jnp.log1p(jnp.exp(-jnp.abs(x)))


def _mlstm_kernel(qkv_ref, sm_ref, smt_ref, mo_ref, bc_ref, br_ref, g_ref, c0_ref, n0_ref, m0_ref,
                  o_ref, c_ref, n_ref, m_ref, *, chunk, valid_len, seqs):
    @pl.when(pl.program_id(1) == 0)
    def _():
        c_ref[...] = c0_ref[...]
        n_ref[...] = n0_ref[...]
        m_ref[...] = m0_ref[...]

    ng = 2 * M_HEADS
    t_i = lax.broadcasted_iota(I32, (chunk, chunk), 0)
    s_i = lax.broadcasted_iota(I32, (chunk, chunk), 1)
    causal = s_i <= t_i
    tri = jnp.where(causal, 1.0, 0.0).astype(BF16)
    tri_t = jnp.where(t_i <= s_i, 1.0, 0.0).astype(BF16)
    for sq in range(seqs):
        _mlstm_sequence(sq, slice(sq * chunk, (sq + 1) * chunk), qkv_ref, sm_ref, smt_ref, mo_ref, bc_ref, br_ref,
                        g_ref, o_ref, c_ref, n_ref, m_ref, causal, tri, tri_t, chunk, valid_len)


def _mlstm_sequence(sq, rows, qkv_ref, sm_ref, smt_ref, mo_ref, bc_ref, br_ref, g_ref, o_ref, c_ref, n_ref, m_ref,
                    causal, tri, tri_t, chunk, valid_len):
    ng = 2 * M_HEADS
    gate_c = sm_ref[rows, GATE_COLS:GATE_COLS + ng] + bc_ref[...]
    gate_r = smt_ref[sq] + br_ref[...]
    lf_c = _log_sigmoid(gate_c)
    lf_r = _log_sigmoid(gate_r)
    ig_c, ig_r = gate_c, gate_r
    if valid_len < chunk:
        tc = lax.broadcasted_iota(I32, (chunk, ng), 0)
        tr = lax.broadcasted_iota(I32, (ng, chunk), 1)
        lf_c = jnp.where(tc < valid_len, lf_c, 0.0)
        lf_r = jnp.where(tr < valid_len, lf_r, 0.0)
        ig_c = jnp.where(tc < valid_len, ig_c, MASK_NEG)
        ig_r = jnp.where(tr < valid_len, ig_r, MASK_NEG)
    b_c = sum(jnp.dot(tri, part, preferred_element_type=F32) for part in _split_bf16(lf_c, 3))
    b_r = sum(jnp.dot(part, tri_t, preferred_element_type=F32) for part in _split_bf16(lf_r, 3))

    nqk = M_HEADS * M_DK
    for n in range(M_HEADS):
        q = qkv_ref[rows, n * M_DK:(n + 1) * M_DK]
        k = qkv_ref[rows, nqk + n * M_DK:nqk + (n + 1) * M_DK]
        v = qkv_ref[rows, 2 * nqk + n * M_DV:2 * nqk + (n + 1) * M_DV]
        qb, kb, vb = q.astype(BF16), k.astype(BF16), v.astype(BF16)
        fcol = M_HEADS + n
        bc = b_c[:, fcol:fcol + 1]
        br = b_r[fcol:fcol + 1, :]
        igc = ig_c[:, n:n + 1]
        igr = ig_r[n:n + 1, :]
        b_end = bc[chunk - 1:chunk, :]
        m_prev = m_ref[sq, :, n:n + 1]
        c_prev = c_ref[sq, n]
        n_prev = n_ref[sq, n:n + 1, :]

        a = bc + m_prev
        d = jnp.where(causal, bc + (igr - br), -jnp.inf)
        m_t = jnp.maximum(a, jnp.max(d, axis=1, keepdims=True))
        w = _nt(qb, kb) * jnp.exp(d - m_t)
        aw = jnp.exp(a - m_t)
        num = (jnp.dot(w.astype(BF16), vb, preferred_element_type=F32)
               + aw * jnp.dot(qb, c_prev.astype(BF16), preferred_element_type=F32))
        den = jnp.sum(w, axis=1, keepdims=True) + aw * jnp.sum(q * n_prev, axis=1, keepdims=True)
        hcell = num / jnp.maximum(jnp.abs(den), jnp.exp(-m_t))

        hn = hcell * lax.rsqrt(jnp.mean(hcell * hcell, axis=-1, keepdims=True) + EPS) * g_ref[:, n * M_DV:(n + 1) * M_DV]
        o_ref[rows, n * M_DV:(n + 1) * M_DV] = (
            jax.nn.sigmoid(mo_ref[rows, n * M_DV:(n + 1) * M_DV]) * hn).astype(o_ref.dtype)

        wl = b_end - bc + igc
        m_new = jnp.maximum(b_end + m_prev, jnp.max(wl, axis=0, keepdims=True))
        decay = jnp.exp(b_end + m_prev - m_new)
        kws = k * jnp.exp(wl - m_new)
        c_ref[sq, n] = decay * c_prev + lax.dot_general(kws.astype(BF16), vb, (((0,), (0,)), ((), ())),
                                                        preferred_element_type=F32)
        n_ref[sq, n:n + 1, :] = decay * n_prev + jnp.sum(kws, axis=0, keepdims=True)
        m_ref[sq, :, n:n + 1] = m_new


def _mlstm_call(qkv, sm, smt, mo, b_if, norm_g, c0, n0, m0, *, batch, seq, chunk, valid_len, seqs=1):
    nc = seq // chunk
    ng = 2 * M_HEADS
    assert batch % seqs == 0 and (seqs == 1 or nc == 1)

    def rowspec(width):
        return pl.BlockSpec((seqs * chunk, width), lambda b, c: (b * nc + c, 0))

    def const(shape):
        return pl.BlockSpec(shape, lambda b, c: (0,) * len(shape))

    state_specs = [pl.BlockSpec((seqs, M_HEADS, M_DK, M_DV), lambda b, c: (b, 0, 0, 0)),
                   pl.BlockSpec((seqs, M_HEADS, M_DK), lambda b, c: (b, 0, 0)),
                   pl.BlockSpec((seqs, 1, M_HEADS), lambda b, c: (b, 0, 0))]
    return pl.pallas_call(
        functools.partial(_mlstm_kernel, chunk=chunk, valid_len=valid_len, seqs=seqs),
        grid=(batch // seqs, nc),
        in_specs=[rowspec(M_QKV_W), rowspec(LANE), pl.BlockSpec((seqs, ng, chunk), lambda b, c: (b, 0, c)),
                  rowspec(M_W), const((1, ng)), const((ng, 1)), const((1, M_W))] + state_specs,
        out_specs=[rowspec(M_W)] + state_specs,
        out_shape=[jax.ShapeDtypeStruct((batch * seq, M_W), BF16),
                   jax.ShapeDtypeStruct((batch, M_HEADS, M_DK, M_DV), F32),
                   jax.ShapeDtypeStruct((batch, M_HEADS, M_DK), F32),
                   jax.ShapeDtypeStruct((batch, 1, M_HEADS), F32)],
        compiler_params=_cparams(("arbitrary", "arbitrary")),
    )(qkv, sm, smt, mo, b_if.reshape(1, ng), b_if.reshape(ng, 1), norm_g.reshape(1, M_W), c0, n0, m0)


def _mlstm_t_kernel(qt_ref, k_ref, vt_ref, mot_ref, sm_ref, smt_ref, bc_ref, br_ref, g_ref, o_ref, ct_ref, n_ref, m_ref,
                    *, chunk):
    @pl.when(pl.program_id(1) == 0)
    def _():
        ct_ref[...] = jnp.zeros(ct_ref.shape, F32)
        n_ref[...] = jnp.zeros(n_ref.shape, F32)
        m_ref[...] = jnp.zeros(m_ref.shape, F32)

    ng = 2 * M_HEADS
    gate_c = sm_ref[:, GATE_COLS:GATE_COLS + ng] + bc_ref[...]
    gate_r = smt_ref[...] + br_ref[...]
    s_i = lax.broadcasted_iota(I32, (chunk, chunk), 0)
    t_i = lax.broadcasted_iota(I32, (chunk, chunk), 1)
    causal = s_i <= t_i
    tri_c = jnp.where(t_i <= s_i, 1.0, 0.0).astype(BF16)
    tri_r = jnp.where(causal, 1.0, 0.0).astype(BF16)
    b_c = sum(jnp.dot(tri_c, part, preferred_element_type=F32) for part in _split_bf16(_log_sigmoid(gate_c), 3))
    b_r = sum(jnp.dot(part, tri_r, preferred_element_type=F32) for part in _split_bf16(_log_sigmoid(gate_r), 3))
    r_c = gate_c[:, :M_HEADS] - b_c[:, M_HEADS:]
    g_full = jnp.concatenate([g_ref[...]] * (chunk // LANE), axis=1)

    outs = []
    for n in range(M_HEADS):
        qt = qt_ref[n * M_DK:(n + 1) * M_DK, :]
        kb = k_ref[:, n * M_DK:(n + 1) * M_DK]
        vt = vt_ref[n * M_DV:(n + 1) * M_DV, :]
        b_row = b_r[M_HEADS + n:M_HEADS + n + 1, :]
        ig_row = gate_r[n:n + 1, :]
        b_end = b_c[chunk - 1:chunk, M_HEADS + n:M_HEADS + n + 1]
        m_prev = m_ref[0, :, n:n + 1]
        ct_prev = ct_ref[0, n]
        n_prev = n_ref[0, n:n + 1, :]

        a = b_row + m_prev
        d_t = jnp.where(causal, r_c[:, n:n + 1] + b_row, -jnp.inf)
        m_t = jnp.maximum(a, jnp.max(d_t, axis=0, keepdims=True))
        w_t = jnp.dot(kb, qt, preferred_element_type=F32) * jnp.exp(d_t - m_t)
        aw = jnp.exp(a - m_t)
        num = (jnp.dot(vt.astype(BF16), w_t.astype(BF16), preferred_element_type=F32)
               + aw * jnp.dot(ct_prev.astype(BF16), qt, preferred_element_type=F32))
        den = (jnp.sum(w_t, axis=0, keepdims=True)
               + aw * jnp.dot(n_prev.astype(BF16), qt, preferred_element_type=F32))
        hcell = num / jnp.maximum(jnp.abs(den), jnp.exp(-m_t))
        hn = (hcell * lax.rsqrt(jnp.mean(hcell * hcell, axis=0, keepdims=True) + EPS)
              * g_full[n * M_DV:(n + 1) * M_DV, :])
        outs.append(jax.nn.sigmoid(mot_ref[n * M_DV:(n + 1) * M_DV, :]) * hn)

        wl = b_end - b_row + ig_row
        m_new = jnp.maximum(b_end + m_prev, jnp.max(wl, axis=1, keepdims=True))
        decay = jnp.exp(b_end + m_prev - m_new)
        ws = jnp.exp(wl - m_new)
        ct_ref[0, n] = decay * ct_prev + jnp.dot((vt * ws).astype(BF16), kb, preferred_element_type=F32)
        n_ref[0, n:n + 1, :] = decay * n_prev + jnp.dot(ws.astype(BF16), kb, preferred_element_type=F32)
        m_ref[0, :, n:n + 1] = m_new
    o_ref[...] = jnp.concatenate(outs, axis=0).T.astype(o_ref.dtype)


def _mlstm_t_call(qt, k, vt, mot, sm, smt, b_if, norm_g, *, batch, seq, chunk):
    nc = seq // chunk
    ng = 2 * M_HEADS
    nqk = M_HEADS * M_DK
    assert chunk % LANE == 0

    def colspec(n):
        return pl.BlockSpec((n, chunk), lambda b, c: (0, b * nc + c))

    def rowspec(width):
        return pl.BlockSpec((chunk, width), lambda b, c: (b * nc + c, 0))

    def const(shape):
        return pl.BlockSpec(shape, lambda b, c: (0,) * len(shape))

    state_specs = [pl.BlockSpec((1, M_HEADS, M_DV, M_DK), lambda b, c: (b, 0, 0, 0)),
                   pl.BlockSpec((1, M_HEADS, M_DK), lambda b, c: (b, 0, 0)),
                   pl.BlockSpec((1, 1, M_HEADS), lambda b, c: (b, 0, 0))]
    return pl.pallas_call(
        functools.partial(_mlstm_t_kernel, chunk=chunk),
        grid=(batch, nc),
        in_specs=[colspec(nqk), rowspec(nqk), colspec(M_W), colspec(M_W), rowspec(LANE), colspec(ng),
                  const((1, ng)), const((ng, 1)), const((M_W, LANE))],
        out_specs=[rowspec(M_W)] + state_specs,
        out_shape=[jax.ShapeDtypeStruct((batch * seq, M_W), BF16),
                   jax.ShapeDtypeStruct((batch, M_HEADS, M_DV, M_DK), F32),
                   jax.ShapeDtypeStruct((batch, M_HEADS, M_DK), F32),
                   jax.ShapeDtypeStruct((batch, 1, M_HEADS), F32)],
        compiler_params=_cparams(("arbitrary", "arbitrary")),
    )(qt, k, vt, mot, sm, smt, b_if.reshape(1, ng), b_if.reshape(ng, 1),
      jnp.broadcast_to(norm_g.reshape(M_W, 1), (M_W, LANE)))


def _ffn_kernel(x_ref, a_ref, mh_ref, wo_ref, g1_ref, sc_ref, sh_ref, g2_ref, n2_ref, wu_ref, wc_ref, bc_ref,
                wd_ref, hist_ref, *refs, tm, tiles_per_seq, shift, ff, ch, final):
    if final:
        fg_ref, xo_ref, cs_ref, y_ref, carry_scr, up_scr = refs
    else:
        xo_ref, cs_ref, carry_scr, up_scr = refs
    hist_rows = (CONV_W - 1) * shift
    hoff = up_scr.shape[0] - tm

    @pl.when(pl.program_id(0) % tiles_per_seq == 0)
    def _():
        carry_scr[...] = hist_ref[0]

    y = (jnp.dot(a_ref[...], wo_ref[0, :A_W, :], preferred_element_type=F32)
         + jnp.dot(mh_ref[...], wo_ref[0, A_W:, :], preferred_element_type=F32))
    x1 = x_ref[...] + g1_ref[0] * y
    xn = x1 * lax.rsqrt(jnp.mean(x1 * x1, axis=-1, keepdims=True) + EPS) * n2_ref[...]
    xb = (xn * (1.0 + sc_ref[0]) + sh_ref[0]).astype(BF16)

    acc = jnp.zeros(x1.shape, F32)
    for lo, hi in ch:
        w = hi - lo
        u = jnp.dot(xb, wu_ref[0, :, lo:hi], preferred_element_type=F32)
        gt = jnp.dot(xb, wu_ref[0, :, ff + lo:ff + hi], preferred_element_type=F32)
        up_scr[hoff - hist_rows:hoff, :w] = carry_scr[:, lo:hi]
        up_scr[hoff:, :w] = u
        tail = u[tm - hist_rows:, :]
        carry_scr[:, lo:hi] = tail
        cs_ref[0, :, lo:hi] = tail
        conv = bc_ref[:, lo:hi]
        for jj in range(CONV_W - 1):
            start = hoff - (CONV_W - 1 - jj) * shift
            conv = conv + wc_ref[jj:jj + 1, lo:hi] * up_scr[start:start + tm, :w]
        conv = conv + wc_ref[CONV_W - 1:CONV_W, lo:hi] * u
        hid = conv * jax.nn.sigmoid(conv) * gt
        acc = acc + jnp.dot(hid.astype(BF16), wd_ref[0, lo:hi, :], preferred_element_type=F32)

    x2 = x1 + g2_ref[0] * acc
    xo_ref[...] = x2
    if final:
        y_ref[...] = x2 * lax.rsqrt(jnp.mean(x2 * x2, axis=-1, keepdims=True) + EPS) * fg_ref[...]


def _ffn_call(x, a_out, m_out, wo, g1, sc, sh, g2, n2, wu, wc, bc, wd, hist, final_g, *,
              tm, tiles_per_seq, shift, per_row_mod, ch, layer):
    rows, d = x.shape
    ff = wd.shape[1]
    groups, hist_rows, _ = hist.shape
    assert hist_rows == (CONV_W - 1) * shift and tm >= hist_rows
    assert ch[0][0] == 0 and ch[-1][1] == ff and all(a[1] == b[0] for a, b in zip(ch, ch[1:]))
    final = final_g is not None
    hoff = -(-hist_rows // SUBLANE) * SUBLANE
    mod_spec = _mod_specs(per_row_mod, tm, d, tiles_per_seq)

    def rowspec(width):
        return pl.BlockSpec((tm, width), lambda i: (i, 0))

    def const(arr):
        return _resident(arr.shape, lambda i: (0,) * arr.ndim)

    hist_spec = pl.BlockSpec((1, hist_rows, ff), lambda i: (i // tiles_per_seq, 0, 0))
    in_specs = [rowspec(d), rowspec(A_W), rowspec(M_W), _layer_block(wo, layer), mod_spec, mod_spec, mod_spec, mod_spec,
                pl.BlockSpec((1, d), lambda i: (0, 0)), _layer_block(wu, layer), const(wc), const(bc),
                _layer_block(wd, layer), hist_spec]
    args = [x, a_out, m_out, wo, g1, sc, sh, g2, n2, wu, wc, bc, wd, hist]
    out_specs = [rowspec(d), hist_spec]
    out_shape = [jax.ShapeDtypeStruct((rows, d), F32), jax.ShapeDtypeStruct(hist.shape, F32)]
    if final:
        in_specs.append(pl.BlockSpec((1, d), lambda i: (0, 0)))
        args.append(final_g)
        out_specs.append(rowspec(d))
        out_shape.append(jax.ShapeDtypeStruct((rows, d), F32))
    return pl.pallas_call(
        functools.partial(_ffn_kernel, tm=tm, tiles_per_seq=tiles_per_seq, shift=shift, ff=ff, ch=ch, final=final),
        grid=(rows // tm,),
        in_specs=in_specs,
        out_specs=out_specs,
        out_shape=out_shape,
        scratch_shapes=[pltpu.VMEM((hist_rows, ff), F32),
                        pltpu.VMEM((hoff + tm, max(hi - lo for lo, hi in ch)), F32)],
        compiler_params=_cparams(("arbitrary",)),
    )(*args)


def _prep_w_in(w_in):
    attn_end = OFF_M
    mlstm_lo = attn_end + GATE_COLS
    mlstm_hi = mlstm_lo + M_QKV_W + M_W
    assert mlstm_hi + 2 * M_HEADS == w_in.shape[-1]
    w = w_in.astype(BF16)
    pad = jnp.zeros(w.shape[:-1] + (LANE - GATE_COLS - 2 * M_HEADS,), BF16)
    return jnp.concatenate([w[..., :attn_end], w[..., mlstm_lo:mlstm_hi], w[..., attn_end:mlstm_lo],
                            w[..., mlstm_hi:], pad], axis=-1)


def _rope_tables(pos):
    half = A_HD // 2
    freq = ROPE_THETA ** (-2.0 * jnp.arange(half, dtype=F32) / A_HD)
    ang = pos.astype(F32)[:, None] * freq[None, :]
    cos, sin = jnp.cos(ang), jnp.sin(ang)
    return jnp.concatenate([cos, cos, cos, cos], axis=-1), jnp.concatenate([-sin, sin, -sin, sin], axis=-1)


def _kv_out(kv, lead):
    return kv.reshape(lead + (2, A_KV, A_HD))


def kernel(x_prompt, x_sample, cache_cmp_kv, cache_sel_kv, cache_win_kv, state_mlstm_C, state_mlstm_n,
           state_mlstm_m, state_ffn_conv, page_table, c_prompt, c_sample, norm1_g, norm2_g, w_mod, b_mod,
           w_in, b_if, mlstm_norm_g, w_out, w_up, w_conv, b_conv, w_down, final_g):
    batch, seq, d = x_prompt.shape
    dec_b, dec_t, _ = x_sample.shape
    depth = w_in.shape[0]
    ff = w_down.shape[1]
    n_pages = page_table.shape[1]
    past_len = n_pages * PAGE_SIZE
    wb = cache_win_kv.shape[2]
    rows_s = dec_t * dec_b
    assert A_KV * dec_t == SUBLANE and dec_t >= CONV_W - 1 and dec_b % SUBLANE == 0

    tm_p = min(512, seq)
    tq, tk = 512, min(512, seq)
    chunk_p = min(256, seq)
    ff_split = -(-ff // (2 * MXU_TILE)) * MXU_TILE
    ch = ((0, ff_split), (ff_split, ff))
    pages_per_step = n_pages
    t_pad = BF16_ROWS

    wi = _prep_w_in(w_in)
    wo, wu, wd = w_out.astype(BF16), w_up.astype(BF16), w_down.astype(BF16)
    fg = final_g.reshape(1, d)

    n_c = batch + dec_b
    c_all = jnp.concatenate([c_prompt, c_sample, jnp.zeros((-n_c % SUBLANE, d), F32)], axis=0)
    mod = _mod_call(c_all, w_mod, b_mod)

    cos_p, sin_p = _rope_tables(jnp.arange(seq))
    cos_s, sin_s = [jnp.repeat(t, dec_b, axis=0) for t in _rope_tables(past_len + jnp.arange(dec_t))]

    cache_cmp = cache_cmp_kv.transpose(0, 1, 3, 4, 5, 2)
    cache_sel = cache_sel_kv.transpose(0, 1, 3, 4, 5, 2)
    cache_win = cache_win_kv.transpose(0, 1, 3, 4, 5, 2)

    def kv_from_t(kv_t):
        return kv_t.reshape(kv_t.shape[:2] + (2, A_KV, A_HD, kv_t.shape[-1])).transpose(0, 1, 5, 2, 3, 4)

    def to_bt(a):
        a = a.reshape(dec_t, dec_b, a.shape[-1]).transpose(1, 0, 2)
        return jnp.pad(a, ((0, 0), (0, t_pad - dec_t), (0, 0)))

    xp = x_prompt.reshape(batch * seq, d)
    xs = x_sample.transpose(1, 0, 2).reshape(rows_s, d)
    outs_p = [[] for _ in range(4)]
    outs_s = [[] for _ in range(7)]
    win_len = min(WINDOW, seq)
    kv_stacks = [jnp.zeros((depth, batch, KV_W, n), F32) for n in (seq, seq, win_len)]
    y_p = y_s = None
    for l in range(depth):
        last = l == depth - 1
        mods = [mod[l, :, k * d:(k + 1) * d] for k in range(6)]
        sh1_p, sc1_p, g1_p, sh2_p, sc2_p, g2_p = [m[:batch].reshape(batch, 1, d) for m in mods]
        sh1_s, sc1_s, g1_s, sh2_s, sc2_s, g2_s = [jnp.tile(m[batch:n_c], (dec_t, 1)).reshape(1, rows_s, d)
                                                  for m in mods]
        n1, n2 = norm1_g[l].reshape(1, d), norm2_g[l].reshape(1, d)
        wc, bc = w_conv[l], b_conv[l].reshape(1, ff)

        tiles_p = seq // tm_p
        (qt, *kv_stacks, mqt, mk, mvt, mot, sm, smt, ksa, vst, kwb, vwt, cm) = _inproj_call(
            xp, sc1_p, sh1_p, n1, wi, cos_p, sin_p, tm=tm_p, tiles_per_seq=tiles_p, per_row_mod=False,
            attn_layouts=True, v_tiles=(tk, tq), kv_stacks=kv_stacks, layer=l)
        a_out = _attn_prompt_call(qt, cm, ksa, vst, kwb, vwt, sm, batch=batch, seq=seq, tq=tq, tk=tk)
        m_out, st_ct, st_n, st_m = _mlstm_t_call(mqt, mk, mvt, mot, sm, smt, b_if[l], mlstm_norm_g[l],
                                                 batch=batch, seq=seq, chunk=chunk_p)
        st_c = st_ct.transpose(0, 1, 3, 2)
        res = _ffn_call(xp, a_out, m_out, wo, g1_p, sc2_p, sh2_p, g2_p, n2, wu, wc, bc, wd,
                        jnp.zeros((batch, CONV_W - 1, ff), F32), fg if last else None,
                        tm=tm_p, tiles_per_seq=tiles_p, shift=1, per_row_mod=False, ch=ch, layer=l)
        xp, conv_p = res[0], res[1]
        if last:
            y_p = res[2]
        for lst, arr in zip(outs_p, (st_c, st_n, st_m.reshape(batch, M_HEADS), conv_p)):
            lst.append(arr)

        (q, kvc, kvs, kvw, mqkv, mo, sm) = _inproj_call(
            xs, sc1_s, sh1_s, n1, wi, cos_s, sin_s, tm=rows_s, tiles_per_seq=1, per_row_mod=True,
            attn_layouts=False, layer=l)
        cm_past = _cmp_means_call(cache_cmp, page_table, l)
        q5 = q.reshape(dec_t, dec_b, A_KV, A_G, LANE).transpose(1, 3, 2, 0, 4)
        q5 = jnp.concatenate([q5[:, :, :1], jnp.roll(q5[:, :, 1:], A_HD, axis=-1)], axis=2)
        gl = sm[:, :GATE_COLS].reshape(dec_t, dec_b, A_KV, A_G, 3).transpose(1, 3, 2, 0, 4)
        gl = jnp.pad(gl.reshape(dec_b, A_HEADS * dec_t, 3), ((0, 0), (0, 0), (0, LANE - 3)))
        o_s = _attn_sample_call(page_table, q5.reshape(dec_b, A_HEADS * dec_t, LANE), cm_past, to_bt(kvs),
                                cache_sel, cache_win, to_bt(kvw), gl, l,
                                pages_per_step=pages_per_step, dec_t=dec_t)
        o6 = o_s.reshape(dec_b, A_G, A_KV, dec_t, A_KV, A_HD)
        a_out = jnp.stack([o6[:, :, h, :, h] for h in range(A_KV)], axis=1)
        a_out = a_out.transpose(3, 0, 1, 2, 4).reshape(rows_s, A_W).astype(BF16)
        sm_bt = to_bt(sm)
        m_out, st_c, st_n, st_m = _mlstm_call(
            to_bt(mqkv).reshape(dec_b * t_pad, M_QKV_W), sm_bt.reshape(dec_b * t_pad, LANE),
            sm_bt[:, :, GATE_COLS:GATE_COLS + 2 * M_HEADS].transpose(0, 2, 1), to_bt(mo).reshape(dec_b * t_pad, M_W),
            b_if[l], mlstm_norm_g[l], state_mlstm_C[l], state_mlstm_n[l],
            state_mlstm_m[l].reshape(dec_b, 1, M_HEADS), batch=dec_b, seq=t_pad, chunk=t_pad, valid_len=dec_t,
            seqs=SUBLANE)
        m_out = m_out.reshape(dec_b, t_pad, M_W)[:, :dec_t].transpose(1, 0, 2).reshape(rows_s, M_W)
        hist = state_ffn_conv[l].transpose(1, 0, 2).reshape(1, (CONV_W - 1) * dec_b, ff)
        res = _ffn_call(xs, a_out, m_out, wo, g1_s, sc2_s, sh2_s, g2_s, n2, wu, wc, bc, wd,
                        hist, fg if last else None,
                        tm=rows_s, tiles_per_seq=1, shift=dec_b, per_row_mod=True, ch=ch, layer=l)
        xs, conv_s = res[0], res[1]
        if last:
            y_s = res[2]

        def s_kv(a):
            return _kv_out(a.reshape(dec_t, dec_b, KV_W).transpose(1, 0, 2), (dec_b, dec_t))

        for lst, arr in zip(outs_s, (s_kv(kvc), s_kv(kvs),
                                     kvw.reshape(dec_t, dec_b, 2, A_KV, A_HD).transpose(1, 2, 3, 4, 0),
                                     st_c, st_n, st_m.reshape(dec_b, M_HEADS),
                                     conv_s.reshape(CONV_W - 1, dec_b, ff).transpose(1, 0, 2))):
            lst.append(arr)

    y_prompt = y_p.reshape(batch, seq, d)
    y_sample = y_s.reshape(dec_t, dec_b, d).transpose(1, 0, 2)
    kvc_all, kvs_all, kvw_all = kv_stacks
    s_cmp, s_sel, s_win_new, *s_rest = [jnp.stack(a, axis=0) for a in outs_s]
    s_win = jnp.concatenate([cache_win, s_win_new], axis=-1)[..., -wb:].transpose(0, 1, 5, 2, 3, 4)
    return ((y_prompt, y_sample, kv_from_t(kvc_all), kv_from_t(kvs_all), kv_from_t(kvw_all))
            + tuple(jnp.stack(a, axis=0) for a in outs_p) + (s_cmp, s_sel, s_win) + tuple(s_rest))
```
